```python
import math
import jax, jax.numpy as jnp
from jax import lax
import numpy as np

D_MODEL = 2048
BATCH = 2
SEQ = 8192
DEPTH = 2

HEAD_DIM = 128
N_HEADS_A = D_MODEL // (2 * HEAD_DIM)
N_HEADS_B = D_MODEL // (2 * HEAD_DIM)
WIDTH_A = N_HEADS_A * HEAD_DIM
WIDTH_B = N_HEADS_B * HEAD_DIM
DILATED_BRANCHES = ((128, 1), (512, 4), (2048, 16))
DIL_BLOCK = 128
MOBA_BLOCK = 256
MOBA_TOPK = 3
MOBA_Q_CHUNK = 64
N_HEADS_C = D_MODEL // (2 * HEAD_DIM)
D_FF = 256 * (-(-(8 * D_MODEL // 3) // 256))
ROPE_THETA = 10000.0
NORM_EPS = 1e-6
SUBLN_EPS = 1e-5
DENSE_Q_BLOCK = 128
NEG_INF = -1e30
N_EVEN = (DEPTH + 1) // 2
N_ODD = DEPTH // 2

kernel_name = "hybrid_dilated_moba_diffattn_macaron"


def rmsnorm(x, g, eps=NORM_EPS):
    xf = x.astype(jnp.float32)
    y = xf * lax.rsqrt(jnp.mean(xf * xf, axis=-1, keepdims=True) + eps)
    return (y * g.astype(jnp.float32)).astype(x.dtype)


def rope(x):
    S, D = x.shape[-2], x.shape[-1]
    half = D // 2
    inv_freq = ROPE_THETA ** (-jnp.arange(half, dtype=jnp.float32) / half)
    ang = jnp.arange(S, dtype=jnp.float32)[:, None] * inv_freq[None, :]
    cos, sin = jnp.cos(ang), jnp.sin(ang)
    xf = x.astype(jnp.float32)
    x1, x2 = xf[..., :half], xf[..., half:]
    return jnp.concatenate([x1 * cos - x2 * sin, x2 * cos + x1 * sin], axis=-1).astype(x.dtype)


def swiglu(x, w_in, w_out):
    g, u = jnp.split(x @ w_in, 2, axis=-1)
    return (jax.nn.silu(g) * u) @ w_out


def dilated_branch(q, k, v, window, dilation):
    B, H, S, D = q.shape
    L = S // dilation
    reach = window // dilation
    Q = DIL_BLOCK
    nblk = -(-L // Q)
    Lp = nblk * Q

    def strided_blocks(t):
        t = t.reshape(B, H, L, dilation, D).transpose(0, 1, 3, 2, 4)
        t = jnp.pad(t, ((0, 0), (0, 0), (0, 0), (0, Lp - L), (0, 0)))
        return t.reshape(B, H, dilation, nblk, Q, D)

    def with_prev(t):
        prev = jnp.pad(t[:, :, :, :-1], ((0, 0), (0, 0), (0, 0), (1, 0), (0, 0), (0, 0)))
        return jnp.concatenate([prev, t], axis=4)

    qb = strided_blocks(q)
    kw = with_prev(strided_blocks(k))
    vw = with_prev(strided_blocks(v))
    qi = jnp.arange(Q)[:, None] + Q
    kj = jnp.arange(2 * Q)[None, :]
    dist = qi - kj
    blk = jnp.arange(nblk)[:, None, None]
    mask = (dist >= 0) & (dist <= reach) & (blk * Q + kj - Q >= 0)
    s = jnp.einsum('bhrnqd,bhrnkd->bhrnqk', qb, kw).astype(jnp.float32) * (D ** -0.5)
    s = jnp.where(mask, s, NEG_INF)
    m = jnp.max(s, axis=-1, keepdims=True)
    p = jnp.exp(s - m)
    den = jnp.sum(p, axis=-1, keepdims=True)
    o = jnp.einsum('bhrnqk,bhrnkd->bhrnqd', p / den, vw.astype(jnp.float32))
    lse = (m + jnp.log(den))[..., 0]
    o = o.reshape(B, H, dilation, Lp, D)[:, :, :, :L].transpose(0, 1, 3, 2, 4).reshape(B, H, S, D)
    lse = lse.reshape(B, H, dilation, Lp)[:, :, :, :L].transpose(0, 1, 3, 2).reshape(B, H, S)
    return o, lse


def dilated_mixture(q, k, v):
    outs, lses = zip(*[dilated_branch(q, k, v, w, d) for (w, d) in DILATED_BRANCHES])
    wts = jax.nn.softmax(jnp.stack(lses, axis=0), axis=0)
    return jnp.sum(wts[..., None] * jnp.stack(outs, axis=0), axis=0).astype(q.dtype)


def moba_attention(q, k, v):
    B, H, S, D = q.shape
    T = MOBA_BLOCK
    nb = -(-S // T)
    Sp = nb * T
    pad = ((0, 0), (0, 0), (0, Sp - S), (0, 0))
    kp, vp = jnp.pad(k, pad), jnp.pad(v, pad)
    kb, vb = kp.reshape(B, H, nb, T, D), vp.reshape(B, H, nb, T, D)
    k_mean = jnp.mean(kb.astype(jnp.float32), axis=3)
    q_blk = jnp.arange(S) // T
    past = jnp.arange(nb)[None, :] < q_blk[:, None]
    gate = jnp.einsum('bhsd,bhnd->bhsn', q.astype(jnp.float32), k_mean)
    gate = jnp.where(past, gate, NEG_INF)
    top = min(MOBA_TOPK, nb)
    _, sel = lax.top_k(gate, top)
    sel_ok = sel < q_blk[:, None]
    C = MOBA_Q_CHUNK
    nq = S // C

    def chunks(t):
        return jnp.moveaxis(t.reshape(B, H, nq, C, *t.shape[3:]), 2, 0)

    bi = jnp.arange(B)[:, None, None, None]
    hi = jnp.arange(H)[None, :, None, None]
    scale = D ** -0.5

    def step(args):
        qc, sc, okc, c = args
        k_sel = kb[bi, hi, sc]
        v_sel = vb[bi, hi, sc]
        own = (c * C) // T
        k_own = lax.dynamic_slice_in_dim(kp, own * T, T, axis=2)
        v_own = lax.dynamic_slice_in_dim(vp, own * T, T, axis=2)
        s_sel = jnp.einsum('bhqd,bhqjtd->bhqjt', qc, k_sel).astype(jnp.float32) * scale
        s_sel = jnp.where(okc[..., None], s_sel, NEG_INF).reshape(B, H, C, top * T)
        qpos = c * C + jnp.arange(C)
        kpos = own * T + jnp.arange(T)
        s_own = jnp.einsum('bhqd,bhtd->bhqt', qc, k_own).astype(jnp.float32) * scale
        s_own = jnp.where(kpos[None, :] <= qpos[:, None], s_own, NEG_INF)
        p = jax.nn.softmax(jnp.concatenate([s_sel, s_own], axis=-1), axis=-1).astype(v.dtype)
        p_sel = p[..., :top * T].reshape(B, H, C, top, T)
        p_own = p[..., top * T:]
        return (jnp.einsum('bhqjt,bhqjtd->bhqd', p_sel, v_sel)
                + jnp.einsum('bhqt,bhtd->bhqd', p_own, v_own))

    o = lax.map(step, (chunks(q), chunks(sel), chunks(sel_ok), jnp.arange(nq)))
    return jnp.moveaxis(o, 0, 2).reshape(B, H, S, D)


def even_mixer(h, w_in, w_out):
    B, S, _ = h.shape
    qa, ka, va, qb, kb, vb = jnp.split(
        h @ w_in, [WIDTH_A, 2 * WIDTH_A, 3 * WIDTH_A, 3 * WIDTH_A + WIDTH_B, 3 * WIDTH_A + 2 * WIDTH_B], axis=-1)

    def heads(t, n):
        return t.reshape(B, S, n, HEAD_DIM).transpose(0, 2, 1, 3)

    o_a = dilated_mixture(rope(heads(qa, N_HEADS_A)), rope(heads(ka, N_HEADS_A)), heads(va, N_HEADS_A))
    o_b = moba_attention(rope(heads(qb, N_HEADS_B)), rope(heads(kb, N_HEADS_B)), heads(vb, N_HEADS_B))
    o = jnp.concatenate([o_a, o_b.astype(o_a.dtype)], axis=1)
    return o.transpose(0, 2, 1, 3).reshape(B, S, WIDTH_A + WIDTH_B) @ w_out


def diff_attention(q1, q2, k1, k2, v, lam):
    B, H, S, d = q1.shape
    Q = DENSE_Q_BLOCK
    nq = S // Q
    kpos = jnp.arange(S)
    scale = d ** -0.5

    def blocks(t):
        return jnp.moveaxis(t.reshape(B, H, nq, Q, d), 2, 0)

    def step(args):
        a, b, c = args
        qpos = c * Q + jnp.arange(Q)
        mask = kpos[None, :] <= qpos[:, None]
        s1 = jnp.where(mask, jnp.einsum('bhqd,bhkd->bhqk', a, k1).astype(jnp.float32) * scale, NEG_INF)
        s2 = jnp.where(mask, jnp.einsum('bhqd,bhkd->bhqk', b, k2).astype(jnp.float32) * scale, NEG_INF)
        w = jax.nn.softmax(s1, axis=-1) - lam * jax.nn.softmax(s2, axis=-1)
        return jnp.einsum('bhqk,bhke->bhqe', w.astype(v.dtype), v)

    o = lax.map(step, (blocks(q1), blocks(q2), jnp.arange(nq)))
    return jnp.moveaxis(o, 0, 2).reshape(B, H, S, 2 * d)


def diff_mixer(h, w_in, w_out, lq1, lk1, lq2, lk2, subln_g, lambda_init):
    B, S, _ = h.shape
    H, d = N_HEADS_C, HEAD_DIM
    q, k, v = jnp.split(h @ w_in, 3, axis=-1)
    q = rope(q.reshape(B, S, H, 2, d).transpose(0, 2, 3, 1, 4))
    k = rope(k.reshape(B, S, H, 2, d).transpose(0, 2, 3, 1, 4))
    v = v.reshape(B, S, H, 2 * d).transpose(0, 2, 1, 3)
    f32 = jnp.float32
    lam = (jnp.exp(jnp.sum(lq1.astype(f32) * lk1.astype(f32)))
           - jnp.exp(jnp.sum(lq2.astype(f32) * lk2.astype(f32))) + lambda_init)
    o = diff_attention(q[:, :, 0], q[:, :, 1], k[:, :, 0], k[:, :, 1], v, lam)
    o = rmsnorm(o, subln_g, SUBLN_EPS) * (1.0 - lambda_init)
    return o.transpose(0, 2, 1, 3).reshape(B, S, H * 2 * d) @ w_out


def lambda_init_fn(layer):
    return 0.8 - 0.6 * math.exp(-0.3 * layer)


def setup_inputs(seed: int = 0) -> dict:
    key = jax.random.key(seed)
    ks = iter(jax.random.split(key, 24))

    def nrm(shape, fan_in):
        return jax.random.normal(next(ks), shape, jnp.float32) * fan_in ** -0.5

    def gain(shape):
        return 1.0 + 0.02 * jax.random.normal(next(ks), shape, jnp.float32)

    def small(shape):
        return 0.1 * jax.random.normal(next(ks), shape, jnp.float32)

    w_even = 3 * (WIDTH_A + WIDTH_B)
    w_odd = 3 * N_HEADS_C * 2 * HEAD_DIM
    return {
        "x": jax.random.normal(next(ks), (BATCH, SEQ, D_MODEL), jnp.float32),
        "ffa_norm": gain((DEPTH, D_MODEL)),
        "ffa_w_in": nrm((DEPTH, D_MODEL, 2 * D_FF), D_MODEL),
        "ffa_w_out": nrm((DEPTH, D_FF, D_MODEL), D_FF),
        "mix_norm": gain((DEPTH, D_MODEL)),
        "even_w_in": nrm((N_EVEN, D_MODEL, w_even), D_MODEL),
        "even_w_out": nrm((N_EVEN, WIDTH_A + WIDTH_B, D_MODEL), WIDTH_A + WIDTH_B),
        "odd_w_in": nrm((N_ODD, D_MODEL, w_odd), D_MODEL),
        "odd_w_out": nrm((N_ODD, N_HEADS_C * 2 * HEAD_DIM, D_MODEL), N_HEADS_C * 2 * HEAD_DIM),
        "lambda_q1": small((N_ODD, HEAD_DIM)),
        "lambda_k1": small((N_ODD, HEAD_DIM)),
        "lambda_q2": small((N_ODD, HEAD_DIM)),
        "lambda_k2": small((N_ODD, HEAD_DIM)),
        "subln_norm": gain((N_ODD, 2 * HEAD_DIM)),
        "ffb_norm": gain((DEPTH, D_MODEL)),
        "ffb_w_in": nrm((DEPTH, D_MODEL, 2 * D_FF), D_MODEL),
        "ffb_w_out": nrm((DEPTH, D_FF, D_MODEL), D_FF),
        "final_norm": gain((D_MODEL,)),
    }


def reference(x, ffa_norm, ffa_w_in, ffa_w_out, mix_norm, even_w_in, even_w_out, odd_w_in, odd_w_out,
              lambda_q1, lambda_k1, lambda_q2, lambda_k2, subln_norm, ffb_norm, ffb_w_in, ffb_w_out,
              final_norm):
    h = x
    for l in range(DEPTH):
        i = l // 2
        h = h + 0.5 * swiglu(rmsnorm(h, ffa_norm[l]), ffa_w_in[l], ffa_w_out[l])
        hn = rmsnorm(h, mix_norm[l])
        if l % 2 == 0:
            h = h + even_mixer(hn, even_w_in[i], even_w_out[i])
        else:
            h = h + diff_mixer(hn, odd_w_in[i], odd_w_out[i], lambda_q1[i], lambda_k1[i],
                               lambda_q2[i], lambda_k2[i], subln_norm[i], lambda_init_fn(l))
        h = h + 0.5 * swiglu(rmsnorm(h, ffb_norm[l]), ffb_w_in[l], ffb_w_out[l])
    return rmsnorm(h, final_norm)
```

```python
import functools
import math

import jax
import jax.numpy as jnp
from jax import lax
from jax.experimental import pallas as pl
from jax.experimental.pallas import tpu as pltpu

F32 = jnp.float32
BF16 = jnp.bfloat16

D_MODEL = 2048
HEAD_DIM = 128
HALF_DIM = HEAD_DIM // 2
N_HEADS_A = 8
N_HEADS_B = 8
N_HEADS_C = 8
WIDTH_A = N_HEADS_A * HEAD_DIM
WIDTH_B = N_HEADS_B * HEAD_DIM
DILATIONS = (1, 4, 16)
DIL_BLOCK = 128
DIL_REACH = 128
DIL_SUPER = DIL_BLOCK * DILATIONS[-1]
MOBA_BLOCK = 256
MOBA_TOPK = 3
DIFF_BLOCK = 512
ROPE_THETA = 10000.0
NORM_EPS = 1e-6
SUBLN_EPS = 1e-5
NEG_INF = -1e30
SCALE = HEAD_DIM ** -0.5

VMEM_V7X_BYTES = 64 * 1024 * 1024
NT_DIMS = (((1,), (1,)), ((), ()))


def _params(semantics, vmem_bytes):
    assert vmem_bytes < VMEM_V7X_BYTES
    return pltpu.CompilerParams(dimension_semantics=semantics, vmem_limit_bytes=int(vmem_bytes))


def _rms(x, g, eps):
    return x * lax.rsqrt(jnp.mean(x * x, axis=-1, keepdims=True) + eps) * g


FFN_ROWS = 512
FFN_COLS = 512


def _ffn_body(x_ref, g_ref, wg_ref, wu_ref, wo_ref, *rest, final):
    if final:
        fg_ref, o_ref, xn_ref = rest
    else:
        o_ref, xn_ref = rest
    j = pl.program_id(1)

    @pl.when(j == 0)
    def _():
        x = x_ref[...]
        xn_ref[...] = _rms(x, g_ref[...], NORM_EPS).astype(BF16)
        o_ref[...] = x

    xn = xn_ref[...]
    gate = jnp.dot(xn, wg_ref[...], preferred_element_type=F32)
    up = jnp.dot(xn, wu_ref[...], preferred_element_type=F32)
    act = (0.5 * gate / (1.0 + jnp.exp(-gate))) * up
    o_ref[...] += jnp.dot(act.astype(BF16), wo_ref[...], preferred_element_type=F32)

    if final:
        @pl.when(j == pl.num_programs(1) - 1)
        def _():
            o_ref[...] = _rms(o_ref[...], fg_ref[...], NORM_EPS)


def _ffn(h, g, w_in, w_out, final_g=None):
    n, d = h.shape
    d_ff = w_out.shape[0]
    tm, tf = FFN_ROWS, FFN_COLS
    nff = d_ff // tf
    assert n % tm == 0 and d_ff % tf == 0
    final = final_g is not None
    in_specs = [
        pl.BlockSpec((tm, d), lambda i, j: (i, 0)),
        pl.BlockSpec((1, d), lambda i, j: (0, 0)),
        pl.BlockSpec((d, tf), lambda i, j: (0, j)),
        pl.BlockSpec((d, tf), lambda i, j: (0, nff + j)),
        pl.BlockSpec((tf, d), lambda i, j: (j, 0)),
    ]
    args = [h, g.reshape(1, d), w_in, w_in, w_out]
    if final:
        in_specs.append(pl.BlockSpec((1, d), lambda i, j: (0, 0)))
        args.append(final_g.reshape(1, d))
    vmem = 2 * tm * d * 4 * 2 + tm * d * 2 + 2 * (2 * d * tf + tf * d) * 2 + 4 * tm * tf * 4 + (8 << 20)
    return pl.pallas_call(
        functools.partial(_ffn_body, final=final),
        grid=(n // tm, nff),
        in_specs=in_specs,
        out_specs=pl.BlockSpec((tm, d), lambda i, j: (i, 0)),
        out_shape=jax.ShapeDtypeStruct((n, d), F32),
        scratch_shapes=[pltpu.VMEM((tm, d), BF16)],
        compiler_params=_params(("parallel", "arbitrary"), vmem),
        name="ffn",
    )(*args)


PROJ_ROWS = 512
PROJ_COLS = 512


def _in_range(j, lo, hi):
    return jnp.logical_and(j >= lo, j < hi)


def _proj_body(x_ref, g_ref, w_ref, cos_ref, sin_ref, o_ref, xn_ref, *, q_tiles, k_tiles):
    j = pl.program_id(1)

    @pl.when(j == 0)
    def _():
        xn_ref[...] = _rms(x_ref[...], g_ref[...], NORM_EPS).astype(BF16)

    y = jnp.dot(xn_ref[...], w_ref[...], preferred_element_type=F32)
    is_q = _in_range(j, *q_tiles)
    is_rope = jnp.logical_or(is_q, _in_range(j, *k_tiles))

    @pl.when(is_rope)
    def _():
        cos = cos_ref[...]
        sin = sin_ref[...]
        scale = jnp.where(is_q, SCALE, 1.0).astype(F32)
        for c in range(y.shape[1] // HEAD_DIM):
            yh = y[:, c * HEAD_DIM:(c + 1) * HEAD_DIM]
            rot = yh * cos + pltpu.roll(yh, HALF_DIM, 1) * sin
            o_ref[:, c * HEAD_DIM:(c + 1) * HEAD_DIM] = (rot * scale).astype(o_ref.dtype)

    @pl.when(jnp.logical_not(is_rope))
    def _():
        o_ref[...] = y.astype(o_ref.dtype)


def _proj(h, g, w, cos, sin, q_cols, k_cols, out_dtype):
    n, d = h.shape
    width = w.shape[1]
    seq = cos.shape[0]
    tm, tn = PROJ_ROWS, PROJ_COLS
    assert n % tm == 0 and width % tn == 0 and seq % tm == 0
    assert all(c % tn == 0 for c in (*q_cols, *k_cols))
    pos_blocks = seq // tm
    body = functools.partial(
        _proj_body,
        q_tiles=(q_cols[0] // tn, q_cols[1] // tn),
        k_tiles=(k_cols[0] // tn, k_cols[1] // tn),
    )
    vmem = 2 * tm * d * 4 + tm * d * 2 + 2 * d * tn * 2 + 4 * tm * tn * 4 + 4 * tm * HEAD_DIM * 4 + (8 << 20)
    return pl.pallas_call(
        body,
        grid=(n // tm, width // tn),
        in_specs=[
            pl.BlockSpec((tm, d), lambda i, j: (i, 0)),
            pl.BlockSpec((1, d), lambda i, j: (0, 0)),
            pl.BlockSpec((d, tn), lambda i, j: (0, j)),
            pl.BlockSpec((tm, HEAD_DIM), lambda i, j: (i % pos_blocks, 0)),
            pl.BlockSpec((tm, HEAD_DIM), lambda i, j: (i % pos_blocks, 0)),
        ],
        out_specs=pl.BlockSpec((tm, tn), lambda i, j: (i, j)),
        out_shape=jax.ShapeDtypeStruct((n, width), out_dtype),
        scratch_shapes=[pltpu.VMEM((tm, d), BF16)],
        compiler_params=_params(("parallel", "arbitrary"), vmem),
        name="proj",
    )(h, g.reshape(1, d), w, cos, sin)


def _rope_tables(seq):
    inv_freq = ROPE_THETA ** (-jnp.arange(HALF_DIM, dtype=F32) / HALF_DIM)
    ang = jnp.arange(seq, dtype=F32)[:, None] * inv_freq[None, :]
    cos, sin = jnp.cos(ang), jnp.sin(ang)
    return jnp.concatenate([cos, cos], axis=-1), jnp.concatenate([-sin, sin], axis=-1)


OPROJ_ROWS = 1024
OPROJ_COLS = 1024


def _oproj_body(*refs):
    *aw, h_ref, o_ref = refs
    acc = h_ref[...]
    for a_ref, w_ref in zip(aw[0::2], aw[1::2]):
        acc = acc + jnp.dot(a_ref[...], w_ref[...], preferred_element_type=F32)
    o_ref[...] = acc


def _oproj(h, pairs):
    n, d = h.shape
    tm, tn = OPROJ_ROWS, OPROJ_COLS
    assert n % tm == 0 and d % tn == 0
    in_specs, args, vmem = [], [], 0
    for a, w in pairs:
        kdim = a.shape[1]
        in_specs += [pl.BlockSpec((tm, kdim), lambda i, j: (i, 0)),
                     pl.BlockSpec((kdim, tn), lambda i, j: (0, j))]
        args += [a, w]
        vmem += 2 * (tm * kdim + kdim * tn) * 2
    in_specs.append(pl.BlockSpec((tm, tn), lambda i, j: (i, j)))
    args.append(h)
    vmem += 6 * tm * tn * 4 + (8 << 20)
    return pl.pallas_call(
        _oproj_body,
        grid=(n // tm, d // tn),
        in_specs=in_specs,
        out_specs=pl.BlockSpec((tm, tn), lambda i, j: (i, j)),
        out_shape=jax.ShapeDtypeStruct((n, d), F32),
        compiler_params=_params(("parallel", "parallel"), vmem),
        name="oproj",
    )(*args)


def _dilated_body(q_ref, kc_ref, kp_ref, vc_ref, vp_ref, o_ref, kbuf, vbuf, *branch_bufs):
    obufs, lbufs = branch_bufs[:3], branch_bufs[3:]
    sb = pl.program_id(2)
    sup = DIL_SUPER
    kbuf[0:sup, :] = kp_ref[...]
    kbuf[sup:2 * sup, :] = kc_ref[...]
    vbuf[0:sup, :] = vp_ref[...]
    vbuf[sup:2 * sup, :] = vc_ref[...]

    row = lax.broadcasted_iota(jnp.int32, (DIL_BLOCK, 2 * DIL_BLOCK), 0)
    col = lax.broadcasted_iota(jnp.int32, (DIL_BLOCK, 2 * DIL_BLOCK), 1)
    band = jnp.logical_and(col >= row + (DIL_BLOCK - DIL_REACH), col <= row + DIL_BLOCK)

    for br, dil in enumerate(DILATIONS):
        nsub = sup // (DIL_BLOCK * dil)
        obuf, lbuf = obufs[br], lbufs[br]

        def block(t, carry, dil=dil, nsub=nsub, obuf=obuf, lbuf=lbuf):
            res = t // nsub
            sub = t % nsub
            q0 = res + dil * DIL_BLOCK * sub
            k0 = sup + q0 - dil * DIL_BLOCK
            if dil == 1:
                q0 = pl.multiple_of(q0, DIL_BLOCK)
                k0 = pl.multiple_of(k0, DIL_BLOCK)
                qs, ks = pl.ds(q0, DIL_BLOCK), pl.ds(k0, 2 * DIL_BLOCK)
            else:
                qs, ks = pl.ds(q0, DIL_BLOCK, stride=dil), pl.ds(k0, 2 * DIL_BLOCK, stride=dil)
            qb = q_ref[qs, :].astype(BF16)
            kb = kbuf[ks, :].astype(BF16)
            vb = vbuf[ks, :].astype(BF16)
            s = lax.dot_general(qb, kb, NT_DIMS, preferred_element_type=F32)
            has_prev = jnp.logical_or(sb > 0, sub > 0)
            first_col = jnp.where(has_prev, 0, DIL_BLOCK)
            s = jnp.where(jnp.logical_and(band, col >= first_col), s, NEG_INF)
            m = jnp.max(s, axis=-1, keepdims=True)
            p = jnp.exp(s - m)
            den = jnp.sum(p, axis=-1, keepdims=True)
            o = jnp.dot(p.astype(BF16), vb, preferred_element_type=F32) / den
            obuf[qs, :] = o
            lbuf[qs, :] = jnp.broadcast_to(m + jnp.log(den), (DIL_BLOCK, HEAD_DIM))
            return carry

        lax.fori_loop(0, nsub * dil, block, 0)

    lses = [lbuf[...] for lbuf in lbufs]
    top = jnp.maximum(jnp.maximum(lses[0], lses[1]), lses[2])
    wts = [jnp.exp(l - top) for l in lses]
    mix = wts[0] * obufs[0][...] + wts[1] * obufs[1][...] + wts[2] * obufs[2][...]
    o_ref[...] = (mix / (wts[0] + wts[1] + wts[2])).astype(o_ref.dtype)


def _dilated(qkv, batch, seq):
    sup = DIL_SUPER
    assert seq % sup == 0
    nsb = seq // sup
    heads = N_HEADS_A

    def cur(col0):
        return pl.BlockSpec((sup, HEAD_DIM), lambda b, h, s: (b * nsb + s, col0 + h))

    def prev(col0):
        return pl.BlockSpec((sup, HEAD_DIM), lambda b, h, s: (b * nsb + jnp.maximum(s - 1, 0), col0 + h))

    blk = sup * HEAD_DIM * 4
    vmem = 2 * 5 * blk + 2 * sup * HEAD_DIM * 2 + 4 * blk + 6 * blk + 8 * blk + (8 << 20)
    return pl.pallas_call(
        _dilated_body,
        grid=(batch, heads, nsb),
        in_specs=[cur(0), cur(heads), prev(heads), cur(2 * heads), prev(2 * heads)],
        out_specs=pl.BlockSpec((sup, HEAD_DIM), lambda b, h, s: (b * nsb + s, h)),
        out_shape=jax.ShapeDtypeStruct((batch * seq, WIDTH_A), BF16),
        scratch_shapes=[pltpu.VMEM((2 * sup, HEAD_DIM), F32)] * 2 + [pltpu.VMEM((sup, HEAD_DIM), F32)] * 6,
        compiler_params=_params(("parallel", "parallel", "arbitrary"), vmem),
        name="dilated",
    )(qkv, qkv, qkv, qkv, qkv)


def _softmax_step(s, vb, m, l, acc):
    m_new = jnp.maximum(m, jnp.max(s, axis=-1, keepdims=True))
    alpha = jnp.exp(m - m_new)
    p = jnp.exp(s - m_new)
    l_new = alpha * l + jnp.sum(p, axis=-1, keepdims=True)
    acc_new = alpha * acc + jnp.dot(p.astype(BF16), vb, preferred_element_type=F32)
    return m_new, l_new, acc_new


def _moba_body(q_ref, k_ref, v_ref, o_ref, kmean_hi, kmean_lo):
    qi = pl.program_id(2)
    blk = MOBA_BLOCK
    nb = k_ref.shape[0] // blk

    @pl.when(qi == 0)
    def _():
        for n in range(nb):
            mean = jnp.mean(k_ref[n * blk:(n + 1) * blk, :].astype(F32), axis=0, keepdims=True)
            hi = mean.astype(BF16)
            kmean_hi[n:n + 1, :] = hi
            kmean_lo[n:n + 1, :] = (mean - hi.astype(F32)).astype(BF16)

    q = q_ref[...]
    gate = (lax.dot_general(q, kmean_hi[...], NT_DIMS, preferred_element_type=F32)
            + lax.dot_general(q, kmean_lo[...], NT_DIMS, preferred_element_type=F32))
    blk_id = lax.broadcasted_iota(jnp.int32, gate.shape, 1)
    past = blk_id < qi
    work = jnp.where(past, gate, NEG_INF)
    chosen = jnp.zeros(gate.shape, F32)
    for _ in range(MOBA_TOPK):
        best = jnp.max(work, axis=-1, keepdims=True)
        first = jnp.min(jnp.where(work == best, blk_id, nb), axis=-1, keepdims=True)
        pick = blk_id == first
        chosen = jnp.where(pick, 1.0, chosen)
        work = jnp.where(pick, -jnp.inf, work)
    chosen = jnp.where(past, chosen, 0.0).astype(BF16)

    row = lax.broadcasted_iota(jnp.int32, (blk, blk), 0)
    col = lax.broadcasted_iota(jnp.int32, (blk, blk), 1)
    own = pl.multiple_of(qi * blk, blk)
    s = lax.dot_general(q, k_ref[pl.ds(own, blk), :], NT_DIMS, preferred_element_type=F32)
    s = jnp.where(col <= row, s, NEG_INF)
    m = jnp.max(s, axis=-1, keepdims=True)
    p = jnp.exp(s - m)
    l = jnp.sum(p, axis=-1, keepdims=True)
    acc = jnp.dot(p.astype(BF16), v_ref[pl.ds(own, blk), :], preferred_element_type=F32)

    sel_row = lax.broadcasted_iota(jnp.int32, (nb, blk), 0)

    def past_block(j, carry):
        m, l, acc = carry
        start = pl.multiple_of(j * blk, blk)
        kb = k_ref[pl.ds(start, blk), :]
        vb = v_ref[pl.ds(start, blk), :]
        s = lax.dot_general(q, kb, NT_DIMS, preferred_element_type=F32)
        spread = jnp.where(sel_row == j, 1.0, 0.0).astype(BF16)
        picked = jnp.dot(chosen, spread, preferred_element_type=F32)
        s = jnp.where(picked > 0.5, s, NEG_INF)
        return _softmax_step(s, vb, m, l, acc)

    m, l, acc = lax.fori_loop(0, qi, past_block, (m, l, acc))
    o_ref[...] = (acc / l).astype(o_ref.dtype)


def _moba(qkv, batch, seq):
    blk = MOBA_BLOCK
    assert seq % blk == 0
    nq = seq // blk
    heads = N_HEADS_B
    vmem = 2 * 2 * seq * HEAD_DIM * 2 + 4 * blk * HEAD_DIM * 2 + 16 * blk * blk * 4 + (8 << 20)
    return pl.pallas_call(
        _moba_body,
        grid=(batch, heads, nq),
        in_specs=[
            pl.BlockSpec((blk, HEAD_DIM), lambda b, h, i: (b * nq + i, h)),
            pl.BlockSpec((seq, HEAD_DIM), lambda b, h, i: (b, heads + h)),
            pl.BlockSpec((seq, HEAD_DIM), lambda b, h, i: (b, 2 * heads + h)),
        ],
        out_specs=pl.BlockSpec((blk, HEAD_DIM), lambda b, h, i: (b * nq + i, h)),
        out_shape=jax.ShapeDtypeStruct((batch * seq, WIDTH_B), BF16),
        scratch_shapes=[pltpu.VMEM((seq // blk, HEAD_DIM), BF16)] * 2,
        compiler_params=_params(("parallel", "parallel", "arbitrary"), vmem),
        name="moba",
    )(qkv, qkv, qkv)


def _diff_body(lq1_ref, lk1_ref, lq2_ref, lk2_ref, g_ref, q1_ref, q2_ref, k1_ref, k2_ref, v_ref, o_ref,
               m1, l1, acc1, m2, l2, acc2, *, lambda_init):
    qi = pl.program_id(2)
    t = DIFF_BLOCK
    streams = ((q1_ref, k1_ref, m1, l1, acc1), (q2_ref, k2_ref, m2, l2, acc2))

    row = lax.broadcasted_iota(jnp.int32, (t, t), 0)
    col = lax.broadcasted_iota(jnp.int32, (t, t), 1)
    own = pl.multiple_of(qi * t, t)
    v_own = v_ref[pl.ds(own, t), :]
    for q_ref, k_ref, m, l, acc in streams:
        s = lax.dot_general(q_ref[...], k_ref[pl.ds(own, t), :], NT_DIMS, preferred_element_type=F32)
        s = jnp.where(col <= row, s, NEG_INF)
        m0 = jnp.max(s, axis=-1, keepdims=True)
        p = jnp.exp(s - m0)
        m[...] = m0
        l[...] = jnp.sum(p, axis=-1, keepdims=True)
        acc[...] = jnp.dot(p.astype(BF16), v_own, preferred_element_type=F32)

    def past_block(j, carry):
        start = pl.multiple_of(j * t, t)
        vb = v_ref[pl.ds(start, t), :]
        for q_ref, k_ref, m, l, acc in streams:
            s = lax.dot_general(q_ref[...], k_ref[pl.ds(start, t), :], NT_DIMS, preferred_element_type=F32)
            m_new, l_new, acc_new = _softmax_step(s, vb, m[...], l[...], acc[...])
            m[...] = m_new
            l[...] = l_new
            acc[...] = acc_new
        return carry

    lax.fori_loop(0, qi, past_block, 0)

    lam = (jnp.exp(jnp.sum(lq1_ref[...] * lk1_ref[...], axis=-1, keepdims=True))
           - jnp.exp(jnp.sum(lq2_ref[...] * lk2_ref[...], axis=-1, keepdims=True)) + lambda_init)
    o = acc1[...] / l1[...] - lam * (acc2[...] / l2[...])
    o_ref[...] = (_rms(o, g_ref[...], SUBLN_EPS) * (1.0 - lambda_init)).astype(o_ref.dtype)


def _diff(qkv, lq1, lk1, lq2, lk2, subln_g, lambda_init, batch, seq):
    t = DIFF_BLOCK
    assert seq % t == 0
    nq = seq // t
    heads = N_HEADS_C
    dv = 2 * HEAD_DIM
    kcol = D_MODEL // HEAD_DIM
    vcol = 2 * D_MODEL // dv

    def vec(width):
        return pl.BlockSpec((1, width), lambda b, h, i: (0, 0))

    def q_spec(part):
        return pl.BlockSpec((t, HEAD_DIM), lambda b, h, i: (b * nq + i, 2 * h + part))

    def k_spec(part):
        return pl.BlockSpec((seq, HEAD_DIM), lambda b, h, i: (b, kcol + 2 * h + part))

    vmem = (2 * (2 * seq * HEAD_DIM + seq * dv) * 2 + 8 * t * HEAD_DIM * 2 + 2 * t * dv * 2
            + 2 * (2 * t * HEAD_DIM + t * dv) * 4 + 10 * t * t * 4 + (8 << 20))
    return pl.pallas_call(
        functools.partial(_diff_body, lambda_init=lambda_init),
        grid=(batch, heads, nq),
        in_specs=[vec(HEAD_DIM)] * 4 + [vec(dv), q_spec(0), q_spec(1), k_spec(0), k_spec(1),
                                        pl.BlockSpec((seq, dv), lambda b, h, i: (b, vcol + h))],
        out_specs=pl.BlockSpec((t, dv), lambda b, h, i: (b * nq + i, h)),
        out_shape=jax.ShapeDtypeStruct((batch * seq, heads * dv), BF16),
        scratch_shapes=[pltpu.VMEM((t, 1), F32), pltpu.VMEM((t, 1), F32), pltpu.VMEM((t, dv), F32)] * 2,
        compiler_params=_params(("parallel", "parallel", "arbitrary"), vmem),
        name="diffattn",
    )(lq1.reshape(1, -1), lk1.reshape(1, -1), lq2.reshape(1, -1), lk2.reshape(1, -1),
      subln_g.reshape(1, -1), qkv, qkv, qkv, qkv, qkv)


def _lambda_init(layer):
    return 0.8 - 0.6 * math.exp(-0.3 * layer)


def kernel(x, ffa_norm, ffa_w_in, ffa_w_out, mix_norm, even_w_in, even_w_out, odd_w_in, odd_w_out,
           lambda_q1, lambda_k1, lambda_q2, lambda_k2, subln_norm, ffb_norm, ffb_w_in, ffb_w_out, final_norm):
    batch, seq, d = x.shape
    depth = ffa_norm.shape[0]
    cos, sin = _rope_tables(seq)
    h = x.reshape(batch * seq, d)
    for layer in range(depth):
        i = layer // 2
        h = _ffn(h, ffa_norm[layer], ffa_w_in[layer].astype(BF16), ffa_w_out[layer].astype(BF16))
        if layer % 2 == 0:
            w_in = even_w_in[i].astype(BF16)
            w_out = even_w_out[i].astype(BF16)
            wa = 3 * WIDTH_A
            qkv_a = _proj(h, mix_norm[layer], w_in[:, :wa], cos, sin,
                          (0, WIDTH_A), (WIDTH_A, 2 * WIDTH_A), F32)
            qkv_b = _proj(h, mix_norm[layer], w_in[:, wa:], cos, sin,
                          (0, WIDTH_B), (WIDTH_B, 2 * WIDTH_B), BF16)
            o_a = _dilated(qkv_a, batch, seq)
            o_b = _moba(qkv_b, batch, seq)
            h = _oproj(h, [(o_a, w_out[:WIDTH_A]), (o_b, w_out[WIDTH_A:])])
        else:
            qkv = _proj(h, mix_norm[layer], odd_w_in[i].astype(BF16), cos, sin,
                        (0, D_MODEL), (D_MODEL, 2 * D_MODEL), BF16)
            o = _diff(qkv, lambda_q1[i], lambda_k1[i], lambda_q2[i], lambda_k2[i], subln_norm[i],
                      _lambda_init(layer), batch, seq)
            h = _oproj(h, [(o, odd_w_out[i].astype(BF16))])
        last = layer == depth - 1
        h = _ffn(h, ffb_norm[layer], ffb_w_in[layer].astype(BF16), ffb_w_out[layer].astype(BF16),
                 final_norm if last else None)
    return h.reshape(batch, seq, d)
```

```python
import functools
import math

import jax
import jax.numpy as jnp
from jax import lax
from jax.experimental import pallas as pl
from jax.experimental.pallas import tpu as pltpu

F32 = jnp.float32
BF16 = jnp.bfloat16

D_MODEL = 2048
HEAD_DIM = 128
HALF_DIM = HEAD_DIM // 2
N_HEADS_A = 8
N_HEADS_B = 8
N_HEADS_C = 8
WIDTH_A = N_HEADS_A * HEAD_DIM
WIDTH_B = N_HEADS_B * HEAD_DIM
DILATIONS = (1, 4, 16)
DIL_BLOCK = 128
DIL_REACH = 128
DIL_SUPER = DIL_BLOCK * DILATIONS[-1]
MOBA_BLOCK = 256
MOBA_TOPK = 3
DIFF_BLOCK = 512
ROPE_THETA = 10000.0
NORM_EPS = 1e-6
SUBLN_EPS = 1e-5
NEG_INF = -1e30
SCALE = HEAD_DIM ** -0.5
LOG2_E = math.log2(math.e)

VMEM_V7X_BYTES = 64 * 1024 * 1024
NT_DIMS = (((1,), (1,)), ((), ()))


def _params(semantics, vmem_bytes):
    assert vmem_bytes < VMEM_V7X_BYTES
    return pltpu.CompilerParams(dimension_semantics=semantics, vmem_limit_bytes=int(vmem_bytes))


def _rms(x, g, eps):
    return x * lax.rsqrt(jnp.mean(x * x, axis=-1, keepdims=True) + eps) * g


FFN_ROWS = 512
FFN_COLS = 512


def _ffn_body(x_ref, g_ref, wg_ref, wu_ref, wo_ref, *rest, final):
    if final:
        fg_ref, o_ref, xn_ref = rest
    else:
        o_ref, xn_ref = rest
    j = pl.program_id(1)

    @pl.when(j == 0)
    def _():
        x = x_ref[...]
        xn_ref[...] = _rms(x, g_ref[...], NORM_EPS).astype(BF16)
        o_ref[...] = x

    xn = xn_ref[...]
    gate = jnp.dot(xn, wg_ref[...], preferred_element_type=F32)
    up = jnp.dot(xn, wu_ref[...], preferred_element_type=F32)
    act = (0.5 * gate / (1.0 + jnp.exp(-gate))) * up
    o_ref[...] += jnp.dot(act.astype(BF16), wo_ref[...], preferred_element_type=F32)

    if final:
        @pl.when(j == pl.num_programs(1) - 1)
        def _():
            o_ref[...] = _rms(o_ref[...], fg_ref[...], NORM_EPS)


def _ffn(h, g, w_in, w_out, final_g=None):
    n, d = h.shape
    d_ff = w_out.shape[0]
    tm, tf = FFN_ROWS, FFN_COLS
    nff = d_ff // tf
    assert n % tm == 0 and d_ff % tf == 0
    final = final_g is not None
    in_specs = [
        pl.BlockSpec((tm, d), lambda i, j: (i, 0)),
        pl.BlockSpec((1, d), lambda i, j: (0, 0)),
        pl.BlockSpec((d, tf), lambda i, j: (0, j)),
        pl.BlockSpec((d, tf), lambda i, j: (0, nff + j)),
        pl.BlockSpec((tf, d), lambda i, j: (j, 0)),
    ]
    args = [h, g.reshape(1, d), w_in, w_in, w_out]
    if final:
        in_specs.append(pl.BlockSpec((1, d), lambda i, j: (0, 0)))
        args.append(final_g.reshape(1, d))
    vmem = 2 * tm * d * 4 * 2 + tm * d * 2 + 2 * (2 * d * tf + tf * d) * 2 + 4 * tm * tf * 4 + (8 << 20)
    return pl.pallas_call(
        functools.partial(_ffn_body, final=final),
        grid=(n // tm, nff),
        in_specs=in_specs,
        out_specs=pl.BlockSpec((tm, d), lambda i, j: (i, 0)),
        out_shape=jax.ShapeDtypeStruct((n, d), F32),
        scratch_shapes=[pltpu.VMEM((tm, d), BF16)],
        compiler_params=_params(("parallel", "arbitrary"), vmem),
        name="ffn",
    )(*args)


PROJ_ROWS = 512
PROJ_COLS = 512


def _in_range(j, lo, hi):
    return jnp.logical_and(j >= lo, j < hi)


def _proj_body(x_ref, g_ref, w_ref, cos_ref, sin_ref, o_ref, xn_ref, *, q_tiles, k_tiles, q_scale):
    j = pl.program_id(1)

    @pl.when(j == 0)
    def _():
        xn_ref[...] = _rms(x_ref[...], g_ref[...], NORM_EPS).astype(BF16)

    y = jnp.dot(xn_ref[...], w_ref[...], preferred_element_type=F32)
    is_q = _in_range(j, *q_tiles)
    is_rope = jnp.logical_or(is_q, _in_range(j, *k_tiles))

    @pl.when(is_rope)
    def _():
        cos = cos_ref[...]
        sin = sin_ref[...]
        scale = jnp.where(is_q, q_scale, 1.0).astype(F32)
        for c in range(y.shape[1] // HEAD_DIM):
            yh = y[:, c * HEAD_DIM:(c + 1) * HEAD_DIM]
            rot = yh * cos + pltpu.roll(yh, HALF_DIM, 1) * sin
            o_ref[:, c * HEAD_DIM:(c + 1) * HEAD_DIM] = (rot * scale).astype(o_ref.dtype)

    @pl.when(jnp.logical_not(is_rope))
    def _():
        o_ref[...] = y.astype(o_ref.dtype)


def _proj(h, g, w, cos, sin, q_cols, k_cols, q_scale, out_dtype):
    n, d = h.shape
    width = w.shape[1]
    seq = cos.shape[0]
    tm, tn = PROJ_ROWS, PROJ_COLS
    assert n % tm == 0 and width % tn == 0 and seq % tm == 0
    assert all(c % tn == 0 for c in (*q_cols, *k_cols))
    pos_blocks = seq // tm
    body = functools.partial(
        _proj_body,
        q_tiles=(q_cols[0] // tn, q_cols[1] // tn),
        k_tiles=(k_cols[0] // tn, k_cols[1] // tn),
        q_scale=q_scale,
    )
    vmem = 2 * tm * d * 4 + tm * d * 2 + 2 * d * tn * 2 + 4 * tm * tn * 4 + 4 * tm * HEAD_DIM * 4 + (8 << 20)
    return pl.pallas_call(
        body,
        grid=(n // tm, width // tn),
        in_specs=[
            pl.BlockSpec((tm, d), lambda i, j: (i, 0)),
            pl.BlockSpec((1, d), lambda i, j: (0, 0)),
            pl.BlockSpec((d, tn), lambda i, j: (0, j)),
            pl.BlockSpec((tm, HEAD_DIM), lambda i, j: (i % pos_blocks, 0)),
            pl.BlockSpec((tm, HEAD_DIM), lambda i, j: (i % pos_blocks, 0)),
        ],
        out_specs=pl.BlockSpec((tm, tn), lambda i, j: (i, j)),
        out_shape=jax.ShapeDtypeStruct((n, width), out_dtype),
        scratch_shapes=[pltpu.VMEM((tm, d), BF16)],
        compiler_params=_params(("parallel", "arbitrary"), vmem),
        name="proj",
    )(h, g.reshape(1, d), w, cos, sin)


def _rope_tables(seq):
    inv_freq = ROPE_THETA ** (-jnp.arange(HALF_DIM, dtype=F32) / HALF_DIM)
    ang = jnp.arange(seq, dtype=F32)[:, None] * inv_freq[None, :]
    cos, sin = jnp.cos(ang), jnp.sin(ang)
    return jnp.concatenate([cos, cos], axis=-1), jnp.concatenate([-sin, sin], axis=-1)


OPROJ_ROWS = 1024
OPROJ_COLS = 1024


def _oproj_body(*refs):
    *aw, h_ref, o_ref = refs
    acc = h_ref[...]
    for a_ref, w_ref in zip(aw[0::2], aw[1::2]):
        acc = acc + jnp.dot(a_ref[...], w_ref[...], preferred_element_type=F32)
    o_ref[...] = acc


def _oproj(h, pairs):
    n, d = h.shape
    tm, tn = OPROJ_ROWS, OPROJ_COLS
    assert n % tm == 0 and d % tn == 0
    in_specs, args, vmem = [], [], 0
    for a, w in pairs:
        kdim = a.shape[1]
        in_specs += [pl.BlockSpec((tm, kdim), lambda i, j: (i, 0)),
                     pl.BlockSpec((kdim, tn), lambda i, j: (0, j))]
        args += [a, w]
        vmem += 2 * (tm * kdim + kdim * tn) * 2
    in_specs.append(pl.BlockSpec((tm, tn), lambda i, j: (i, j)))
    args.append(h)
    vmem += 6 * tm * tn * 4 + (8 << 20)
    return pl.pallas_call(
        _oproj_body,
        grid=(n // tm, d // tn),
        in_specs=in_specs,
        out_specs=pl.BlockSpec((tm, tn), lambda i, j: (i, j)),
        out_shape=jax.ShapeDtypeStruct((n, d), F32),
        compiler_params=_params(("parallel", "parallel"), vmem),
        name="oproj",
    )(*args)


def _dilated_body(q_ref, kc_ref, kp_ref, vc_ref, vp_ref, o_ref, kbuf, vbuf, *branch_bufs):
    obufs, lbufs = branch_bufs[:3], branch_bufs[3:]
    sb = pl.program_id(2)
    sup = DIL_SUPER
    kbuf[0:sup, :] = kp_ref[...]
    kbuf[sup:2 * sup, :] = kc_ref[...]
    vbuf[0:sup, :] = vp_ref[...]
    vbuf[sup:2 * sup, :] = vc_ref[...]

    row = lax.broadcasted_iota(jnp.int32, (DIL_BLOCK, 2 * DIL_BLOCK), 0)
    col = lax.broadcasted_iota(jnp.int32, (DIL_BLOCK, 2 * DIL_BLOCK), 1)
    band = jnp.logical_and(col >= row + (DIL_BLOCK - DIL_REACH), col <= row + DIL_BLOCK)

    for br, dil in enumerate(DILATIONS):
        nsub = sup // (DIL_BLOCK * dil)
        obuf, lbuf = obufs[br], lbufs[br]

        def block(t, carry, dil=dil, nsub=nsub, obuf=obuf, lbuf=lbuf):
            res = t // nsub
            sub = t % nsub
            q0 = res + dil * DIL_BLOCK * sub
            k0 = sup + q0 - dil * DIL_BLOCK
            if dil == 1:
                q0 = pl.multiple_of(q0, DIL_BLOCK)
                k0 = pl.multiple_of(k0, DIL_BLOCK)
                qs, ks = pl.ds(q0, DIL_BLOCK), pl.ds(k0, 2 * DIL_BLOCK)
            else:
                qs, ks = pl.ds(q0, DIL_BLOCK, stride=dil), pl.ds(k0, 2 * DIL_BLOCK, stride=dil)
            qb = q_ref[qs, :].astype(BF16)
            kb = kbuf[ks, :].astype(BF16)
            vb = vbuf[ks, :].astype(BF16)
            s = lax.dot_general(qb, kb, NT_DIMS, preferred_element_type=F32)
            has_prev = jnp.logical_or(sb > 0, sub > 0)
            first_col = jnp.where(has_prev, 0, DIL_BLOCK)
            s = jnp.where(jnp.logical_and(band, col >= first_col), s, NEG_INF)
            m = jnp.max(s, axis=-1, keepdims=True)
            p = jnp.exp(s - m)
            den = jnp.sum(p, axis=-1, keepdims=True)
            o = jnp.dot(p.astype(BF16), vb, preferred_element_type=F32) / den
            obuf[qs, :] = o
            lbuf[qs, :] = jnp.broadcast_to(m + jnp.log(den), (DIL_BLOCK, HEAD_DIM))
            return carry

        lax.fori_loop(0, nsub * dil, block, 0)

    lses = [lbuf[...] for lbuf in lbufs]
    top = jnp.maximum(jnp.maximum(lses[0], lses[1]), lses[2])
    wts = [jnp.exp(l - top) for l in lses]
    mix = wts[0] * obufs[0][...] + wts[1] * obufs[1][...] + wts[2] * obufs[2][...]
    o_ref[...] = (mix / (wts[0] + wts[1] + wts[2])).astype(o_ref.dtype)


def _dilated(qkv, batch, seq):
    sup = DIL_SUPER
    assert seq % sup == 0
    nsb = seq // sup
    heads = N_HEADS_A

    def cur(col0):
        return pl.BlockSpec((sup, HEAD_DIM), lambda b, h, s: (b * nsb + s, col0 + h))

    def prev(col0):
        return pl.BlockSpec((sup, HEAD_DIM), lambda b, h, s: (b * nsb + jnp.maximum(s - 1, 0), col0 + h))

    blk = sup * HEAD_DIM * 4
    vmem = 2 * 5 * blk + 2 * sup * HEAD_DIM * 2 + 4 * blk + 6 * blk + 8 * blk + (8 << 20)
    return pl.pallas_call(
        _dilated_body,
        grid=(batch, heads, nsb),
        in_specs=[cur(0), cur(heads), prev(heads), cur(2 * heads), prev(2 * heads)],
        out_specs=pl.BlockSpec((sup, HEAD_DIM), lambda b, h, s: (b * nsb + s, h)),
        out_shape=jax.ShapeDtypeStruct((batch * seq, WIDTH_A), BF16),
        scratch_shapes=[pltpu.VMEM((2 * sup, HEAD_DIM), F32)] * 2 + [pltpu.VMEM((sup, HEAD_DIM), F32)] * 6,
        compiler_params=_params(("parallel", "parallel", "arbitrary"), vmem),
        name="dilated",
    )(qkv, qkv, qkv, qkv, qkv)


def _scores_t(k_blk, q):
    return lax.dot_general(k_blk, q, NT_DIMS, preferred_element_type=F32)


def _softmax_init_t(s, vt_blk):
    m = jnp.max(s, axis=0, keepdims=True)
    p = jnp.exp2(s - m)
    l = jnp.sum(p, axis=0, keepdims=True)
    acc = jnp.dot(vt_blk, p.astype(BF16), preferred_element_type=F32)
    return m, l, acc


def _softmax_step_t(s, vt_blk, m, l, acc):
    m_new = jnp.maximum(m, jnp.max(s, axis=0, keepdims=True))
    alpha = jnp.exp2(m - m_new)
    p = jnp.exp2(s - m_new)
    l_new = alpha * l + jnp.sum(p, axis=0, keepdims=True)
    acc_new = alpha * acc + jnp.dot(vt_blk, p.astype(BF16), preferred_element_type=F32)
    return m_new, l_new, acc_new


def _store_transposed_blocks(v_ref, vt_ref):
    nblk, _, rows = vt_ref.shape
    for n in range(nblk):
        vt_ref[n] = v_ref[n * rows:(n + 1) * rows, :].astype(F32).T.astype(vt_ref.dtype)


def _moba_body(q_ref, k_ref, v_ref, o_ref, kmean_hi, kmean_lo, vt_ref, sel_ref):
    qi = pl.program_id(2)
    blk = MOBA_BLOCK
    nb = k_ref.shape[0] // blk

    @pl.when(qi == 0)
    def _():
        for n in range(nb):
            mean = jnp.mean(k_ref[n * blk:(n + 1) * blk, :].astype(F32), axis=0, keepdims=True)
            hi = mean.astype(BF16)
            kmean_hi[n:n + 1, :] = hi
            kmean_lo[n:n + 1, :] = (mean - hi.astype(F32)).astype(BF16)
        _store_transposed_blocks(v_ref, vt_ref)

    q = q_ref[...]
    gate = _scores_t(kmean_hi[...], q) + _scores_t(kmean_lo[...], q)
    blk_id = lax.broadcasted_iota(jnp.int32, gate.shape, 0)
    past = blk_id < qi
    work = jnp.where(past, gate, NEG_INF)
    chosen = jnp.zeros(gate.shape, F32)
    for _ in range(MOBA_TOPK):
        best = jnp.max(work, axis=0, keepdims=True)
        first = jnp.min(jnp.where(work == best, blk_id, nb), axis=0, keepdims=True)
        pick = blk_id == first
        chosen = jnp.where(pick, 1.0, chosen)
        work = jnp.where(pick, -jnp.inf, work)
    sel_ref[...] = jnp.where(past, chosen, 0.0)

    key = lax.broadcasted_iota(jnp.int32, (blk, blk), 0)
    qry = lax.broadcasted_iota(jnp.int32, (blk, blk), 1)
    own = pl.multiple_of(qi * blk, blk)
    s = jnp.where(key <= qry, _scores_t(k_ref[pl.ds(own, blk), :], q), NEG_INF)
    carry = _softmax_init_t(s, vt_ref[qi])

    def past_block(j, carry):
        start = pl.multiple_of(j * blk, blk)
        s = _scores_t(k_ref[pl.ds(start, blk), :], q)
        s = jnp.where(sel_ref[pl.ds(j, 1), :] > 0.5, s, NEG_INF)
        return _softmax_step_t(s, vt_ref[j], *carry)

    _, l, acc = lax.fori_loop(0, qi, past_block, carry)
    o_ref[...] = (acc * (1.0 / l)).T.astype(o_ref.dtype)


def _moba(qkv, batch, seq):
    blk = MOBA_BLOCK
    assert seq % blk == 0
    nq = seq // blk
    heads = N_HEADS_B
    vmem = 2 * 2 * seq * HEAD_DIM * 2 + seq * HEAD_DIM * 2 + 4 * blk * HEAD_DIM * 2 + 16 * blk * blk * 4 + (8 << 20)
    return pl.pallas_call(
        _moba_body,
        grid=(batch, heads, nq),
        in_specs=[
            pl.BlockSpec((blk, HEAD_DIM), lambda b, h, i: (b * nq + i, h)),
            pl.BlockSpec((seq, HEAD_DIM), lambda b, h, i: (b, heads + h)),
            pl.BlockSpec((seq, HEAD_DIM), lambda b, h, i: (b, 2 * heads + h)),
        ],
        out_specs=pl.BlockSpec((blk, HEAD_DIM), lambda b, h, i: (b * nq + i, h)),
        out_shape=jax.ShapeDtypeStruct((batch * seq, WIDTH_B), BF16),
        scratch_shapes=[pltpu.VMEM((nq, HEAD_DIM), BF16)] * 2
        + [pltpu.VMEM((nq, HEAD_DIM, blk), BF16), pltpu.VMEM((nq, blk), F32)],
        compiler_params=_params(("parallel", "parallel", "arbitrary"), vmem),
        name="moba",
    )(qkv, qkv, qkv)


def _diff_body(lq1_ref, lk1_ref, lq2_ref, lk2_ref, g_ref, q1_ref, q2_ref, k1_ref, k2_ref, v_ref, o_ref,
               vt_ref, m1, l1, acc1, m2, l2, acc2, *, lambda_init):
    qi = pl.program_id(2)
    t = DIFF_BLOCK
    streams = ((q1_ref, k1_ref, m1, l1, acc1), (q2_ref, k2_ref, m2, l2, acc2))

    @pl.when(qi == 0)
    def _():
        _store_transposed_blocks(v_ref, vt_ref)

    key = lax.broadcasted_iota(jnp.int32, (t, t), 0)
    qry = lax.broadcasted_iota(jnp.int32, (t, t), 1)
    own = pl.multiple_of(qi * t, t)
    for q_ref, k_ref, m, l, acc in streams:
        s = jnp.where(key <= qry, _scores_t(k_ref[pl.ds(own, t), :], q_ref[...]), NEG_INF)
        m[...], l[...], acc[...] = _softmax_init_t(s, vt_ref[qi])

    def past_block(j, carry):
        start = pl.multiple_of(j * t, t)
        for q_ref, k_ref, m, l, acc in streams:
            s = _scores_t(k_ref[pl.ds(start, t), :], q_ref[...])
            m[...], l[...], acc[...] = _softmax_step_t(s, vt_ref[j], m[...], l[...], acc[...])
        return carry

    lax.fori_loop(0, qi, past_block, 0)

    lam = (jnp.exp(jnp.sum(lq1_ref[...] * lk1_ref[...], axis=-1, keepdims=True))
           - jnp.exp(jnp.sum(lq2_ref[...] * lk2_ref[...], axis=-1, keepdims=True)) + lambda_init)
    o = acc1[...] * (1.0 / l1[...]) - lam * (acc2[...] * (1.0 / l2[...]))
    o = o * lax.rsqrt(jnp.mean(o * o, axis=0, keepdims=True) + SUBLN_EPS)
    o_ref[...] = (o.T * g_ref[...] * (1.0 - lambda_init)).astype(o_ref.dtype)


def _diff(qkv, lq1, lk1, lq2, lk2, subln_g, lambda_init, batch, seq):
    t = DIFF_BLOCK
    assert seq % t == 0
    nq = seq // t
    heads = N_HEADS_C
    dv = 2 * HEAD_DIM
    kcol = D_MODEL // HEAD_DIM
    vcol = 2 * D_MODEL // dv

    def vec(width):
        return pl.BlockSpec((1, width), lambda b, h, i: (0, 0))

    def q_spec(part):
        return pl.BlockSpec((t, HEAD_DIM), lambda b, h, i: (b * nq + i, 2 * h + part))

    def k_spec(part):
        return pl.BlockSpec((seq, HEAD_DIM), lambda b, h, i: (b, kcol + 2 * h + part))

    vmem = (2 * (2 * seq * HEAD_DIM + seq * dv) * 2 + seq * dv * 2 + 8 * t * HEAD_DIM * 2 + 2 * t * dv * 2
            + 2 * t * dv * 4 + 10 * t * t * 4 + (8 << 20))
    return pl.pallas_call(
        functools.partial(_diff_body, lambda_init=lambda_init),
        grid=(batch, heads, nq),
        in_specs=[vec(HEAD_DIM)] * 4 + [vec(dv), q_spec(0), q_spec(1), k_spec(0), k_spec(1),
                                        pl.BlockSpec((seq, dv), lambda b, h, i: (b, vcol + h))],
        out_specs=pl.BlockSpec((t, dv), lambda b, h, i: (b * nq + i, h)),
        out_shape=jax.ShapeDtypeStruct((batch * seq, heads * dv), BF16),
        scratch_shapes=[pltpu.VMEM((nq, dv, t), BF16)]
        + [pltpu.VMEM((1, t), F32), pltpu.VMEM((1, t), F32), pltpu.VMEM((dv, t), F32)] * 2,
        compiler_params=_params(("parallel", "parallel", "arbitrary"), vmem),
        name="diffattn",
    )(lq1.reshape(1, -1), lk1.reshape(1, -1), lq2.reshape(1, -1), lk2.reshape(1, -1),
      subln_g.reshape(1, -1), qkv, qkv, qkv, qkv, qkv)


def _lambda_init(layer):
    return 0.8 - 0.6 * math.exp(-0.3 * layer)


def kernel(x, ffa_norm, ffa_w_in, ffa_w_out, mix_norm, even_w_in, even_w_out, odd_w_in, odd_w_out,
           lambda_q1, lambda_k1, lambda_q2, lambda_k2, subln_norm, ffb_norm, ffb_w_in, ffb_w_out, final_norm):
    batch, seq, d = x.shape
    depth = ffa_norm.shape[0]
    cos, sin = _rope_tables(seq)
    h = x.reshape(batch * seq, d)
    for layer in range(depth):
        i = layer // 2
        h = _ffn(h, ffa_norm[layer], ffa_w_in[layer].astype(BF16), ffa_w_out[layer].astype(BF16))
        if layer % 2 == 0:
            w_in = even_w_in[i].astype(BF16)
            w_out = even_w_out[i].astype(BF16)
            wa = 3 * WIDTH_A
            qkv_a = _proj(h, mix_norm[layer], w_in[:, :wa], cos, sin,
                          (0, WIDTH_A), (WIDTH_A, 2 * WIDTH_A), SCALE, F32)
            qkv_b = _proj(h, mix_norm[layer], w_in[:, wa:], cos, sin,
                          (0, WIDTH_B), (WIDTH_B, 2 * WIDTH_B), SCALE * LOG2_E, BF16)
            o_a = _dilated(qkv_a, batch, seq)
            o_b = _moba(qkv_b, batch, seq)
            h = _oproj(h, [(o_a, w_out[:WIDTH_A]), (o_b, w_out[WIDTH_A:])])
        else:
            qkv = _proj(h, mix_norm[layer], odd_w_in[i].astype(BF16), cos, sin,
                        (0, D_MODEL), (D_MODEL, 2 * D_MODEL), SCALE * LOG2_E, BF16)
            o = _diff(qkv, lambda_q1[i], lambda_k1[i], lambda_q2[i], lambda_k2[i], subln_norm[i],
                      _lambda_init(layer), batch, seq)
            h = _oproj(h, [(o, odd_w_out[i].astype(BF16))])
        last = layer == depth - 1
        h = _ffn(h, ffb_norm[layer], ffb_w_in[layer].astype(BF16), ffb_w_out[layer].astype(BF16),
                 final_norm if last else None)
    return h.reshape(batch, seq, d)
```

```python
import functools
import math

import jax
import jax.numpy as jnp
from jax import lax
from jax.experimental import pallas as pl
from jax.experimental.pallas import tpu as pltpu

F32 = jnp.float32
BF16 = jnp.bfloat16

D_MODEL = 2048
HEAD_DIM = 128
HALF_DIM = HEAD_DIM // 2
N_HEADS_A = 8
N_HEADS_B = 8
N_HEADS_C = 8
WIDTH_A = N_HEADS_A * HEAD_DIM
WIDTH_B = N_HEADS_B * HEAD_DIM
DILATIONS = (1, 4, 16)
DIL_BLOCK = 128
DIL_REACH = 128
DIL_SUPER = DIL_BLOCK * DILATIONS[-1]
DIL_UNROLL = 16
MOBA_BLOCK = 256
MOBA_TOPK = 3
MOBA_TILE = 2 * MOBA_BLOCK
DIFF_BLOCK = 512
ROPE_THETA = 10000.0
NORM_EPS = 1e-6
SUBLN_EPS = 1e-5
NEG_INF = -1e30
SCALE = HEAD_DIM ** -0.5
LOG2_E = math.log2(math.e)

VMEM_V7X_BYTES = 64 * 1024 * 1024
NT_DIMS = (((1,), (1,)), ((), ()))


def _params(semantics, vmem_bytes):
    assert vmem_bytes < VMEM_V7X_BYTES
    return pltpu.CompilerParams(dimension_semantics=semantics, vmem_limit_bytes=int(vmem_bytes))


def _rms(x, g, eps):
    return x * lax.rsqrt(jnp.mean(x * x, axis=-1, keepdims=True) + eps) * g


FFN_ROWS = 512
FFN_COLS = 512


def _ffn_body(x_ref, g_ref, wg_ref, wu_ref, wo_ref, *rest, final):
    if final:
        fg_ref, o_ref, xn_ref = rest
    else:
        o_ref, xn_ref = rest
    j = pl.program_id(1)

    @pl.when(j == 0)
    def _():
        x = x_ref[...]
        xn_ref[...] = _rms(x, g_ref[...], NORM_EPS).astype(BF16)
        o_ref[...] = x

    xn = xn_ref[...]
    gate = jnp.dot(xn, wg_ref[...], preferred_element_type=F32)
    up = jnp.dot(xn, wu_ref[...], preferred_element_type=F32)
    act = (0.5 * gate / (1.0 + jnp.exp(-gate))) * up
    o_ref[...] += jnp.dot(act.astype(BF16), wo_ref[...], preferred_element_type=F32)

    if final:
        @pl.when(j == pl.num_programs(1) - 1)
        def _():
            o_ref[...] = _rms(o_ref[...], fg_ref[...], NORM_EPS)


def _ffn(h, g, w_in, w_out, final_g=None):
    n, d = h.shape
    d_ff = w_out.shape[0]
    tm, tf = FFN_ROWS, FFN_COLS
    nff = d_ff // tf
    assert n % tm == 0 and d_ff % tf == 0
    final = final_g is not None
    in_specs = [
        pl.BlockSpec((tm, d), lambda i, j: (i, 0)),
        pl.BlockSpec((1, d), lambda i, j: (0, 0)),
        pl.BlockSpec((d, tf), lambda i, j: (0, j)),
        pl.BlockSpec((d, tf), lambda i, j: (0, nff + j)),
        pl.BlockSpec((tf, d), lambda i, j: (j, 0)),
    ]
    args = [h, g.reshape(1, d), w_in, w_in, w_out]
    if final:
        in_specs.append(pl.BlockSpec((1, d), lambda i, j: (0, 0)))
        args.append(final_g.reshape(1, d))
    vmem = 2 * tm * d * 4 * 2 + tm * d * 2 + 2 * (2 * d * tf + tf * d) * 2 + 4 * tm * tf * 4 + (8 << 20)
    return pl.pallas_call(
        functools.partial(_ffn_body, final=final),
        grid=(n // tm, nff),
        in_specs=in_specs,
        out_specs=pl.BlockSpec((tm, d), lambda i, j: (i, 0)),
        out_shape=jax.ShapeDtypeStruct((n, d), F32),
        scratch_shapes=[pltpu.VMEM((tm, d), BF16)],
        compiler_params=_params(("parallel", "arbitrary"), vmem),
        name="ffn",
    )(*args)


PROJ_ROWS = 1024
PROJ_COLS = 1024


def _in_range(j, lo, hi):
    return jnp.logical_and(j >= lo, j < hi)


def _proj_body(x_ref, g_ref, w_ref, cos_ref, sin_ref, o_ref, xn_ref, *, q_tiles, k_tiles, q_scale):
    j = pl.program_id(1)

    @pl.when(j == 0)
    def _():
        xn_ref[...] = _rms(x_ref[...], g_ref[...], NORM_EPS).astype(BF16)

    y = jnp.dot(xn_ref[...], w_ref[...], preferred_element_type=F32)
    is_q = _in_range(j, *q_tiles)
    is_rope = jnp.logical_or(is_q, _in_range(j, *k_tiles))

    @pl.when(is_rope)
    def _():
        cos = cos_ref[...]
        sin = sin_ref[...]
        scale = jnp.where(is_q, q_scale, 1.0).astype(F32)
        for c in range(y.shape[1] // HEAD_DIM):
            yh = y[:, c * HEAD_DIM:(c + 1) * HEAD_DIM]
            rot = yh * cos + pltpu.roll(yh, HALF_DIM, 1) * sin
            o_ref[:, c * HEAD_DIM:(c + 1) * HEAD_DIM] = (rot * scale).astype(o_ref.dtype)

    @pl.when(jnp.logical_not(is_rope))
    def _():
        o_ref[...] = y.astype(o_ref.dtype)


def _proj(h, g, w, cos, sin, q_cols, k_cols, q_scale, out_dtype):
    n, d = h.shape
    width = w.shape[1]
    seq = cos.shape[0]
    tm, tn = PROJ_ROWS, PROJ_COLS
    assert n % tm == 0 and width % tn == 0 and seq % tm == 0
    assert all(c % tn == 0 for c in (*q_cols, *k_cols))
    pos_blocks = seq // tm
    body = functools.partial(
        _proj_body,
        q_tiles=(q_cols[0] // tn, q_cols[1] // tn),
        k_tiles=(k_cols[0] // tn, k_cols[1] // tn),
        q_scale=q_scale,
    )
    vmem = 2 * tm * d * 4 + tm * d * 2 + 2 * d * tn * 2 + 4 * tm * tn * 4 + 4 * tm * HEAD_DIM * 4 + (8 << 20)
    return pl.pallas_call(
        body,
        grid=(n // tm, width // tn),
        in_specs=[
            pl.BlockSpec((tm, d), lambda i, j: (i, 0)),
            pl.BlockSpec((1, d), lambda i, j: (0, 0)),
            pl.BlockSpec((d, tn), lambda i, j: (0, j)),
            pl.BlockSpec((tm, HEAD_DIM), lambda i, j: (i % pos_blocks, 0)),
            pl.BlockSpec((tm, HEAD_DIM), lambda i, j: (i % pos_blocks, 0)),
        ],
        out_specs=pl.BlockSpec((tm, tn), lambda i, j: (i, j)),
        out_shape=jax.ShapeDtypeStruct((n, width), out_dtype),
        scratch_shapes=[pltpu.VMEM((tm, d), BF16)],
        compiler_params=_params(("parallel", "arbitrary"), vmem),
        name="proj",
    )(h, g.reshape(1, d), w, cos, sin)


def _rope_tables(seq):
    inv_freq = ROPE_THETA ** (-jnp.arange(HALF_DIM, dtype=F32) / HALF_DIM)
    ang = jnp.arange(seq, dtype=F32)[:, None] * inv_freq[None, :]
    cos, sin = jnp.cos(ang), jnp.sin(ang)
    return jnp.concatenate([cos, cos], axis=-1), jnp.concatenate([-sin, sin], axis=-1)


OPROJ_ROWS = 1024
OPROJ_COLS = 1024


def _oproj_body(*refs):
    *aw, h_ref, o_ref = refs
    acc = h_ref[...]
    for a_ref, w_ref in zip(aw[0::2], aw[1::2]):
        acc = acc + jnp.dot(a_ref[...], w_ref[...], preferred_element_type=F32)
    o_ref[...] = acc


def _oproj(h, pairs):
    n, d = h.shape
    tm, tn = OPROJ_ROWS, OPROJ_COLS
    assert n % tm == 0 and d % tn == 0
    in_specs, args, vmem = [], [], 0
    for a, w in pairs:
        kdim = a.shape[1]
        in_specs += [pl.BlockSpec((tm, kdim), lambda i, j: (i, 0)),
                     pl.BlockSpec((kdim, tn), lambda i, j: (0, j))]
        args += [a, w]
        vmem += 2 * (tm * kdim + kdim * tn) * 2
    in_specs.append(pl.BlockSpec((tm, tn), lambda i, j: (i, j)))
    args.append(h)
    vmem += 6 * tm * tn * 4 + (8 << 20)
    return pl.pallas_call(
        _oproj_body,
        grid=(n // tm, d // tn),
        in_specs=in_specs,
        out_specs=pl.BlockSpec((tm, tn), lambda i, j: (i, j)),
        out_shape=jax.ShapeDtypeStruct((n, d), F32),
        compiler_params=_params(("parallel", "parallel"), vmem),
        name="oproj",
    )(*args)


def _dilated_body(q_ref, kc_ref, kp_ref, vc_ref, vp_ref, o_ref, kbuf, vbuf, *branch_bufs):
    obufs, lbufs = branch_bufs[:3], branch_bufs[3:]
    sb = pl.program_id(2)
    sup = DIL_SUPER
    kbuf[0:sup, :] = kp_ref[...]
    kbuf[sup:2 * sup, :] = kc_ref[...]
    vbuf[0:sup, :] = vp_ref[...]
    vbuf[sup:2 * sup, :] = vc_ref[...]

    row = lax.broadcasted_iota(jnp.int32, (DIL_BLOCK, 2 * DIL_BLOCK), 0)
    col = lax.broadcasted_iota(jnp.int32, (DIL_BLOCK, 2 * DIL_BLOCK), 1)
    band = jnp.logical_and(col >= row + (DIL_BLOCK - DIL_REACH), col <= row + DIL_BLOCK)

    for br, dil in enumerate(DILATIONS):
        nsub = sup // (DIL_BLOCK * dil)
        obuf, lbuf = obufs[br], lbufs[br]

        def block(t, carry, dil=dil, nsub=nsub, obuf=obuf, lbuf=lbuf):
            res = t // nsub
            sub = t % nsub
            q0 = res + dil * DIL_BLOCK * sub
            k0 = sup + q0 - dil * DIL_BLOCK
            if dil == 1:
                q0 = pl.multiple_of(q0, DIL_BLOCK)
                k0 = pl.multiple_of(k0, DIL_BLOCK)
                qs, ks = pl.ds(q0, DIL_BLOCK), pl.ds(k0, 2 * DIL_BLOCK)
            else:
                qs, ks = pl.ds(q0, DIL_BLOCK, stride=dil), pl.ds(k0, 2 * DIL_BLOCK, stride=dil)
            qb = q_ref[qs, :].astype(BF16)
            kb = kbuf[ks, :].astype(BF16)
            vb = vbuf[ks, :].astype(BF16)
            s = lax.dot_general(qb, kb, NT_DIMS, preferred_element_type=F32)
            has_prev = jnp.logical_or(sb > 0, sub > 0)
            first_col = jnp.where(has_prev, 0, DIL_BLOCK)
            s = jnp.where(jnp.logical_and(band, col >= first_col), s, NEG_INF)
            m = jnp.max(s, axis=-1, keepdims=True)
            p = jnp.exp(s - m)
            den = jnp.sum(p, axis=-1, keepdims=True)
            o = jnp.dot(p.astype(BF16), vb, preferred_element_type=F32) / den
            obuf[qs, :] = o
            lbuf[qs, :] = jnp.broadcast_to(m + jnp.log(den), (DIL_BLOCK, HEAD_DIM))
            return carry

        lax.fori_loop(0, nsub * dil, block, 0, unroll=DIL_UNROLL)

    lses = [lbuf[...] for lbuf in lbufs]
    top = jnp.maximum(jnp.maximum(lses[0], lses[1]), lses[2])
    wts = [jnp.exp(l - top) for l in lses]
    mix = wts[0] * obufs[0][...] + wts[1] * obufs[1][...] + wts[2] * obufs[2][...]
    o_ref[...] = (mix / (wts[0] + wts[1] + wts[2])).astype(o_ref.dtype)


def _dilated(qkv, batch, seq):
    sup = DIL_SUPER
    assert seq % sup == 0
    nsb = seq // sup
    heads = N_HEADS_A

    def cur(col0):
        return pl.BlockSpec((sup, HEAD_DIM), lambda b, h, s: (b * nsb + s, col0 + h))

    def prev(col0):
        return pl.BlockSpec((sup, HEAD_DIM), lambda b, h, s: (b * nsb + jnp.maximum(s - 1, 0), col0 + h))

    blk = sup * HEAD_DIM * 4
    vmem = 2 * 5 * blk + 2 * sup * HEAD_DIM * 2 + 4 * blk + 6 * blk + 8 * blk + (8 << 20)
    return pl.pallas_call(
        _dilated_body,
        grid=(batch, heads, nsb),
        in_specs=[cur(0), cur(heads), prev(heads), cur(2 * heads), prev(2 * heads)],
        out_specs=pl.BlockSpec((sup, HEAD_DIM), lambda b, h, s: (b * nsb + s, h)),
        out_shape=jax.ShapeDtypeStruct((batch * seq, WIDTH_A), BF16),
        scratch_shapes=[pltpu.VMEM((2 * sup, HEAD_DIM), F32)] * 2 + [pltpu.VMEM((sup, HEAD_DIM), F32)] * 6,
        compiler_params=_params(("parallel", "parallel", "arbitrary"), vmem),
        name="dilated",
    )(qkv, qkv, qkv, qkv, qkv)


def _scores_t(k_blk, q):
    return lax.dot_general(k_blk, q, NT_DIMS, preferred_element_type=F32)


def _flash_t(streams, qi, tile, diag_mask, past_mask):
    own = pl.multiple_of(qi * tile, tile)
    for q, k_ref, _, p_ref, acc_ref, st_ref in streams:
        s = diag_mask(_scores_t(k_ref[pl.ds(own, tile), :], q))
        m = jnp.max(s, axis=0, keepdims=True)
        p = jnp.exp2(s - m)
        p_ref[0] = p.astype(BF16)
        acc_ref[...] = jnp.zeros(acc_ref.shape, F32)
        st_ref[0] = m
        st_ref[1] = jnp.sum(p, axis=0, keepdims=True)
        st_ref[2] = jnp.ones_like(m)

    def flush(pending, slot):
        for _, _, vt_ref, p_ref, acc_ref, st_ref in streams:
            acc_ref[...] = st_ref[2] * acc_ref[...] + jnp.dot(vt_ref[pending], p_ref[slot],
                                                              preferred_element_type=F32)

    def step(j, pending, rd, wr):
        start = pl.multiple_of(j * tile, tile)
        scores = [_scores_t(st[1][pl.ds(start, tile), :], st[0]) for st in streams]
        flush(pending, rd)
        for s, (_, _, _, p_ref, _, st_ref) in zip(scores, streams):
            s = past_mask(s, j)
            m = st_ref[0]
            m_new = jnp.maximum(m, jnp.max(s, axis=0, keepdims=True))
            alpha = jnp.exp2(m - m_new)
            p = jnp.exp2(s - m_new)
            p_ref[wr] = p.astype(BF16)
            st_ref[0] = m_new
            st_ref[1] = alpha * st_ref[1] + jnp.sum(p, axis=0, keepdims=True)
            st_ref[2] = alpha

    def pair(i, carry):
        step(2 * i, jnp.where(i == 0, qi, 2 * i - 1), 0, 1)
        step(2 * i + 1, 2 * i, 1, 0)
        return carry

    npairs = qi // 2
    lax.fori_loop(0, npairs, pair, 0)
    odd = qi % 2 == 1
    last = jnp.maximum(qi - 1, 0)

    @pl.when(odd)
    def _():
        step(last, jnp.where(qi == 1, qi, qi - 2), 0, 1)
        flush(last, 1)

    @pl.when(jnp.logical_not(odd))
    def _():
        flush(last, 0)

    return [(st[4][...], st[5][1]) for st in streams]


def _store_transposed_blocks(v_ref, vt_ref):
    nblk, _, rows = vt_ref.shape
    for n in range(nblk):
        vt_ref[n] = v_ref[n * rows:(n + 1) * rows, :].astype(F32).T.astype(vt_ref.dtype)


def _moba_body(q_ref, k_ref, v_ref, o_ref, kmean_hi, kmean_lo, vt_ref, sel_ref, p_ref, acc_ref, st_ref):
    qi = pl.program_id(2)
    blk, tile = MOBA_BLOCK, MOBA_TILE
    nb = k_ref.shape[0] // blk

    @pl.when(qi == 0)
    def _():
        for n in range(nb):
            mean = jnp.mean(k_ref[n * blk:(n + 1) * blk, :].astype(F32), axis=0, keepdims=True)
            hi = mean.astype(BF16)
            kmean_hi[n:n + 1, :] = hi
            kmean_lo[n:n + 1, :] = (mean - hi.astype(F32)).astype(BF16)
        _store_transposed_blocks(v_ref, vt_ref)

    q = q_ref[...]
    gate = _scores_t(kmean_hi[...], q) + _scores_t(kmean_lo[...], q)
    blk_id = lax.broadcasted_iota(jnp.int32, gate.shape, 0)
    q_blk = 2 * qi + lax.broadcasted_iota(jnp.int32, gate.shape, 1) // blk
    past = blk_id < q_blk
    work = jnp.where(past, gate, NEG_INF)
    chosen = jnp.zeros(gate.shape, F32)
    for _ in range(MOBA_TOPK):
        best = jnp.max(work, axis=0, keepdims=True)
        first = jnp.min(jnp.where(work == best, blk_id, nb), axis=0, keepdims=True)
        pick = blk_id == first
        chosen = jnp.where(pick, 1.0, chosen)
        work = jnp.where(pick, -jnp.inf, work)
    sel_ref[...] = jnp.where(past, chosen, 0.0)

    def picked(block):
        return sel_ref[pl.ds(block, 1), :] > 0.5

    def past_mask(s, j):
        return jnp.concatenate([jnp.where(picked(2 * j), s[:blk], NEG_INF),
                                jnp.where(picked(2 * j + 1), s[blk:], NEG_INF)], axis=0)

    def diag_mask(s):
        key = lax.broadcasted_iota(jnp.int32, (tile, tile), 0)
        qry = lax.broadcasted_iota(jnp.int32, (tile, tile), 1)
        other = jnp.where(key < blk, jnp.where(picked(2 * qi), s, NEG_INF), NEG_INF)
        own_start = (qry // blk) * blk
        return jnp.where(key <= qry, jnp.where(key >= own_start, s, other), other)

    [(acc, l)] = _flash_t([(q, k_ref, vt_ref, p_ref, acc_ref, st_ref)], qi, tile, diag_mask, past_mask)
    o_ref[...] = (acc * (1.0 / l)).T.astype(o_ref.dtype)


def _moba(qkv, batch, seq):
    blk, tile = MOBA_BLOCK, MOBA_TILE
    assert seq % tile == 0
    nq = seq // tile
    nb = seq // blk
    heads = N_HEADS_B
    vmem = (2 * 2 * seq * HEAD_DIM * 2 + seq * HEAD_DIM * 2 + 4 * tile * HEAD_DIM * 2 + 2 * tile * tile * 2
            + HEAD_DIM * tile * 4 + 10 * tile * tile * 4 + (8 << 20))
    return pl.pallas_call(
        _moba_body,
        grid=(batch, heads, nq),
        in_specs=[
            pl.BlockSpec((tile, HEAD_DIM), lambda b, h, i: (b * nq + i, h)),
            pl.BlockSpec((seq, HEAD_DIM), lambda b, h, i: (b, heads + h)),
            pl.BlockSpec((seq, HEAD_DIM), lambda b, h, i: (b, 2 * heads + h)),
        ],
        out_specs=pl.BlockSpec((tile, HEAD_DIM), lambda b, h, i: (b * nq + i, h)),
        out_shape=jax.ShapeDtypeStruct((batch * seq, WIDTH_B), BF16),
        scratch_shapes=[pltpu.VMEM((nb, HEAD_DIM), BF16)] * 2
        + [pltpu.VMEM((nq, HEAD_DIM, tile), BF16), pltpu.VMEM((nb, tile), F32),
           pltpu.VMEM((2, tile, tile), BF16), pltpu.VMEM((HEAD_DIM, tile), F32),
           pltpu.VMEM((3, 1, tile), F32)],
        compiler_params=_params(("parallel", "parallel", "arbitrary"), vmem),
        name="moba",
    )(qkv, qkv, qkv)


def _diff_body(lq1_ref, lk1_ref, lq2_ref, lk2_ref, g_ref, q1_ref, q2_ref, k1_ref, k2_ref, v_ref, o_ref,
               vt_ref, p1_ref, p2_ref, acc1_ref, acc2_ref, st1_ref, st2_ref, *, lambda_init):
    qi = pl.program_id(2)
    t = DIFF_BLOCK

    @pl.when(qi == 0)
    def _():
        _store_transposed_blocks(v_ref, vt_ref)

    def diag_mask(s):
        key = lax.broadcasted_iota(jnp.int32, (t, t), 0)
        qry = lax.broadcasted_iota(jnp.int32, (t, t), 1)
        return jnp.where(key <= qry, s, NEG_INF)

    streams = [(q1_ref[...], k1_ref, vt_ref, p1_ref, acc1_ref, st1_ref),
               (q2_ref[...], k2_ref, vt_ref, p2_ref, acc2_ref, st2_ref)]
    (a1, l1), (a2, l2) = _flash_t(streams, qi, t, diag_mask, lambda s, j: s)

    lam = (jnp.exp(jnp.sum(lq1_ref[...] * lk1_ref[...], axis=-1, keepdims=True))
           - jnp.exp(jnp.sum(lq2_ref[...] * lk2_ref[...], axis=-1, keepdims=True)) + lambda_init)
    o = a1 * (1.0 / l1) - lam * (a2 * (1.0 / l2))
    o = o * lax.rsqrt(jnp.mean(o * o, axis=0, keepdims=True) + SUBLN_EPS)
    o_ref[...] = (o.T * g_ref[...] * (1.0 - lambda_init)).astype(o_ref.dtype)


def _diff(qkv, lq1, lk1, lq2, lk2, subln_g, lambda_init, batch, seq):
    t = DIFF_BLOCK
    assert seq % t == 0
    nq = seq // t
    heads = N_HEADS_C
    dv = 2 * HEAD_DIM
    kcol = D_MODEL // HEAD_DIM
    vcol = 2 * D_MODEL // dv

    def vec(width):
        return pl.BlockSpec((1, width), lambda b, h, i: (0, 0))

    def q_spec(part):
        return pl.BlockSpec((t, HEAD_DIM), lambda b, h, i: (b * nq + i, 2 * h + part))

    def k_spec(part):
        return pl.BlockSpec((seq, HEAD_DIM), lambda b, h, i: (b, kcol + 2 * h + part))

    vmem = (2 * (2 * seq * HEAD_DIM + seq * dv) * 2 + seq * dv * 2 + 8 * t * HEAD_DIM * 2 + 2 * t * dv * 2
            + 2 * t * dv * 4 + 4 * t * t * 2 + 10 * t * t * 4 + (8 << 20))
    return pl.pallas_call(
        functools.partial(_diff_body, lambda_init=lambda_init),
        grid=(batch, heads, nq),
        in_specs=[vec(HEAD_DIM)] * 4 + [vec(dv), q_spec(0), q_spec(1), k_spec(0), k_spec(1),
                                        pl.BlockSpec((seq, dv), lambda b, h, i: (b, vcol + h))],
        out_specs=pl.BlockSpec((t, dv), lambda b, h, i: (b * nq + i, h)),
        out_shape=jax.ShapeDtypeStruct((batch * seq, heads * dv), BF16),
        scratch_shapes=[pltpu.VMEM((nq, dv, t), BF16)] + [pltpu.VMEM((2, t, t), BF16)] * 2
        + [pltpu.VMEM((dv, t), F32)] * 2 + [pltpu.VMEM((3, 1, t), F32)] * 2,
        compiler_params=_params(("parallel", "parallel", "arbitrary"), vmem),
        name="diffattn",
    )(lq1.reshape(1, -1), lk1.reshape(1, -1), lq2.reshape(1, -1), lk2.reshape(1, -1),
      subln_g.reshape(1, -1), qkv, qkv, qkv, qkv, qkv)


def _lambda_init(layer):
    return 0.8 - 0.6 * math.exp(-0.3 * layer)


def kernel(x, ffa_norm, ffa_w_in, ffa_w_out, mix_norm, even_w_in, even_w_out, odd_w_in, odd_w_out,
           lambda_q1, lambda_k1, lambda_q2, lambda_k2, subln_norm, ffb_norm, ffb_w_in, ffb_w_out, final_norm):
    batch, seq, d = x.shape
    depth = ffa_norm.shape[0]
    cos, sin = _rope_tables(seq)
    h = x.reshape(batch * seq, d)
    for layer in range(depth):
        i = layer // 2
        h = _ffn(h, ffa_norm[layer], ffa_w_in[layer].astype(BF16), ffa_w_out[layer].astype(BF16))
        if layer % 2 == 0:
            w_in = even_w_in[i].astype(BF16)
            w_out = even_w_out[i].astype(BF16)
            wa = 3 * WIDTH_A
            qkv_a = _proj(h, mix_norm[layer], w_in[:, :wa], cos, sin,
                          (0, WIDTH_A), (WIDTH_A, 2 * WIDTH_A), SCALE, F32)
            qkv_b = _proj(h, mix_norm[layer], w_in[:, wa:], cos, sin,
                          (0, WIDTH_B), (WIDTH_B, 2 * WIDTH_B), SCALE * LOG2_E, BF16)
            o_a = _dilated(qkv_a, batch, seq)
            o_b = _moba(qkv_b, batch, seq)
            h = _oproj(h, [(o_a, w_out[:WIDTH_A]), (o_b, w_out[WIDTH_A:])])
        else:
            qkv = _proj(h, mix_norm[layer], odd_w_in[i].astype(BF16), cos, sin,
                        (0, D_MODEL), (D_MODEL, 2 * D_MODEL), SCALE * LOG2_E, BF16)
            o = _diff(qkv, lambda_q1[i], lambda_k1[i], lambda_q2[i], lambda_k2[i], subln_norm[i],
                      _lambda_init(layer), batch, seq)
            h = _oproj(h, [(o, odd_w_out[i].astype(BF16))])
        last = layer == depth - 1
        h = _ffn(h, ffb_norm[layer], ffb_w_in[layer].astype(BF16), ffb_w_out[layer].astype(BF16),
                 final_norm if last else None)
    return h.reshape(batch, seq, d)
```

```python
import functools
import math

import jax
import jax.numpy as jnp
from jax import lax
from jax.experimental import pallas as pl
from jax.experimental.pallas import tpu as pltpu

F32 = jnp.float32
BF16 = jnp.bfloat16

D_MODEL = 2048
HEAD_DIM = 128
HALF_DIM = HEAD_DIM // 2
N_HEADS_A = 8
N_HEADS_B = 8
N_HEADS_C = 8
WIDTH_A = N_HEADS_A * HEAD_DIM
WIDTH_B = N_HEADS_B * HEAD_DIM
DILATIONS = (1, 4, 16)
DIL_BLOCK = 128
DIL_REACH = 128
DIL_SUPER = DIL_BLOCK * DILATIONS[-1]
DIL_UNROLL = 16
MOBA_BLOCK = 256
MOBA_TOPK = 3
MOBA_TILE = 2 * MOBA_BLOCK
DIFF_BLOCK = 512
ROPE_THETA = 10000.0
NORM_EPS = 1e-6
SUBLN_EPS = 1e-5
NEG_INF = -1e30
SCALE = HEAD_DIM ** -0.5
LOG2_E = math.log2(math.e)

VMEM_V7X_BYTES = 64 * 1024 * 1024
NT_DIMS = (((1,), (1,)), ((), ()))


def _params(semantics, vmem_bytes):
    assert vmem_bytes < VMEM_V7X_BYTES
    return pltpu.CompilerParams(dimension_semantics=semantics, vmem_limit_bytes=int(vmem_bytes))


def _rms(x, g, eps):
    return x * lax.rsqrt(jnp.mean(x * x, axis=-1, keepdims=True) + eps) * g


FFN_ROWS = 1024
FFN_COLS = 512


def _ffn_body(x_ref, g_ref, wg_ref, wu_ref, wo_ref, *rest, final):
    if final:
        fg_ref, o_ref, xn_ref = rest
    else:
        o_ref, xn_ref = rest
    j = pl.program_id(1)

    @pl.when(j == 0)
    def _():
        x = x_ref[...]
        xn_ref[...] = _rms(x, g_ref[...], NORM_EPS).astype(BF16)
        o_ref[...] = x

    xn = xn_ref[...]
    gate = jnp.dot(xn, wg_ref[...], preferred_element_type=F32)
    up = jnp.dot(xn, wu_ref[...], preferred_element_type=F32)
    act = (0.5 * gate / (1.0 + jnp.exp(-gate))) * up
    o_ref[...] += jnp.dot(act.astype(BF16), wo_ref[...], preferred_element_type=F32)

    if final:
        @pl.when(j == pl.num_programs(1) - 1)
        def _():
            o_ref[...] = _rms(o_ref[...], fg_ref[...], NORM_EPS)


def _ffn(h, g, w_in, w_out, final_g=None):
    n, d = h.shape
    d_ff = w_out.shape[0]
    tm, tf = FFN_ROWS, FFN_COLS
    nff = d_ff // tf
    assert n % tm == 0 and d_ff % tf == 0
    final = final_g is not None
    in_specs = [
        pl.BlockSpec((tm, d), lambda i, j: (i, 0)),
        pl.BlockSpec((1, d), lambda i, j: (0, 0)),
        pl.BlockSpec((d, tf), lambda i, j: (0, j)),
        pl.BlockSpec((d, tf), lambda i, j: (0, nff + j)),
        pl.BlockSpec((tf, d), lambda i, j: (j, 0)),
    ]
    args = [h, g.reshape(1, d), w_in, w_in, w_out]
    if final:
        in_specs.append(pl.BlockSpec((1, d), lambda i, j: (0, 0)))
        args.append(final_g.reshape(1, d))
    vmem = 2 * tm * d * 4 * 2 + tm * d * 2 + 2 * (2 * d * tf + tf * d) * 2 + 3 * tm * tf * 4 + (4 << 20)
    return pl.pallas_call(
        functools.partial(_ffn_body, final=final),
        grid=(n // tm, nff),
        in_specs=in_specs,
        out_specs=pl.BlockSpec((tm, d), lambda i, j: (i, 0)),
        out_shape=jax.ShapeDtypeStruct((n, d), F32),
        scratch_shapes=[pltpu.VMEM((tm, d), BF16)],
        compiler_params=_params(("parallel", "arbitrary"), vmem),
        name="ffn",
    )(*args)


PROJ_ROWS = 1024
PROJ_COLS = 1024


def _proj_body(x_ref, g_ref, w_ref, cos_ref, sin_ref, o_ref, xn_ref, *, q_tiles, q_scale):
    j = pl.program_id(1)

    @pl.when(j == 0)
    def _():
        xn_ref[...] = _rms(x_ref[...], g_ref[...], NORM_EPS).astype(BF16)

    y = jnp.dot(xn_ref[...], w_ref[...], preferred_element_type=F32)
    scale = jnp.where(j < q_tiles, q_scale, 1.0).astype(F32)
    cos = cos_ref[...] * scale
    sin = sin_ref[...] * scale
    for c in range(y.shape[1] // HEAD_DIM):
        yh = y[:, c * HEAD_DIM:(c + 1) * HEAD_DIM]
        rot = yh * cos + pltpu.roll(yh, HALF_DIM, 1) * sin
        o_ref[:, c * HEAD_DIM:(c + 1) * HEAD_DIM] = rot.astype(o_ref.dtype)


def _proj(h, g, w, cos, sin, col0, q_width, rope_width, width, q_scale, out_dtype):
    n, d = h.shape
    seq = cos.shape[1]
    tm, tn = PROJ_ROWS, PROJ_COLS
    assert n % tm == 0 and seq % tm == 0
    assert all(c % tn == 0 for c in (col0, q_width, rope_width, width))
    pos_blocks = seq // tm
    rope_tiles = rope_width // tn
    body = functools.partial(_proj_body, q_tiles=q_width // tn, q_scale=q_scale)

    def table_spec():
        return pl.BlockSpec((None, tm, HEAD_DIM),
                            lambda i, j: (jnp.where(j < rope_tiles, 0, 1), i % pos_blocks, 0))

    vmem = 2 * tm * d * 4 + tm * d * 2 + 2 * d * tn * 2 + 4 * tm * tn * 4 + 4 * tm * HEAD_DIM * 4 + (8 << 20)
    return pl.pallas_call(
        body,
        grid=(n // tm, width // tn),
        in_specs=[
            pl.BlockSpec((tm, d), lambda i, j: (i, 0)),
            pl.BlockSpec((1, d), lambda i, j: (0, 0)),
            pl.BlockSpec((d, tn), lambda i, j: (0, col0 // tn + j)),
            table_spec(),
            table_spec(),
        ],
        out_specs=pl.BlockSpec((tm, tn), lambda i, j: (i, j)),
        out_shape=jax.ShapeDtypeStruct((n, width), out_dtype),
        scratch_shapes=[pltpu.VMEM((tm, d), BF16)],
        compiler_params=_params(("parallel", "arbitrary"), vmem),
        name="proj",
    )(h, g.reshape(1, d), w, cos, sin)


def _rope_tables(seq):
    inv_freq = ROPE_THETA ** (-jnp.arange(HALF_DIM, dtype=F32) / HALF_DIM)
    ang = jnp.arange(seq, dtype=F32)[:, None] * inv_freq[None, :]
    cos, sin = jnp.cos(ang), jnp.sin(ang)
    cos = jnp.concatenate([cos, cos], axis=-1)
    sin = jnp.concatenate([-sin, sin], axis=-1)
    return jnp.stack([cos, jnp.ones_like(cos)]), jnp.stack([sin, jnp.zeros_like(sin)])


OPROJ_ROWS = 1024
OPROJ_COLS = 1024


def _oproj_body(*refs):
    *aw, h_ref, o_ref = refs
    acc = h_ref[...]
    for a_ref, w_ref in zip(aw[0::2], aw[1::2]):
        acc = acc + jnp.dot(a_ref[...], w_ref[...], preferred_element_type=F32)
    o_ref[...] = acc


def _oproj(h, parts, w):
    n, d = h.shape
    tm, tn = OPROJ_ROWS, OPROJ_COLS
    assert n % tm == 0 and d % tn == 0
    in_specs, args, vmem, row0 = [], [], 0, 0
    for a in parts:
        kdim = a.shape[1]
        assert row0 % kdim == 0
        in_specs += [pl.BlockSpec((tm, kdim), lambda i, j: (i, 0)),
                     pl.BlockSpec((kdim, tn), lambda i, j, r=row0 // kdim: (r, j))]
        args += [a, w]
        vmem += 2 * (tm * kdim + kdim * tn) * 2
        row0 += kdim
    assert row0 == w.shape[0]
    in_specs.append(pl.BlockSpec((tm, tn), lambda i, j: (i, j)))
    args.append(h)
    vmem += 6 * tm * tn * 4 + (8 << 20)
    return pl.pallas_call(
        _oproj_body,
        grid=(n // tm, d // tn),
        in_specs=in_specs,
        out_specs=pl.BlockSpec((tm, tn), lambda i, j: (i, j)),
        out_shape=jax.ShapeDtypeStruct((n, d), F32),
        compiler_params=_params(("parallel", "parallel"), vmem),
        name="oproj",
    )(*args)


def _dilated_body(q_ref, kc_ref, kp_ref, vc_ref, vp_ref, o_ref, kbuf, vbuf, *branch_bufs):
    obufs, lbufs = branch_bufs[:3], branch_bufs[3:]
    sb = pl.program_id(2)
    sup = DIL_SUPER
    kbuf[0:sup, :] = kp_ref[...]
    kbuf[sup:2 * sup, :] = kc_ref[...]
    vbuf[0:sup, :] = vp_ref[...]
    vbuf[sup:2 * sup, :] = vc_ref[...]

    row = lax.broadcasted_iota(jnp.int32, (DIL_BLOCK, 2 * DIL_BLOCK), 0)
    col = lax.broadcasted_iota(jnp.int32, (DIL_BLOCK, 2 * DIL_BLOCK), 1)
    band = jnp.logical_and(col >= row + (DIL_BLOCK - DIL_REACH), col <= row + DIL_BLOCK)

    for br, dil in enumerate(DILATIONS):
        nsub = sup // (DIL_BLOCK * dil)
        obuf, lbuf = obufs[br], lbufs[br]

        def block(t, carry, dil=dil, nsub=nsub, obuf=obuf, lbuf=lbuf):
            res = t // nsub
            sub = t % nsub
            q0 = res + dil * DIL_BLOCK * sub
            k0 = sup + q0 - dil * DIL_BLOCK
            if dil == 1:
                q0 = pl.multiple_of(q0, DIL_BLOCK)
                k0 = pl.multiple_of(k0, DIL_BLOCK)
                qs, ks = pl.ds(q0, DIL_BLOCK), pl.ds(k0, 2 * DIL_BLOCK)
            else:
                qs, ks = pl.ds(q0, DIL_BLOCK, stride=dil), pl.ds(k0, 2 * DIL_BLOCK, stride=dil)
            qb = q_ref[qs, :].astype(BF16)
            kb = kbuf[ks, :].astype(BF16)
            vb = vbuf[ks, :].astype(BF16)
            s = lax.dot_general(qb, kb, NT_DIMS, preferred_element_type=F32)
            has_prev = jnp.logical_or(sb > 0, sub > 0)
            first_col = jnp.where(has_prev, 0, DIL_BLOCK)
            s = jnp.where(jnp.logical_and(band, col >= first_col), s, NEG_INF)
            m = jnp.max(s, axis=-1, keepdims=True)
            p = jnp.exp(s - m)
            den = jnp.sum(p, axis=-1, keepdims=True)
            o = jnp.dot(p.astype(BF16), vb, preferred_element_type=F32) / den
            obuf[qs, :] = o
            lbuf[qs, :] = jnp.broadcast_to(m + jnp.log(den), (DIL_BLOCK, HEAD_DIM))
            return carry

        lax.fori_loop(0, nsub * dil, block, 0, unroll=DIL_UNROLL)

    lses = [lbuf[...] for lbuf in lbufs]
    top = jnp.maximum(jnp.maximum(lses[0], lses[1]), lses[2])
    wts = [jnp.exp(l - top) for l in lses]
    mix = wts[0] * obufs[0][...] + wts[1] * obufs[1][...] + wts[2] * obufs[2][...]
    o_ref[...] = (mix / (wts[0] + wts[1] + wts[2])).astype(o_ref.dtype)


def _dilated(qkv, batch, seq):
    sup = DIL_SUPER
    assert seq % sup == 0
    nsb = seq // sup
    heads = N_HEADS_A

    def cur(col0):
        return pl.BlockSpec((sup, HEAD_DIM), lambda b, h, s: (b * nsb + s, col0 + h))

    def prev(col0):
        return pl.BlockSpec((sup, HEAD_DIM), lambda b, h, s: (b * nsb + jnp.maximum(s - 1, 0), col0 + h))

    blk = sup * HEAD_DIM * 4
    vmem = 2 * 5 * blk + 2 * sup * HEAD_DIM * 2 + 4 * blk + 6 * blk + 8 * blk + (8 << 20)
    return pl.pallas_call(
        _dilated_body,
        grid=(batch, heads, nsb),
        in_specs=[cur(0), cur(heads), prev(heads), cur(2 * heads), prev(2 * heads)],
        out_specs=pl.BlockSpec((sup, HEAD_DIM), lambda b, h, s: (b * nsb + s, h)),
        out_shape=jax.ShapeDtypeStruct((batch * seq, WIDTH_A), BF16),
        scratch_shapes=[pltpu.VMEM((2 * sup, HEAD_DIM), F32)] * 2 + [pltpu.VMEM((sup, HEAD_DIM), F32)] * 6,
        compiler_params=_params(("parallel", "parallel", "arbitrary"), vmem),
        name="dilated",
    )(qkv, qkv, qkv, qkv, qkv)


def _scores_t(k_blk, q):
    return lax.dot_general(k_blk, q, NT_DIMS, preferred_element_type=F32)


def _flash_t(streams, qi, tile, diag_mask, past_mask):
    own = pl.multiple_of(qi * tile, tile)
    for q, k_ref, _, p_ref, acc_ref, st_ref in streams:
        s = diag_mask(_scores_t(k_ref[pl.ds(own, tile), :], q))
        m = jnp.max(s, axis=0, keepdims=True)
        p = jnp.exp2(s - m)
        p_ref[0] = p.astype(BF16)
        acc_ref[...] = jnp.zeros(acc_ref.shape, F32)
        st_ref[0] = m
        st_ref[1] = jnp.sum(p, axis=0, keepdims=True)
        st_ref[2] = jnp.ones_like(m)

    def flush(pending, slot):
        for _, _, vt_ref, p_ref, acc_ref, st_ref in streams:
            acc_ref[...] = st_ref[2] * acc_ref[...] + jnp.dot(vt_ref[pending], p_ref[slot],
                                                              preferred_element_type=F32)

    def step(j, pending, rd, wr):
        start = pl.multiple_of(j * tile, tile)
        scores = [_scores_t(st[1][pl.ds(start, tile), :], st[0]) for st in streams]
        flush(pending, rd)
        for s, (_, _, _, p_ref, _, st_ref) in zip(scores, streams):
            s = past_mask(s, j)
            m = st_ref[0]
            m_new = jnp.maximum(m, jnp.max(s, axis=0, keepdims=True))
            alpha = jnp.exp2(m - m_new)
            p = jnp.exp2(s - m_new)
            p_ref[wr] = p.astype(BF16)
            st_ref[0] = m_new
            st_ref[1] = alpha * st_ref[1] + jnp.sum(p, axis=0, keepdims=True)
            st_ref[2] = alpha

    def pair(i, carry):
        step(2 * i, jnp.where(i == 0, qi, 2 * i - 1), 0, 1)
        step(2 * i + 1, 2 * i, 1, 0)
        return carry

    npairs = qi // 2
    lax.fori_loop(0, npairs, pair, 0)
    odd = qi % 2 == 1
    last = jnp.maximum(qi - 1, 0)

    @pl.when(odd)
    def _():
        step(last, jnp.where(qi == 1, qi, qi - 2), 0, 1)
        flush(last, 1)

    @pl.when(jnp.logical_not(odd))
    def _():
        flush(last, 0)

    return [(st[4][...], st[5][1]) for st in streams]


def _store_transposed_blocks(v_ref, vt_ref):
    nblk, _, rows = vt_ref.shape
    for n in range(nblk):
        vt_ref[n] = v_ref[n * rows:(n + 1) * rows, :].astype(F32).T.astype(vt_ref.dtype)


def _moba_body(q_ref, k_ref, v_ref, o_ref, kmean_hi, kmean_lo, vt_ref, sel_ref, p_ref, acc_ref, st_ref):
    qi = pl.program_id(2)
    blk, tile = MOBA_BLOCK, MOBA_TILE
    nb = k_ref.shape[0] // blk

    @pl.when(qi == 0)
    def _():
        for n in range(nb):
            mean = jnp.mean(k_ref[n * blk:(n + 1) * blk, :].astype(F32), axis=0, keepdims=True)
            hi = mean.astype(BF16)
            kmean_hi[n:n + 1, :] = hi
            kmean_lo[n:n + 1, :] = (mean - hi.astype(F32)).astype(BF16)
        _store_transposed_blocks(v_ref, vt_ref)

    q = q_ref[...]
    gate = _scores_t(kmean_hi[...], q) + _scores_t(kmean_lo[...], q)
    blk_id = lax.broadcasted_iota(jnp.int32, gate.shape, 0)
    q_blk = 2 * qi + lax.broadcasted_iota(jnp.int32, gate.shape, 1) // blk
    past = blk_id < q_blk
    work = jnp.where(past, gate, NEG_INF)
    chosen = jnp.zeros(gate.shape, F32)
    for _ in range(MOBA_TOPK):
        best = jnp.max(work, axis=0, keepdims=True)
        first = jnp.min(jnp.where(work == best, blk_id, nb), axis=0, keepdims=True)
        pick = blk_id == first
        chosen = jnp.where(pick, 1.0, chosen)
        work = jnp.where(pick, -jnp.inf, work)
    sel_ref[...] = jnp.where(past, chosen, 0.0)

    def picked(block):
        return sel_ref[pl.ds(block, 1), :] > 0.5

    def past_mask(s, j):
        return jnp.concatenate([jnp.where(picked(2 * j), s[:blk], NEG_INF),
                                jnp.where(picked(2 * j + 1), s[blk:], NEG_INF)], axis=0)

    def diag_mask(s):
        key = lax.broadcasted_iota(jnp.int32, (tile, tile), 0)
        qry = lax.broadcasted_iota(jnp.int32, (tile, tile), 1)
        other = jnp.where(key < blk, jnp.where(picked(2 * qi), s, NEG_INF), NEG_INF)
        own_start = (qry // blk) * blk
        return jnp.where(key <= qry, jnp.where(key >= own_start, s, other), other)

    [(acc, l)] = _flash_t([(q, k_ref, vt_ref, p_ref, acc_ref, st_ref)], qi, tile, diag_mask, past_mask)
    o_ref[...] = (acc * (1.0 / l)).T.astype(o_ref.dtype)


def _moba(qkv, batch, seq):
    blk, tile = MOBA_BLOCK, MOBA_TILE
    assert seq % tile == 0
    nq = seq // tile
    nb = seq // blk
    heads = N_HEADS_B
    vmem = (2 * 2 * seq * HEAD_DIM * 2 + seq * HEAD_DIM * 2 + 4 * tile * HEAD_DIM * 2 + 2 * tile * tile * 2
            + HEAD_DIM * tile * 4 + 10 * tile * tile * 4 + (8 << 20))
    return pl.pallas_call(
        _moba_body,
        grid=(batch, heads, nq),
        in_specs=[
            pl.BlockSpec((tile, HEAD_DIM), lambda b, h, i: (b * nq + i, h)),
            pl.BlockSpec((seq, HEAD_DIM), lambda b, h, i: (b, heads + h)),
            pl.BlockSpec((seq, HEAD_DIM), lambda b, h, i: (b, 2 * heads + h)),
        ],
        out_specs=pl.BlockSpec((tile, HEAD_DIM), lambda b, h, i: (b * nq + i, h)),
        out_shape=jax.ShapeDtypeStruct((batch * seq, WIDTH_B), BF16),
        scratch_shapes=[pltpu.VMEM((nb, HEAD_DIM), BF16)] * 2
        + [pltpu.VMEM((nq, HEAD_DIM, tile), BF16), pltpu.VMEM((nb, tile), F32),
           pltpu.VMEM((2, tile, tile), BF16), pltpu.VMEM((HEAD_DIM, tile), F32),
           pltpu.VMEM((3, 1, tile), F32)],
        compiler_params=_params(("parallel", "parallel", "arbitrary"), vmem),
        name="moba",
    )(qkv, qkv, qkv)


def _diff_body(lq1_ref, lk1_ref, lq2_ref, lk2_ref, g_ref, q1_ref, q2_ref, k1_ref, k2_ref, v_ref, o_ref,
               vt_ref, p1_ref, p2_ref, acc1_ref, acc2_ref, st1_ref, st2_ref, *, lambda_init):
    qi = pl.program_id(2)
    t = DIFF_BLOCK

    @pl.when(qi == 0)
    def _():
        _store_transposed_blocks(v_ref, vt_ref)

    def diag_mask(s):
        key = lax.broadcasted_iota(jnp.int32, (t, t), 0)
        qry = lax.broadcasted_iota(jnp.int32, (t, t), 1)
        return jnp.where(key <= qry, s, NEG_INF)

    streams = [(q1_ref[...], k1_ref, vt_ref, p1_ref, acc1_ref, st1_ref),
               (q2_ref[...], k2_ref, vt_ref, p2_ref, acc2_ref, st2_ref)]
    (a1, l1), (a2, l2) = _flash_t(streams, qi, t, diag_mask, lambda s, j: s)

    lam = (jnp.exp(jnp.sum(lq1_ref[...] * lk1_ref[...], axis=-1, keepdims=True))
           - jnp.exp(jnp.sum(lq2_ref[...] * lk2_ref[...], axis=-1, keepdims=True)) + lambda_init)
    o = a1 * (1.0 / l1) - lam * (a2 * (1.0 / l2))
    o = o * lax.rsqrt(jnp.mean(o * o, axis=0, keepdims=True) + SUBLN_EPS)
    o_ref[...] = (o.T * g_ref[...] * (1.0 - lambda_init)).astype(o_ref.dtype)


def _diff(qkv, lq1, lk1, lq2, lk2, subln_g, lambda_init, batch, seq):
    t = DIFF_BLOCK
    assert seq % t == 0
    nq = seq // t
    heads = N_HEADS_C
    dv = 2 * HEAD_DIM
    kcol = D_MODEL // HEAD_DIM
    vcol = 2 * D_MODEL // dv

    def vec(width):
        return pl.BlockSpec((1, width), lambda b, h, i: (0, 0))

    def q_spec(part):
        return pl.BlockSpec((t, HEAD_DIM), lambda b, h, i: (b * nq + i, 2 * h + part))

    def k_spec(part):
        return pl.BlockSpec((seq, HEAD_DIM), lambda b, h, i: (b, kcol + 2 * h + part))

    vmem = (2 * (2 * seq * HEAD_DIM + seq * dv) * 2 + seq * dv * 2 + 8 * t * HEAD_DIM * 2 + 2 * t * dv * 2
            + 2 * t * dv * 4 + 4 * t * t * 2 + 10 * t * t * 4 + (8 << 20))
    return pl.pallas_call(
        functools.partial(_diff_body, lambda_init=lambda_init),
        grid=(batch, heads, nq),
        in_specs=[vec(HEAD_DIM)] * 4 + [vec(dv), q_spec(0), q_spec(1), k_spec(0), k_spec(1),
                                        pl.BlockSpec((seq, dv), lambda b, h, i: (b, vcol + h))],
        out_specs=pl.BlockSpec((t, dv), lambda b, h, i: (b * nq + i, h)),
        out_shape=jax.ShapeDtypeStruct((batch * seq, heads * dv), BF16),
        scratch_shapes=[pltpu.VMEM((nq, dv, t), BF16)] + [pltpu.VMEM((2, t, t), BF16)] * 2
        + [pltpu.VMEM((dv, t), F32)] * 2 + [pltpu.VMEM((3, 1, t), F32)] * 2,
        compiler_params=_params(("parallel", "parallel", "arbitrary"), vmem),
        name="diffattn",
    )(lq1.reshape(1, -1), lk1.reshape(1, -1), lq2.reshape(1, -1), lk2.reshape(1, -1),
      subln_g.reshape(1, -1), qkv, qkv, qkv, qkv, qkv)


def _lambda_init(layer):
    return 0.8 - 0.6 * math.exp(-0.3 * layer)


def kernel(x, ffa_norm, ffa_w_in, ffa_w_out, mix_norm, even_w_in, even_w_out, odd_w_in, odd_w_out,
           lambda_q1, lambda_k1, lambda_q2, lambda_k2, subln_norm, ffb_norm, ffb_w_in, ffb_w_out, final_norm):
    batch, seq, d = x.shape
    depth = ffa_norm.shape[0]
    cos, sin = _rope_tables(seq)
    h = x.reshape(batch * seq, d)
    for layer in range(depth):
        i = layer // 2
        h = _ffn(h, ffa_norm[layer], ffa_w_in[layer].astype(BF16), ffa_w_out[layer].astype(BF16))
        if layer % 2 == 0:
            w_in = even_w_in[i].astype(BF16)
            wa, wb = 3 * WIDTH_A, 3 * WIDTH_B
            qkv_a = _proj(h, mix_norm[layer], w_in, cos, sin, 0, WIDTH_A, 2 * WIDTH_A, wa, SCALE, F32)
            qkv_b = _proj(h, mix_norm[layer], w_in, cos, sin, wa, WIDTH_B, 2 * WIDTH_B, wb, SCALE * LOG2_E, BF16)
            o_a = _dilated(qkv_a, batch, seq)
            o_b = _moba(qkv_b, batch, seq)
            h = _oproj(h, [o_a, o_b], even_w_out[i].astype(BF16))
        else:
            qkv = _proj(h, mix_norm[layer], odd_w_in[i].astype(BF16), cos, sin,
                        0, D_MODEL, 2 * D_MODEL, 3 * D_MODEL, SCALE * LOG2_E, BF16)
            o = _diff(qkv, lambda_q1[i], lambda_k1[i], lambda_q2[i], lambda_k2[i], subln_norm[i],
                      _lambda_init(layer), batch, seq)
            h = _oproj(h, [o], odd_w_out[i].astype(BF16))
        last = layer == depth - 1
        h = _ffn(h, ffb_norm[layer], ffb_w_in[layer].astype(BF16), ffb_w_out[layer].astype(BF16),
                 final_norm if last else None)
    return h.reshape(batch, seq, d)
```

```python
import functools
import math

import jax
import jax.numpy as jnp
from jax import lax
from jax.experimental import pallas as pl
from jax.experimental.pallas import tpu as pltpu

F32 = jnp.float32
BF16 = jnp.bfloat16

D_MODEL = 2048
HEAD_DIM = 128
HALF_DIM = HEAD_DIM // 2
N_HEADS_A = 8
N_HEADS_B = 8
N_HEADS_C = 8
WIDTH_A = N_HEADS_A * HEAD_DIM
WIDTH_B = N_HEADS_B * HEAD_DIM
DILATIONS = (1, 4, 16)
DIL_BLOCK = 128
DIL_REACH = 128
DIL_SUPER = DIL_BLOCK * DILATIONS[-1]
DIL_UNROLL = 16
MOBA_BLOCK = 256
MOBA_TOPK = 3
MOBA_TILE = 2 * MOBA_BLOCK
DIFF_BLOCK = 512
ROPE_THETA = 10000.0
NORM_EPS = 1e-6
SUBLN_EPS = 1e-5
NEG_INF = -1e30
SCALE = HEAD_DIM ** -0.5
LOG2_E = math.log2(math.e)

VMEM_V7X_BYTES = 64 * 1024 * 1024
NT_DIMS = (((1,), (1,)), ((), ()))
ONES_ROWS = 16


def _params(semantics, vmem_bytes):
    assert vmem_bytes < VMEM_V7X_BYTES
    return pltpu.CompilerParams(dimension_semantics=semantics, vmem_limit_bytes=int(vmem_bytes))


def _rms(x, g, eps):
    return x * lax.rsqrt(jnp.mean(x * x, axis=-1, keepdims=True) + eps) * g


FFN_ROWS = 1024
FFN_COLS = 512


def _ffn_body(x_ref, g_ref, wg_ref, wu_ref, wo_ref, *rest, final):
    if final:
        fg_ref, o_ref, xn_ref = rest
    else:
        o_ref, xn_ref = rest
    j = pl.program_id(1)

    @pl.when(j == 0)
    def _():
        x = x_ref[...]
        xn_ref[...] = _rms(x, g_ref[...], NORM_EPS).astype(BF16)
        o_ref[...] = x

    xn = xn_ref[...]
    gate = jnp.dot(xn, wg_ref[...], preferred_element_type=F32)
    up = jnp.dot(xn, wu_ref[...], preferred_element_type=F32)
    act = (0.5 * gate / (1.0 + jnp.exp(-gate))) * up
    o_ref[...] += jnp.dot(act.astype(BF16), wo_ref[...], preferred_element_type=F32)

    if final:
        @pl.when(j == pl.num_programs(1) - 1)
        def _():
            o_ref[...] = _rms(o_ref[...], fg_ref[...], NORM_EPS)


def _ffn(h, g, w_in, w_out, final_g=None):
    n, d = h.shape
    d_ff = w_out.shape[0]
    tm, tf = FFN_ROWS, FFN_COLS
    nff = d_ff // tf
    assert n % tm == 0 and d_ff % tf == 0
    final = final_g is not None
    in_specs = [
        pl.BlockSpec((tm, d), lambda i, j: (i, 0)),
        pl.BlockSpec((1, d), lambda i, j: (0, 0)),
        pl.BlockSpec((d, tf), lambda i, j: (0, j)),
        pl.BlockSpec((d, tf), lambda i, j: (0, nff + j)),
        pl.BlockSpec((tf, d), lambda i, j: (j, 0)),
    ]
    args = [h, g.reshape(1, d), w_in, w_in, w_out]
    if final:
        in_specs.append(pl.BlockSpec((1, d), lambda i, j: (0, 0)))
        args.append(final_g.reshape(1, d))
    vmem = 2 * tm * d * 4 * 2 + tm * d * 2 + 2 * (2 * d * tf + tf * d) * 2 + 3 * tm * tf * 4 + (4 << 20)
    return pl.pallas_call(
        functools.partial(_ffn_body, final=final),
        grid=(n // tm, nff),
        in_specs=in_specs,
        out_specs=pl.BlockSpec((tm, d), lambda i, j: (i, 0)),
        out_shape=jax.ShapeDtypeStruct((n, d), F32),
        scratch_shapes=[pltpu.VMEM((tm, d), BF16)],
        compiler_params=_params(("parallel", "arbitrary"), vmem),
        name="ffn",
    )(*args)


PROJ_ROWS = 1024
PROJ_COLS = 1024


def _proj_body(x_ref, g_ref, w_ref, cos_ref, sin_ref, o_ref, xn_ref, *, q_tiles, q_scale):
    j = pl.program_id(1)

    @pl.when(j == 0)
    def _():
        xn_ref[...] = _rms(x_ref[...], g_ref[...], NORM_EPS).astype(BF16)

    y = jnp.dot(xn_ref[...], w_ref[...], preferred_element_type=F32)
    scale = jnp.where(j < q_tiles, q_scale, 1.0).astype(F32)
    cos = cos_ref[...] * scale
    sin = sin_ref[...] * scale
    for c in range(y.shape[1] // HEAD_DIM):
        yh = y[:, c * HEAD_DIM:(c + 1) * HEAD_DIM]
        rot = yh * cos + pltpu.roll(yh, HALF_DIM, 1) * sin
        o_ref[:, c * HEAD_DIM:(c + 1) * HEAD_DIM] = rot.astype(o_ref.dtype)


def _proj(h, g, w, cos, sin, col0, q_width, rope_width, width, q_scale, out_dtype):
    n, d = h.shape
    seq = cos.shape[1]
    tm, tn = PROJ_ROWS, PROJ_COLS
    assert n % tm == 0 and seq % tm == 0
    assert all(c % tn == 0 for c in (col0, q_width, rope_width, width))
    pos_blocks = seq // tm
    rope_tiles = rope_width // tn
    body = functools.partial(_proj_body, q_tiles=q_width // tn, q_scale=q_scale)

    def table_spec():
        return pl.BlockSpec((None, tm, HEAD_DIM),
                            lambda i, j: (jnp.where(j < rope_tiles, 0, 1), i % pos_blocks, 0))

    vmem = 2 * tm * d * 4 + tm * d * 2 + 2 * d * tn * 2 + 4 * tm * tn * 4 + 4 * tm * HEAD_DIM * 4 + (8 << 20)
    return pl.pallas_call(
        body,
        grid=(n // tm, width // tn),
        in_specs=[
            pl.BlockSpec((tm, d), lambda i, j: (i, 0)),
            pl.BlockSpec((1, d), lambda i, j: (0, 0)),
            pl.BlockSpec((d, tn), lambda i, j: (0, col0 // tn + j)),
            table_spec(),
            table_spec(),
        ],
        out_specs=pl.BlockSpec((tm, tn), lambda i, j: (i, j)),
        out_shape=jax.ShapeDtypeStruct((n, width), out_dtype),
        scratch_shapes=[pltpu.VMEM((tm, d), BF16)],
        compiler_params=_params(("parallel", "arbitrary"), vmem),
        name="proj",
    )(h, g.reshape(1, d), w, cos, sin)


def _rope_tables(seq):
    inv_freq = ROPE_THETA ** (-jnp.arange(HALF_DIM, dtype=F32) / HALF_DIM)
    ang = jnp.arange(seq, dtype=F32)[:, None] * inv_freq[None, :]
    cos, sin = jnp.cos(ang), jnp.sin(ang)
    cos = jnp.concatenate([cos, cos], axis=-1)
    sin = jnp.concatenate([-sin, sin], axis=-1)
    return jnp.stack([cos, jnp.ones_like(cos)]), jnp.stack([sin, jnp.zeros_like(sin)])


OPROJ_ROWS = 1024
OPROJ_COLS = 1024


def _oproj_body(*refs):
    *aw, h_ref, o_ref = refs
    acc = h_ref[...]
    for a_ref, w_ref in zip(aw[0::2], aw[1::2]):
        acc = acc + jnp.dot(a_ref[...], w_ref[...], preferred_element_type=F32)
    o_ref[...] = acc


def _oproj(h, parts, w):
    n, d = h.shape
    tm, tn = OPROJ_ROWS, OPROJ_COLS
    assert n % tm == 0 and d % tn == 0
    in_specs, args, vmem, row0 = [], [], 0, 0
    for a in parts:
        kdim = a.shape[1]
        assert row0 % kdim == 0
        in_specs += [pl.BlockSpec((tm, kdim), lambda i, j: (i, 0)),
                     pl.BlockSpec((kdim, tn), lambda i, j, r=row0 // kdim: (r, j))]
        args += [a, w]
        vmem += 2 * (tm * kdim + kdim * tn) * 2
        row0 += kdim
    assert row0 == w.shape[0]
    in_specs.append(pl.BlockSpec((tm, tn), lambda i, j: (i, j)))
    args.append(h)
    vmem += 6 * tm * tn * 4 + (8 << 20)
    return pl.pallas_call(
        _oproj_body,
        grid=(n // tm, d // tn),
        in_specs=in_specs,
        out_specs=pl.BlockSpec((tm, tn), lambda i, j: (i, j)),
        out_shape=jax.ShapeDtypeStruct((n, d), F32),
        compiler_params=_params(("parallel", "parallel"), vmem),
        name="oproj",
    )(*args)


def _dilated_body(q_ref, kc_ref, kp_ref, vc_ref, vp_ref, o_ref, kbuf, vbuf, *branch_bufs):
    obufs, lbufs = branch_bufs[:3], branch_bufs[3:]
    sb = pl.program_id(2)
    sup = DIL_SUPER
    kbuf[0:sup, :] = kp_ref[...]
    kbuf[sup:2 * sup, :] = kc_ref[...]
    vbuf[0:sup, :] = vp_ref[...]
    vbuf[sup:2 * sup, :] = vc_ref[...]

    row = lax.broadcasted_iota(jnp.int32, (DIL_BLOCK, 2 * DIL_BLOCK), 0)
    col = lax.broadcasted_iota(jnp.int32, (DIL_BLOCK, 2 * DIL_BLOCK), 1)
    band = jnp.logical_and(col >= row + (DIL_BLOCK - DIL_REACH), col <= row + DIL_BLOCK)

    for br, dil in enumerate(DILATIONS):
        nsub = sup // (DIL_BLOCK * dil)
        obuf, lbuf = obufs[br], lbufs[br]

        def block(t, carry, dil=dil, nsub=nsub, obuf=obuf, lbuf=lbuf):
            res = t // nsub
            sub = t % nsub
            q0 = res + dil * DIL_BLOCK * sub
            k0 = sup + q0 - dil * DIL_BLOCK
            if dil == 1:
                q0 = pl.multiple_of(q0, DIL_BLOCK)
                k0 = pl.multiple_of(k0, DIL_BLOCK)
                qs, ks = pl.ds(q0, DIL_BLOCK), pl.ds(k0, 2 * DIL_BLOCK)
            else:
                qs, ks = pl.ds(q0, DIL_BLOCK, stride=dil), pl.ds(k0, 2 * DIL_BLOCK, stride=dil)
            qb = q_ref[qs, :].astype(BF16)
            kb = kbuf[ks, :].astype(BF16)
            vb = vbuf[ks, :].astype(BF16)
            s = lax.dot_general(qb, kb, NT_DIMS, preferred_element_type=F32)
            has_prev = jnp.logical_or(sb > 0, sub > 0)
            first_col = jnp.where(has_prev, 0, DIL_BLOCK)
            s = jnp.where(jnp.logical_and(band, col >= first_col), s, NEG_INF)
            m = jnp.max(s, axis=-1, keepdims=True)
            p = jnp.exp(s - m)
            den = jnp.sum(p, axis=-1, keepdims=True)
            o = jnp.dot(p.astype(BF16), vb, preferred_element_type=F32) / den
            obuf[qs, :] = o
            lbuf[qs, :] = jnp.broadcast_to(m + jnp.log(den), (DIL_BLOCK, HEAD_DIM))
            return carry

        lax.fori_loop(0, nsub * dil, block, 0, unroll=DIL_UNROLL)

    lses = [lbuf[...] for lbuf in lbufs]
    top = jnp.maximum(jnp.maximum(lses[0], lses[1]), lses[2])
    wts = [jnp.exp(l - top) for l in lses]
    mix = wts[0] * obufs[0][...] + wts[1] * obufs[1][...] + wts[2] * obufs[2][...]
    o_ref[...] = (mix / (wts[0] + wts[1] + wts[2])).astype(o_ref.dtype)


def _dilated(qkv, batch, seq):
    sup = DIL_SUPER
    assert seq % sup == 0
    nsb = seq // sup
    heads = N_HEADS_A

    def cur(col0):
        return pl.BlockSpec((sup, HEAD_DIM), lambda b, h, s: (b * nsb + s, col0 + h))

    def prev(col0):
        return pl.BlockSpec((sup, HEAD_DIM), lambda b, h, s: (b * nsb + jnp.maximum(s - 1, 0), col0 + h))

    blk = sup * HEAD_DIM * 4
    vmem = 2 * 5 * blk + 2 * sup * HEAD_DIM * 2 + 4 * blk + 6 * blk + 8 * blk + (8 << 20)
    return pl.pallas_call(
        _dilated_body,
        grid=(batch, heads, nsb),
        in_specs=[cur(0), cur(heads), prev(heads), cur(2 * heads), prev(2 * heads)],
        out_specs=pl.BlockSpec((sup, HEAD_DIM), lambda b, h, s: (b * nsb + s, h)),
        out_shape=jax.ShapeDtypeStruct((batch * seq, WIDTH_A), BF16),
        scratch_shapes=[pltpu.VMEM((2 * sup, HEAD_DIM), F32)] * 2 + [pltpu.VMEM((sup, HEAD_DIM), F32)] * 6,
        compiler_params=_params(("parallel", "parallel", "arbitrary"), vmem),
        name="dilated",
    )(qkv, qkv, qkv, qkv, qkv)


def _scores_t(k_blk, q):
    return lax.dot_general(k_blk, q, NT_DIMS, preferred_element_type=F32)


def _flash_t(streams, qi, tile, diag_mask, past_mask):
    own = pl.multiple_of(qi * tile, tile)
    for q, k_ref, _, p_ref, acc_ref, st_ref in streams:
        s = diag_mask(_scores_t(k_ref[pl.ds(own, tile), :], q)).astype(BF16)
        m = jnp.max(s, axis=0, keepdims=True)
        p_ref[0] = jnp.exp2(s - m)
        m = m.astype(F32)
        acc_ref[...] = jnp.zeros(acc_ref.shape, F32)
        st_ref[0] = m
        st_ref[1] = jnp.ones_like(m)

    def flush(pending, slot):
        for _, _, vt_ref, p_ref, acc_ref, st_ref in streams:
            acc_ref[...] = st_ref[1] * acc_ref[...] + jnp.dot(vt_ref[pending], p_ref[slot],
                                                              preferred_element_type=F32)

    def step(j, pending, rd, wr):
        start = pl.multiple_of(j * tile, tile)
        scores = [_scores_t(st[1][pl.ds(start, tile), :], st[0]) for st in streams]
        flush(pending, rd)
        for s, (_, _, _, p_ref, _, st_ref) in zip(scores, streams):
            s = past_mask(s, j).astype(BF16)
            m = st_ref[0]
            m_new = jnp.maximum(m, jnp.max(s, axis=0, keepdims=True).astype(F32))
            alpha = jnp.exp2(m - m_new)
            p_ref[wr] = jnp.exp2(s - m_new.astype(BF16))
            st_ref[0] = m_new
            st_ref[1] = alpha

    def pair(i, carry):
        step(2 * i, jnp.where(i == 0, qi, 2 * i - 1), 0, 1)
        step(2 * i + 1, 2 * i, 1, 0)
        return carry

    npairs = qi // 2
    lax.fori_loop(0, npairs, pair, 0)
    odd = qi % 2 == 1
    last = jnp.maximum(qi - 1, 0)

    @pl.when(odd)
    def _():
        step(last, jnp.where(qi == 1, qi, qi - 2), 0, 1)
        flush(last, 1)

    @pl.when(jnp.logical_not(odd))
    def _():
        flush(last, 0)

    outs = []
    for _, _, _, _, acc_ref, _ in streams:
        dv = acc_ref.shape[0] - ONES_ROWS
        outs.append((acc_ref[0:dv, :], acc_ref[dv:dv + 1, :]))
    return outs


def _store_transposed_blocks(v_ref, vt_ref):
    nblk, rows_t, rows = vt_ref.shape
    dv = rows_t - ONES_ROWS
    for n in range(nblk):
        vt_ref[n, 0:dv, :] = v_ref[n * rows:(n + 1) * rows, :].astype(F32).T.astype(vt_ref.dtype)
        vt_ref[n, dv:rows_t, :] = jnp.ones((ONES_ROWS, rows), vt_ref.dtype)


def _moba_body(q_ref, k_ref, v_ref, o_ref, kmean_hi, kmean_lo, vt_ref, sel_ref, p_ref, acc_ref, st_ref):
    qi = pl.program_id(2)
    blk, tile = MOBA_BLOCK, MOBA_TILE
    nb = k_ref.shape[0] // blk

    @pl.when(qi == 0)
    def _():
        for n in range(nb):
            mean = jnp.mean(k_ref[n * blk:(n + 1) * blk, :].astype(F32), axis=0, keepdims=True)
            hi = mean.astype(BF16)
            kmean_hi[n:n + 1, :] = hi
            kmean_lo[n:n + 1, :] = (mean - hi.astype(F32)).astype(BF16)
        _store_transposed_blocks(v_ref, vt_ref)

    q = q_ref[...]
    gate = _scores_t(kmean_hi[...], q) + _scores_t(kmean_lo[...], q)
    blk_id = lax.broadcasted_iota(jnp.int32, gate.shape, 0)
    q_blk = 2 * qi + lax.broadcasted_iota(jnp.int32, gate.shape, 1) // blk
    past = blk_id < q_blk
    work = jnp.where(past, gate, NEG_INF)
    chosen = jnp.zeros(gate.shape, F32)
    for _ in range(MOBA_TOPK):
        best = jnp.max(work, axis=0, keepdims=True)
        first = jnp.min(jnp.where(work == best, blk_id, nb), axis=0, keepdims=True)
        pick = blk_id == first
        chosen = jnp.where(pick, 1.0, chosen)
        work = jnp.where(pick, -jnp.inf, work)
    sel_ref[...] = jnp.where(past, chosen, 0.0)

    def picked(block):
        return sel_ref[pl.ds(block, 1), :] > 0.5

    def past_mask(s, j):
        return jnp.concatenate([jnp.where(picked(2 * j), s[:blk], NEG_INF),
                                jnp.where(picked(2 * j + 1), s[blk:], NEG_INF)], axis=0)

    def diag_mask(s):
        key = lax.broadcasted_iota(jnp.int32, (tile, tile), 0)
        qry = lax.broadcasted_iota(jnp.int32, (tile, tile), 1)
        other = jnp.where(key < blk, jnp.where(picked(2 * qi), s, NEG_INF), NEG_INF)
        own_start = (qry // blk) * blk
        return jnp.where(key <= qry, jnp.where(key >= own_start, s, other), other)

    [(acc, l)] = _flash_t([(q, k_ref, vt_ref, p_ref, acc_ref, st_ref)], qi, tile, diag_mask, past_mask)
    o_ref[...] = (acc * (1.0 / l)).T.astype(o_ref.dtype)


def _moba(qkv, batch, seq):
    blk, tile = MOBA_BLOCK, MOBA_TILE
    assert seq % tile == 0
    nq = seq // tile
    nb = seq // blk
    heads = N_HEADS_B
    vmem = (2 * 2 * seq * HEAD_DIM * 2 + seq * HEAD_DIM * 2 + 4 * tile * HEAD_DIM * 2 + 2 * tile * tile * 2
            + HEAD_DIM * tile * 4 + 10 * tile * tile * 4 + (8 << 20))
    return pl.pallas_call(
        _moba_body,
        grid=(batch, heads, nq),
        in_specs=[
            pl.BlockSpec((tile, HEAD_DIM), lambda b, h, i: (b * nq + i, h)),
            pl.BlockSpec((seq, HEAD_DIM), lambda b, h, i: (b, heads + h)),
            pl.BlockSpec((seq, HEAD_DIM), lambda b, h, i: (b, 2 * heads + h)),
        ],
        out_specs=pl.BlockSpec((tile, HEAD_DIM), lambda b, h, i: (b * nq + i, h)),
        out_shape=jax.ShapeDtypeStruct((batch * seq, WIDTH_B), BF16),
        scratch_shapes=[pltpu.VMEM((nb, HEAD_DIM), BF16)] * 2
        + [pltpu.VMEM((nq, HEAD_DIM + ONES_ROWS, tile), BF16), pltpu.VMEM((nb, tile), F32),
           pltpu.VMEM((2, tile, tile), BF16), pltpu.VMEM((HEAD_DIM + ONES_ROWS, tile), F32),
           pltpu.VMEM((2, 1, tile), F32)],
        compiler_params=_params(("parallel", "parallel", "arbitrary"), vmem),
        name="moba",
    )(qkv, qkv, qkv)


def _diff_body(lq1_ref, lk1_ref, lq2_ref, lk2_ref, g_ref, q1_ref, q2_ref, k1_ref, k2_ref, v_ref, o_ref,
               vt_ref, p1_ref, p2_ref, acc1_ref, acc2_ref, st1_ref, st2_ref, *, lambda_init):
    qi = pl.program_id(2)
    t = DIFF_BLOCK

    @pl.when(qi == 0)
    def _():
        _store_transposed_blocks(v_ref, vt_ref)

    def diag_mask(s):
        key = lax.broadcasted_iota(jnp.int32, (t, t), 0)
        qry = lax.broadcasted_iota(jnp.int32, (t, t), 1)
        return jnp.where(key <= qry, s, NEG_INF)

    streams = [(q1_ref[...], k1_ref, vt_ref, p1_ref, acc1_ref, st1_ref),
               (q2_ref[...], k2_ref, vt_ref, p2_ref, acc2_ref, st2_ref)]
    (a1, l1), (a2, l2) = _flash_t(streams, qi, t, diag_mask, lambda s, j: s)

    lam = (jnp.exp(jnp.sum(lq1_ref[...] * lk1_ref[...], axis=-1, keepdims=True))
           - jnp.exp(jnp.sum(lq2_ref[...] * lk2_ref[...], axis=-1, keepdims=True)) + lambda_init)
    o = a1 * (1.0 / l1) - lam * (a2 * (1.0 / l2))
    o = o * lax.rsqrt(jnp.mean(o * o, axis=0, keepdims=True) + SUBLN_EPS)
    o_ref[...] = (o.T * g_ref[...] * (1.0 - lambda_init)).astype(o_ref.dtype)


def _diff(qkv, lq1, lk1, lq2, lk2, subln_g, lambda_init, batch, seq):
    t = DIFF_BLOCK
    assert seq % t == 0
    nq = seq // t
    heads = N_HEADS_C
    dv = 2 * HEAD_DIM
    kcol = D_MODEL // HEAD_DIM
    vcol = 2 * D_MODEL // dv

    def vec(width):
        return pl.BlockSpec((1, width), lambda b, h, i: (0, 0))

    def q_spec(part):
        return pl.BlockSpec((t, HEAD_DIM), lambda b, h, i: (b * nq + i, 2 * h + part))

    def k_spec(part):
        return pl.BlockSpec((seq, HEAD_DIM), lambda b, h, i: (b, kcol + 2 * h + part))

    vmem = (2 * (2 * seq * HEAD_DIM + seq * dv) * 2 + seq * dv * 2 + 8 * t * HEAD_DIM * 2 + 2 * t * dv * 2
            + 2 * t * dv * 4 + 4 * t * t * 2 + 10 * t * t * 4 + (8 << 20))
    return pl.pallas_call(
        functools.partial(_diff_body, lambda_init=lambda_init),
        grid=(batch, heads, nq),
        in_specs=[vec(HEAD_DIM)] * 4 + [vec(dv), q_spec(0), q_spec(1), k_spec(0), k_spec(1),
                                        pl.BlockSpec((seq, dv), lambda b, h, i: (b, vcol + h))],
        out_specs=pl.BlockSpec((t, dv), lambda b, h, i: (b * nq + i, h)),
        out_shape=jax.ShapeDtypeStruct((batch * seq, heads * dv), BF16),
        scratch_shapes=[pltpu.VMEM((nq, dv + ONES_ROWS, t), BF16)] + [pltpu.VMEM((2, t, t), BF16)] * 2
        + [pltpu.VMEM((dv + ONES_ROWS, t), F32)] * 2 + [pltpu.VMEM((2, 1, t), F32)] * 2,
        compiler_params=_params(("parallel", "parallel", "arbitrary"), vmem),
        name="diffattn",
    )(lq1.reshape(1, -1), lk1.reshape(1, -1), lq2.reshape(1, -1), lk2.reshape(1, -1),
      subln_g.reshape(1, -1), qkv, qkv, qkv, qkv, qkv)


def _lambda_init(layer):
    return 0.8 - 0.6 * math.exp(-0.3 * layer)


def kernel(x, ffa_norm, ffa_w_in, ffa_w_out, mix_norm, even_w_in, even_w_out, odd_w_in, odd_w_out,
           lambda_q1, lambda_k1, lambda_q2, lambda_k2, subln_norm, ffb_norm, ffb_w_in, ffb_w_out, final_norm):
    batch, seq, d = x.shape
    depth = ffa_norm.shape[0]
    cos, sin = _rope_tables(seq)
    h = x.reshape(batch * seq, d)
    for layer in range(depth):
        i = layer // 2
        h = _ffn(h, ffa_norm[layer], ffa_w_in[layer].astype(BF16), ffa_w_out[layer].astype(BF16))
        if layer % 2 == 0:
            w_in = even_w_in[i].astype(BF16)
            wa, wb = 3 * WIDTH_A, 3 * WIDTH_B
            qkv_a = _proj(h, mix_norm[layer], w_in, cos, sin, 0, WIDTH_A, 2 * WIDTH_A, wa, SCALE, F32)
            qkv_b = _proj(h, mix_norm[layer], w_in, cos, sin, wa, WIDTH_B, 2 * WIDTH_B, wb, SCALE * LOG2_E, BF16)
            o_a = _dilated(qkv_a, batch, seq)
            o_b = _moba(qkv_b, batch, seq)
            h = _oproj(h, [o_a, o_b], even_w_out[i].astype(BF16))
        else:
            qkv = _proj(h, mix_norm[layer], odd_w_in[i].astype(BF16), cos, sin,
                        0, D_MODEL, 2 * D_MODEL, 3 * D_MODEL, SCALE * LOG2_E, BF16)
            o = _diff(qkv, lambda_q1[i], lambda_k1[i], lambda_q2[i], lambda_k2[i], subln_norm[i],
                      _lambda_init(layer), batch, seq)
            h = _oproj(h, [o], odd_w_out[i].astype(BF16))
        last = layer == depth - 1
        h = _ffn(h, ffb_norm[layer], ffb_w_in[layer].astype(BF16), ffb_w_out[layer].astype(BF16),
                 final_norm if last else None)
    return h.reshape(batch, seq, d)
```

```python
import functools
import math

import jax
import jax.numpy as jnp
from jax import lax
from jax.experimental import pallas as pl
from jax.experimental.pallas import tpu as pltpu

F32 = jnp.float32
BF16 = jnp.bfloat16

D_MODEL = 2048
HEAD_DIM = 128
HALF_DIM = HEAD_DIM // 2
N_HEADS_A = 8
N_HEADS_B = 8
N_HEADS_C = 8
WIDTH_A = N_HEADS_A * HEAD_DIM
WIDTH_B = N_HEADS_B * HEAD_DIM
DILATIONS = (1, 4, 16)
DIL_BLOCK = 128
DIL_REACH = 128
DIL_SUPER = DIL_BLOCK * DILATIONS[-1]
DIL_UNROLL = 16
MOBA_BLOCK = 256
MOBA_TOPK = 3
MOBA_TILE = 2 * MOBA_BLOCK
DIFF_BLOCK = 512
ROPE_THETA = 10000.0
NORM_EPS = 1e-6
SUBLN_EPS = 1e-5
NEG_INF = -1e30
SCALE = HEAD_DIM ** -0.5
LOG2_E = math.log2(math.e)

VMEM_V7X_BYTES = 64 * 1024 * 1024
NT_DIMS = (((1,), (1,)), ((), ()))
ONES_ROWS = 16


def _params(semantics, vmem_bytes):
    assert vmem_bytes < VMEM_V7X_BYTES
    return pltpu.CompilerParams(dimension_semantics=semantics, vmem_limit_bytes=int(vmem_bytes))


def _rms(x, g, eps):
    return x * lax.rsqrt(jnp.mean(x * x, axis=-1, keepdims=True) + eps) * g


FFN_ROWS = 1024
FFN_COLS = 512


def _ffn_body(x_ref, g_ref, wg_ref, wu_ref, wo_ref, *rest, final):
    if final:
        fg_ref, o_ref, xn_ref = rest
    else:
        o_ref, xn_ref = rest
    j = pl.program_id(1)

    @pl.when(j == 0)
    def _():
        x = x_ref[...]
        xn_ref[...] = _rms(x, g_ref[...], NORM_EPS).astype(BF16)
        o_ref[...] = x

    xn = xn_ref[...]
    gate = jnp.dot(xn, wg_ref[...].astype(BF16), preferred_element_type=F32)
    up = jnp.dot(xn, wu_ref[...].astype(BF16), preferred_element_type=F32)
    act = (0.5 * gate / (1.0 + jnp.exp(-gate))) * up
    o_ref[...] += jnp.dot(act.astype(BF16), wo_ref[...].astype(BF16), preferred_element_type=F32)

    if final:
        @pl.when(j == pl.num_programs(1) - 1)
        def _():
            o_ref[...] = _rms(o_ref[...], fg_ref[...], NORM_EPS)


def _ffn(h, g, w_in, w_out, layer, final_g=None):
    n, d = h.shape
    d_ff = w_out.shape[1]
    tm, tf = FFN_ROWS, FFN_COLS
    nff = d_ff // tf
    assert n % tm == 0 and d_ff % tf == 0
    final = final_g is not None
    in_specs = [
        pl.BlockSpec((tm, d), lambda i, j: (i, 0)),
        pl.BlockSpec((1, d), lambda i, j: (0, 0)),
        pl.BlockSpec((None, d, tf), lambda i, j: (layer, 0, j)),
        pl.BlockSpec((None, d, tf), lambda i, j: (layer, 0, nff + j)),
        pl.BlockSpec((None, tf, d), lambda i, j: (layer, j, 0)),
    ]
    args = [h, g.reshape(1, d), w_in, w_in, w_out]
    if final:
        in_specs.append(pl.BlockSpec((1, d), lambda i, j: (0, 0)))
        args.append(final_g.reshape(1, d))
    vmem = 3 * tm * d * 4 + tm * d * 2 + 2 * (2 * d * tf + tf * d) * 4 + (8 << 20)
    return pl.pallas_call(
        functools.partial(_ffn_body, final=final),
        grid=(n // tm, nff),
        in_specs=in_specs,
        out_specs=pl.BlockSpec((tm, d), lambda i, j: (i, 0), pipeline_mode=pl.Buffered(1)),
        out_shape=jax.ShapeDtypeStruct((n, d), F32),
        scratch_shapes=[pltpu.VMEM((tm, d), BF16)],
        compiler_params=_params(("parallel", "arbitrary"), vmem),
        name="ffn",
    )(*args)


PROJ_ROWS = 1024
PROJ_COLS = 1024


def _proj_body(x_ref, g_ref, w_ref, cos_ref, sin_ref, o_ref, xn_ref, *, q_tiles, q_scale):
    j = pl.program_id(1)

    @pl.when(j == 0)
    def _():
        xn_ref[...] = _rms(x_ref[...], g_ref[...], NORM_EPS).astype(BF16)

    y = jnp.dot(xn_ref[...], w_ref[...], preferred_element_type=F32)
    scale = jnp.where(j < q_tiles, q_scale, 1.0).astype(F32)
    cos = cos_ref[...] * scale
    sin = sin_ref[...] * scale
    for c in range(y.shape[1] // HEAD_DIM):
        yh = y[:, c * HEAD_DIM:(c + 1) * HEAD_DIM]
        rot = yh * cos + pltpu.roll(yh, HALF_DIM, 1) * sin
        o_ref[:, c * HEAD_DIM:(c + 1) * HEAD_DIM] = rot.astype(o_ref.dtype)


def _proj(h, g, w, cos, sin, col0, q_width, rope_width, width, q_scale, out_dtype):
    n, d = h.shape
    seq = cos.shape[1]
    tm, tn = PROJ_ROWS, PROJ_COLS
    assert n % tm == 0 and seq % tm == 0
    assert all(c % tn == 0 for c in (col0, q_width, rope_width, width))
    pos_blocks = seq // tm
    rope_tiles = rope_width // tn
    body = functools.partial(_proj_body, q_tiles=q_width // tn, q_scale=q_scale)

    def table_spec():
        return pl.BlockSpec((None, tm, HEAD_DIM),
                            lambda i, j: (jnp.where(j < rope_tiles, 0, 1), i % pos_blocks, 0))

    vmem = 2 * tm * d * 4 + tm * d * 2 + 2 * d * tn * 2 + 4 * tm * tn * 4 + 4 * tm * HEAD_DIM * 4 + (8 << 20)
    return pl.pallas_call(
        body,
        grid=(n // tm, width // tn),
        in_specs=[
            pl.BlockSpec((tm, d), lambda i, j: (i, 0)),
            pl.BlockSpec((1, d), lambda i, j: (0, 0)),
            pl.BlockSpec((d, tn), lambda i, j: (0, col0 // tn + j)),
            table_spec(),
            table_spec(),
        ],
        out_specs=pl.BlockSpec((tm, tn), lambda i, j: (i, j)),
        out_shape=jax.ShapeDtypeStruct((n, width), out_dtype),
        scratch_shapes=[pltpu.VMEM((tm, d), BF16)],
        compiler_params=_params(("parallel", "arbitrary"), vmem),
        name="proj",
    )(h, g.reshape(1, d), w, cos, sin)


def _rope_tables(seq):
    inv_freq = ROPE_THETA ** (-jnp.arange(HALF_DIM, dtype=F32) / HALF_DIM)
    ang = jnp.arange(seq, dtype=F32)[:, None] * inv_freq[None, :]
    cos, sin = jnp.cos(ang), jnp.sin(ang)
    cos = jnp.concatenate([cos, cos], axis=-1)
    sin = jnp.concatenate([-sin, sin], axis=-1)
    return jnp.stack([cos, jnp.ones_like(cos)]), jnp.stack([sin, jnp.zeros_like(sin)])


OPROJ_ROWS = 1024
OPROJ_COLS = 1024


def _oproj_body(*refs):
    *aw, h_ref, o_ref = refs
    acc = h_ref[...]
    for a_ref, w_ref in zip(aw[0::2], aw[1::2]):
        acc = acc + jnp.dot(a_ref[...], w_ref[...], preferred_element_type=F32)
    o_ref[...] = acc


def _oproj(h, parts, w):
    n, d = h.shape
    tm, tn = OPROJ_ROWS, OPROJ_COLS
    assert n % tm == 0 and d % tn == 0
    in_specs, args, vmem, row0 = [], [], 0, 0
    for a in parts:
        kdim = a.shape[1]
        assert row0 % kdim == 0
        in_specs += [pl.BlockSpec((tm, kdim), lambda i, j: (i, 0)),
                     pl.BlockSpec((kdim, tn), lambda i, j, r=row0 // kdim: (r, j))]
        args += [a, w]
        vmem += 2 * (tm * kdim + kdim * tn) * 2
        row0 += kdim
    assert row0 == w.shape[0]
    in_specs.append(pl.BlockSpec((tm, tn), lambda i, j: (i, j)))
    args.append(h)
    vmem += 6 * tm * tn * 4 + (8 << 20)
    return pl.pallas_call(
        _oproj_body,
        grid=(n // tm, d // tn),
        in_specs=in_specs,
        out_specs=pl.BlockSpec((tm, tn), lambda i, j: (i, j)),
        out_shape=jax.ShapeDtypeStruct((n, d), F32),
        compiler_params=_params(("parallel", "parallel"), vmem),
        name="oproj",
    )(*args)


def _dilated_body(q_ref, kc_ref, kp_ref, vc_ref, vp_ref, o_ref, kbuf, vbuf, *branch_bufs):
    obufs, lbufs = branch_bufs[:3], branch_bufs[3:]
    sb = pl.program_id(2)
    sup = DIL_SUPER
    kbuf[0:sup, :] = kp_ref[...]
    kbuf[sup:2 * sup, :] = kc_ref[...]
    vbuf[0:sup, :] = vp_ref[...]
    vbuf[sup:2 * sup, :] = vc_ref[...]

    row = lax.broadcasted_iota(jnp.int32, (DIL_BLOCK, 2 * DIL_BLOCK), 0)
    col = lax.broadcasted_iota(jnp.int32, (DIL_BLOCK, 2 * DIL_BLOCK), 1)
    band = jnp.logical_and(col >= row + (DIL_BLOCK - DIL_REACH), col <= row + DIL_BLOCK)

    for br, dil in enumerate(DILATIONS):
        nsub = sup // (DIL_BLOCK * dil)
        obuf, lbuf = obufs[br], lbufs[br]

        def block(t, carry, dil=dil, nsub=nsub, obuf=obuf, lbuf=lbuf):
            res = t // nsub
            sub = t % nsub
            q0 = res + dil * DIL_BLOCK * sub
            k0 = sup + q0 - dil * DIL_BLOCK
            if dil == 1:
                q0 = pl.multiple_of(q0, DIL_BLOCK)
                k0 = pl.multiple_of(k0, DIL_BLOCK)
                qs, ks = pl.ds(q0, DIL_BLOCK), pl.ds(k0, 2 * DIL_BLOCK)
            else:
                qs, ks = pl.ds(q0, DIL_BLOCK, stride=dil), pl.ds(k0, 2 * DIL_BLOCK, stride=dil)
            qb = q_ref[qs, :].astype(BF16)
            kb = kbuf[ks, :].astype(BF16)
            vb = vbuf[ks, :].astype(BF16)
            s = lax.dot_general(qb, kb, NT_DIMS, preferred_element_type=F32)
            has_prev = jnp.logical_or(sb > 0, sub > 0)
            first_col = jnp.where(has_prev, 0, DIL_BLOCK)
            s = jnp.where(jnp.logical_and(band, col >= first_col), s, NEG_INF)
            m = jnp.max(s, axis=-1, keepdims=True)
            p = jnp.exp(s - m)
            den = jnp.sum(p, axis=-1, keepdims=True)
            o = jnp.dot(p.astype(BF16), vb, preferred_element_type=F32) / den
            obuf[qs, :] = o
            lbuf[qs, :] = jnp.broadcast_to(m + jnp.log(den), (DIL_BLOCK, HEAD_DIM))
            return carry

        lax.fori_loop(0, nsub * dil, block, 0, unroll=DIL_UNROLL)

    lses = [lbuf[...] for lbuf in lbufs]
    top = jnp.maximum(jnp.maximum(lses[0], lses[1]), lses[2])
    wts = [jnp.exp(l - top) for l in lses]
    mix = wts[0] * obufs[0][...] + wts[1] * obufs[1][...] + wts[2] * obufs[2][...]
    o_ref[...] = (mix / (wts[0] + wts[1] + wts[2])).astype(o_ref.dtype)


def _dilated(qkv, batch, seq):
    sup = DIL_SUPER
    assert seq % sup == 0
    nsb = seq // sup
    heads = N_HEADS_A

    def cur(col0):
        return pl.BlockSpec((sup, HEAD_DIM), lambda b, h, s: (b * nsb + s, col0 + h))

    def prev(col0):
        return pl.BlockSpec((sup, HEAD_DIM), lambda b, h, s: (b * nsb + jnp.maximum(s - 1, 0), col0 + h))

    blk = sup * HEAD_DIM * 4
    vmem = 2 * 5 * blk + 2 * sup * HEAD_DIM * 2 + 4 * blk + 6 * blk + 8 * blk + (8 << 20)
    return pl.pallas_call(
        _dilated_body,
        grid=(batch, heads, nsb),
        in_specs=[cur(0), cur(heads), prev(heads), cur(2 * heads), prev(2 * heads)],
        out_specs=pl.BlockSpec((sup, HEAD_DIM), lambda b, h, s: (b * nsb + s, h)),
        out_shape=jax.ShapeDtypeStruct((batch * seq, WIDTH_A), BF16),
        scratch_shapes=[pltpu.VMEM((2 * sup, HEAD_DIM), F32)] * 2 + [pltpu.VMEM((sup, HEAD_DIM), F32)] * 6,
        compiler_params=_params(("parallel", "parallel", "arbitrary"), vmem),
        name="dilated",
    )(qkv, qkv, qkv, qkv, qkv)


def _scores_t(k_blk, q):
    return lax.dot_general(k_blk, q, NT_DIMS, preferred_element_type=F32)


def _flash_t(streams, qi, tile, diag_mask, past_mask):
    own = pl.multiple_of(qi * tile, tile)
    for q, k_ref, _, p_ref, acc_ref, st_ref in streams:
        s = diag_mask(_scores_t(k_ref[pl.ds(own, tile), :], q)).astype(BF16)
        m = jnp.max(s, axis=0, keepdims=True)
        p_ref[0] = jnp.exp2(s - m)
        m = m.astype(F32)
        acc_ref[...] = jnp.zeros(acc_ref.shape, F32)
        st_ref[0] = m
        st_ref[1] = jnp.ones_like(m)

    def flush(pending, slot):
        for _, _, vt_ref, p_ref, acc_ref, st_ref in streams:
            acc_ref[...] = st_ref[1] * acc_ref[...] + jnp.dot(vt_ref[pending], p_ref[slot],
                                                              preferred_element_type=F32)

    def step(j, pending, rd, wr):
        start = pl.multiple_of(j * tile, tile)
        scores = [_scores_t(st[1][pl.ds(start, tile), :], st[0]) for st in streams]
        flush(pending, rd)
        for s, (_, _, _, p_ref, _, st_ref) in zip(scores, streams):
            s = past_mask(s, j).astype(BF16)
            m = st_ref[0]
            m_new = jnp.maximum(m, jnp.max(s, axis=0, keepdims=True).astype(F32))
            alpha = jnp.exp2(m - m_new)
            p_ref[wr] = jnp.exp2(s - m_new.astype(BF16))
            st_ref[0] = m_new
            st_ref[1] = alpha

    def pair(i, carry):
        step(2 * i, jnp.where(i == 0, qi, 2 * i - 1), 0, 1)
        step(2 * i + 1, 2 * i, 1, 0)
        return carry

    npairs = qi // 2
    lax.fori_loop(0, npairs, pair, 0)
    odd = qi % 2 == 1
    last = jnp.maximum(qi - 1, 0)

    @pl.when(odd)
    def _():
        step(last, jnp.where(qi == 1, qi, qi - 2), 0, 1)
        flush(last, 1)

    @pl.when(jnp.logical_not(odd))
    def _():
        flush(last, 0)

    outs = []
    for _, _, _, _, acc_ref, _ in streams:
        dv = acc_ref.shape[0] - ONES_ROWS
        outs.append((acc_ref[0:dv, :], acc_ref[dv:dv + 1, :]))
    return outs


def _store_transposed_blocks(v_ref, vt_ref):
    nblk, rows_t, rows = vt_ref.shape
    dv = rows_t - ONES_ROWS
    for n in range(nblk):
        vt_ref[n, 0:dv, :] = v_ref[n * rows:(n + 1) * rows, :].astype(F32).T.astype(vt_ref.dtype)
        vt_ref[n, dv:rows_t, :] = jnp.ones((ONES_ROWS, rows), vt_ref.dtype)


def _moba_body(q_ref, k_ref, v_ref, o_ref, kmean_hi, kmean_lo, vt_ref, sel_ref, p_ref, acc_ref, st_ref):
    qi = pl.program_id(2)
    blk, tile = MOBA_BLOCK, MOBA_TILE
    nb = k_ref.shape[0] // blk

    @pl.when(qi == 0)
    def _():
        for n in range(nb):
            mean = jnp.mean(k_ref[n * blk:(n + 1) * blk, :].astype(F32), axis=0, keepdims=True)
            hi = mean.astype(BF16)
            kmean_hi[n:n + 1, :] = hi
            kmean_lo[n:n + 1, :] = (mean - hi.astype(F32)).astype(BF16)
        _store_transposed_blocks(v_ref, vt_ref)

    q = q_ref[...]
    gate = _scores_t(kmean_hi[...], q) + _scores_t(kmean_lo[...], q)
    blk_id = lax.broadcasted_iota(jnp.int32, gate.shape, 0)
    q_blk = 2 * qi + lax.broadcasted_iota(jnp.int32, gate.shape, 1) // blk
    past = blk_id < q_blk
    work = jnp.where(past, gate, NEG_INF)
    chosen = jnp.zeros(gate.shape, F32)
    for _ in range(MOBA_TOPK):
        best = jnp.max(work, axis=0, keepdims=True)
        first = jnp.min(jnp.where(work == best, blk_id, nb), axis=0, keepdims=True)
        pick = blk_id == first
        chosen = jnp.where(pick, 1.0, chosen)
        work = jnp.where(pick, -jnp.inf, work)
    sel_ref[...] = jnp.where(past, chosen, 0.0)

    def picked(block):
        return sel_ref[pl.ds(block, 1), :] > 0.5

    def past_mask(s, j):
        return jnp.concatenate([jnp.where(picked(2 * j), s[:blk], NEG_INF),
                                jnp.where(picked(2 * j + 1), s[blk:], NEG_INF)], axis=0)

    def diag_mask(s):
        key = lax.broadcasted_iota(jnp.int32, (tile, tile), 0)
        qry = lax.broadcasted_iota(jnp.int32, (tile, tile), 1)
        other = jnp.where(key < blk, jnp.where(picked(2 * qi), s, NEG_INF), NEG_INF)
        own_start = (qry // blk) * blk
        return jnp.where(key <= qry, jnp.where(key >= own_start, s, other), other)

    [(acc, l)] = _flash_t([(q, k_ref, vt_ref, p_ref, acc_ref, st_ref)], qi, tile, diag_mask, past_mask)
    o_ref[...] = (acc * (1.0 / l)).T.astype(o_ref.dtype)


def _moba(qkv, batch, seq):
    blk, tile = MOBA_BLOCK, MOBA_TILE
    assert seq % tile == 0
    nq = seq // tile
    nb = seq // blk
    heads = N_HEADS_B
    vmem = (2 * 2 * seq * HEAD_DIM * 2 + seq * HEAD_DIM * 2 + 4 * tile * HEAD_DIM * 2 + 2 * tile * tile * 2
            + HEAD_DIM * tile * 4 + 10 * tile * tile * 4 + (8 << 20))
    return pl.pallas_call(
        _moba_body,
        grid=(batch, heads, nq),
        in_specs=[
            pl.BlockSpec((tile, HEAD_DIM), lambda b, h, i: (b * nq + i, h)),
            pl.BlockSpec((seq, HEAD_DIM), lambda b, h, i: (b, heads + h)),
            pl.BlockSpec((seq, HEAD_DIM), lambda b, h, i: (b, 2 * heads + h)),
        ],
        out_specs=pl.BlockSpec((tile, HEAD_DIM), lambda b, h, i: (b * nq + i, h)),
        out_shape=jax.ShapeDtypeStruct((batch * seq, WIDTH_B), BF16),
        scratch_shapes=[pltpu.VMEM((nb, HEAD_DIM), BF16)] * 2
        + [pltpu.VMEM((nq, HEAD_DIM + ONES_ROWS, tile), BF16), pltpu.VMEM((nb, tile), F32),
           pltpu.VMEM((2, tile, tile), BF16), pltpu.VMEM((HEAD_DIM + ONES_ROWS, tile), F32),
           pltpu.VMEM((2, 1, tile), F32)],
        compiler_params=_params(("parallel", "parallel", "arbitrary"), vmem),
        name="moba",
    )(qkv, qkv, qkv)


def _diff_body(lq1_ref, lk1_ref, lq2_ref, lk2_ref, g_ref, q1_ref, q2_ref, k1_ref, k2_ref, v_ref, o_ref,
               vt_ref, p1_ref, p2_ref, acc1_ref, acc2_ref, st1_ref, st2_ref, *, lambda_init):
    qi = pl.program_id(2)
    t = DIFF_BLOCK

    @pl.when(qi == 0)
    def _():
        _store_transposed_blocks(v_ref, vt_ref)

    def diag_mask(s):
        key = lax.broadcasted_iota(jnp.int32, (t, t), 0)
        qry = lax.broadcasted_iota(jnp.int32, (t, t), 1)
        return jnp.where(key <= qry, s, NEG_INF)

    streams = [(q1_ref[...], k1_ref, vt_ref, p1_ref, acc1_ref, st1_ref),
               (q2_ref[...], k2_ref, vt_ref, p2_ref, acc2_ref, st2_ref)]
    (a1, l1), (a2, l2) = _flash_t(streams, qi, t, diag_mask, lambda s, j: s)

    lam = (jnp.exp(jnp.sum(lq1_ref[...] * lk1_ref[...], axis=-1, keepdims=True))
           - jnp.exp(jnp.sum(lq2_ref[...] * lk2_ref[...], axis=-1, keepdims=True)) + lambda_init)
    o = a1 * (1.0 / l1) - lam * (a2 * (1.0 / l2))
    o = o * lax.rsqrt(jnp.mean(o * o, axis=0, keepdims=True) + SUBLN_EPS)
    o_ref[...] = (o.T * g_ref[...] * (1.0 - lambda_init)).astype(o_ref.dtype)


def _diff(qkv, lq1, lk1, lq2, lk2, subln_g, lambda_init, batch, seq):
    t = DIFF_BLOCK
    assert seq % t == 0
    nq = seq // t
    heads = N_HEADS_C
    dv = 2 * HEAD_DIM
    kcol = D_MODEL // HEAD_DIM
    vcol = 2 * D_MODEL // dv

    def vec(width):
        return pl.BlockSpec((1, width), lambda b, h, i: (0, 0))

    def q_spec(part):
        return pl.BlockSpec((t, HEAD_DIM), lambda b, h, i: (b * nq + i, 2 * h + part))

    def k_spec(part):
        return pl.BlockSpec((seq, HEAD_DIM), lambda b, h, i: (b, kcol + 2 * h + part))

    vmem = (2 * (2 * seq * HEAD_DIM + seq * dv) * 2 + seq * dv * 2 + 8 * t * HEAD_DIM * 2 + 2 * t * dv * 2
            + 2 * t * dv * 4 + 4 * t * t * 2 + 10 * t * t * 4 + (8 << 20))
    return pl.pallas_call(
        functools.partial(_diff_body, lambda_init=lambda_init),
        grid=(batch, heads, nq),
        in_specs=[vec(HEAD_DIM)] * 4 + [vec(dv), q_spec(0), q_spec(1), k_spec(0), k_spec(1),
                                        pl.BlockSpec((seq, dv), lambda b, h, i: (b, vcol + h))],
        out_specs=pl.BlockSpec((t, dv), lambda b, h, i: (b * nq + i, h)),
        out_shape=jax.ShapeDtypeStruct((batch * seq, heads * dv), BF16),
        scratch_shapes=[pltpu.VMEM((nq, dv + ONES_ROWS, t), BF16)] + [pltpu.VMEM((2, t, t), BF16)] * 2
        + [pltpu.VMEM((dv + ONES_ROWS, t), F32)] * 2 + [pltpu.VMEM((2, 1, t), F32)] * 2,
        compiler_params=_params(("parallel", "parallel", "arbitrary"), vmem),
        name="diffattn",
    )(lq1.reshape(1, -1), lk1.reshape(1, -1), lq2.reshape(1, -1), lk2.reshape(1, -1),
      subln_g.reshape(1, -1), qkv, qkv, qkv, qkv, qkv)


def _lambda_init(layer):
    return 0.8 - 0.6 * math.exp(-0.3 * layer)


def kernel(x, ffa_norm, ffa_w_in, ffa_w_out, mix_norm, even_w_in, even_w_out, odd_w_in, odd_w_out,
           lambda_q1, lambda_k1, lambda_q2, lambda_k2, subln_norm, ffb_norm, ffb_w_in, ffb_w_out, final_norm):
    batch, seq, d = x.shape
    depth = ffa_norm.shape[0]
    cos, sin = _rope_tables(seq)
    h = x.reshape(batch * seq, d)
    for layer in range(depth):
        i = layer // 2
        h = _ffn(h, ffa_norm[layer], ffa_w_in, ffa_w_out, layer)
        if layer % 2 == 0:
            w_in = even_w_in[i].astype(BF16)
            wa, wb = 3 * WIDTH_A, 3 * WIDTH_B
            qkv_a = _proj(h, mix_norm[layer], w_in, cos, sin, 0, WIDTH_A, 2 * WIDTH_A, wa, SCALE, F32)
            qkv_b = _proj(h, mix_norm[layer], w_in, cos, sin, wa, WIDTH_B, 2 * WIDTH_B, wb, SCALE * LOG2_E, BF16)
            o_a = _dilated(qkv_a, batch, seq)
            o_b = _moba(qkv_b, batch, seq)
            h = _oproj(h, [o_a, o_b], even_w_out[i].astype(BF16))
        else:
            qkv = _proj(h, mix_norm[layer], odd_w_in[i].astype(BF16), cos, sin,
                        0, D_MODEL, 2 * D_MODEL, 3 * D_MODEL, SCALE * LOG2_E, BF16)
            o = _diff(qkv, lambda_q1[i], lambda_k1[i], lambda_q2[i], lambda_k2[i], subln_norm[i],
                      _lambda_init(layer), batch, seq)
            h = _oproj(h, [o], odd_w_out[i].astype(BF16))
        last = layer == depth - 1
        h = _ffn(h, ffb_norm[layer], ffb_w_in, ffb_w_out, layer, final_norm if last else None)
    return h.reshape(batch, seq, d)
```

```python
import functools
import math

import jax
import jax.numpy as jnp
from jax import lax
from jax.experimental import pallas as pl
from jax.experimental.pallas import tpu as pltpu

F32 = jnp.float32
BF16 = jnp.bfloat16

D_MODEL = 2048
HEAD_DIM = 128
HALF_DIM = HEAD_DIM // 2
N_HEADS_A = 8
N_HEADS_B = 8
N_HEADS_C = 8
WIDTH_A = N_HEADS_A * HEAD_DIM
WIDTH_B = N_HEADS_B * HEAD_DIM
DILATIONS = (1, 4, 16)
DIL_BLOCK = 128
DIL_REACH = 128
DIL_SUPER = DIL_BLOCK * DILATIONS[-1]
DIL_UNROLL = 16
MOBA_BLOCK = 256
MOBA_TOPK = 3
MOBA_TILE = 2 * MOBA_BLOCK
DIFF_BLOCK = 512
ROPE_THETA = 10000.0
NORM_EPS = 1e-6
SUBLN_EPS = 1e-5
NEG_INF = -1e30
SCALE = HEAD_DIM ** -0.5
LOG2_E = math.log2(math.e)

VMEM_V7X_BYTES = 64 * 1024 * 1024
NT_DIMS = (((1,), (1,)), ((), ()))
ONES_ROWS = 16


def _params(semantics, vmem_bytes):
    assert vmem_bytes < VMEM_V7X_BYTES
    return pltpu.CompilerParams(dimension_semantics=semantics, vmem_limit_bytes=int(vmem_bytes))


def _rms(x, g, eps):
    return x * lax.rsqrt(jnp.mean(x * x, axis=-1, keepdims=True) + eps) * g


FFN_ROWS = 1024
FFN_COLS = 512


def _ffn_body(h_hbm, g_ref, wg_ref, wu_ref, wo_ref, *rest, final):
    if final:
        fg_ref, o_ref, xn_ref, x_buf, x_sem = rest
    else:
        o_ref, xn_ref, x_buf, x_sem = rest
    i, j = pl.program_id(0), pl.program_id(1)
    rows = x_buf.shape[0]

    def x_copy(tile):
        return pltpu.make_async_copy(h_hbm.at[pl.ds(tile * rows, rows), :], x_buf, x_sem)

    @pl.when(j == 0)
    def _():
        @pl.when(i == 0)
        def _():
            x_copy(i).start()

        x_copy(i).wait()
        x = x_buf[...]
        xn_ref[...] = _rms(x, g_ref[...], NORM_EPS).astype(BF16)
        o_ref[...] = x

    @pl.when(jnp.logical_and(j == 1, i + 1 < pl.num_programs(0)))
    def _():
        x_copy(i + 1).start()

    xn = xn_ref[...]
    gate = jnp.dot(xn, wg_ref[...].astype(BF16), preferred_element_type=F32)
    up = jnp.dot(xn, wu_ref[...].astype(BF16), preferred_element_type=F32)
    act = (0.5 * gate / (1.0 + jnp.exp(-gate))) * up
    o_ref[...] += jnp.dot(act.astype(BF16), wo_ref[...].astype(BF16), preferred_element_type=F32)

    if final:
        @pl.when(j == pl.num_programs(1) - 1)
        def _():
            o_ref[...] = _rms(o_ref[...], fg_ref[...], NORM_EPS)


def _ffn(h, g, w_in, w_out, layer, final_g=None):
    n, d = h.shape
    d_ff = w_out.shape[1]
    tm, tf = FFN_ROWS, FFN_COLS
    nff = d_ff // tf
    assert n % tm == 0 and d_ff % tf == 0 and nff >= 2
    final = final_g is not None
    in_specs = [
        pl.BlockSpec(memory_space=pl.ANY),
        pl.BlockSpec((1, d), lambda i, j: (0, 0)),
        pl.BlockSpec((None, d, tf), lambda i, j: (layer, 0, j)),
        pl.BlockSpec((None, d, tf), lambda i, j: (layer, 0, nff + j)),
        pl.BlockSpec((None, tf, d), lambda i, j: (layer, j, 0)),
    ]
    args = [h, g.reshape(1, d), w_in, w_in, w_out]
    if final:
        in_specs.append(pl.BlockSpec((1, d), lambda i, j: (0, 0)))
        args.append(final_g.reshape(1, d))
    vmem = 3 * tm * d * 4 + tm * d * 2 + 2 * (2 * d * tf + tf * d) * 4 + (8 << 20)
    return pl.pallas_call(
        functools.partial(_ffn_body, final=final),
        grid=(n // tm, nff),
        in_specs=in_specs,
        out_specs=pl.BlockSpec((tm, d), lambda i, j: (i, 0)),
        out_shape=jax.ShapeDtypeStruct((n, d), F32),
        scratch_shapes=[pltpu.VMEM((tm, d), BF16), pltpu.VMEM((tm, d), F32), pltpu.SemaphoreType.DMA(())],
        compiler_params=_params(("arbitrary", "arbitrary"), vmem),
        name="ffn",
    )(*args)


PROJ_ROWS = 1024
PROJ_COLS = 1024


def _proj_body(x_ref, g_ref, w_ref, cos_ref, sin_ref, o_ref, xn_ref, *, q_tiles, q_scale):
    j = pl.program_id(1)

    @pl.when(j == 0)
    def _():
        xn_ref[...] = _rms(x_ref[...], g_ref[...], NORM_EPS).astype(BF16)

    y = jnp.dot(xn_ref[...], w_ref[...], preferred_element_type=F32)
    scale = jnp.where(j < q_tiles, q_scale, 1.0).astype(F32)
    cos = cos_ref[...] * scale
    sin = sin_ref[...] * scale
    for c in range(y.shape[1] // HEAD_DIM):
        yh = y[:, c * HEAD_DIM:(c + 1) * HEAD_DIM]
        rot = yh * cos + pltpu.roll(yh, HALF_DIM, 1) * sin
        o_ref[:, c * HEAD_DIM:(c + 1) * HEAD_DIM] = rot.astype(o_ref.dtype)


def _proj(h, g, w, cos, sin, col0, q_width, rope_width, width, q_scale, out_dtype):
    n, d = h.shape
    seq = cos.shape[1]
    tm, tn = PROJ_ROWS, PROJ_COLS
    assert n % tm == 0 and seq % tm == 0
    assert all(c % tn == 0 for c in (col0, q_width, rope_width, width))
    pos_blocks = seq // tm
    rope_tiles = rope_width // tn
    body = functools.partial(_proj_body, q_tiles=q_width // tn, q_scale=q_scale)

    def table_spec():
        return pl.BlockSpec((None, tm, HEAD_DIM),
                            lambda i, j: (jnp.where(j < rope_tiles, 0, 1), i % pos_blocks, 0))

    vmem = 2 * tm * d * 4 + tm * d * 2 + 2 * d * tn * 2 + 4 * tm * tn * 4 + 4 * tm * HEAD_DIM * 4 + (8 << 20)
    return pl.pallas_call(
        body,
        grid=(n // tm, width // tn),
        in_specs=[
            pl.BlockSpec((tm, d), lambda i, j: (i, 0)),
            pl.BlockSpec((1, d), lambda i, j: (0, 0)),
            pl.BlockSpec((d, tn), lambda i, j: (0, col0 // tn + j)),
            table_spec(),
            table_spec(),
        ],
        out_specs=pl.BlockSpec((tm, tn), lambda i, j: (i, j)),
        out_shape=jax.ShapeDtypeStruct((n, width), out_dtype),
        scratch_shapes=[pltpu.VMEM((tm, d), BF16)],
        compiler_params=_params(("parallel", "arbitrary"), vmem),
        name="proj",
    )(h, g.reshape(1, d), w, cos, sin)


def _rope_tables(seq):
    inv_freq = ROPE_THETA ** (-jnp.arange(HALF_DIM, dtype=F32) / HALF_DIM)
    ang = jnp.arange(seq, dtype=F32)[:, None] * inv_freq[None, :]
    cos, sin = jnp.cos(ang), jnp.sin(ang)
    cos = jnp.concatenate([cos, cos], axis=-1)
    sin = jnp.concatenate([-sin, sin], axis=-1)
    return jnp.stack([cos, jnp.ones_like(cos)]), jnp.stack([sin, jnp.zeros_like(sin)])


OPROJ_ROWS = 1024
OPROJ_COLS = 1024


def _oproj_body(*refs):
    *aw, h_ref, o_ref = refs
    acc = h_ref[...]
    for a_ref, w_ref in zip(aw[0::2], aw[1::2]):
        acc = acc + jnp.dot(a_ref[...], w_ref[...], preferred_element_type=F32)
    o_ref[...] = acc


def _oproj(h, parts, w):
    n, d = h.shape
    tm, tn = OPROJ_ROWS, OPROJ_COLS
    assert n % tm == 0 and d % tn == 0
    in_specs, args, vmem, row0 = [], [], 0, 0
    for a in parts:
        kdim = a.shape[1]
        assert row0 % kdim == 0
        in_specs += [pl.BlockSpec((tm, kdim), lambda i, j: (i, 0)),
                     pl.BlockSpec((kdim, tn), lambda i, j, r=row0 // kdim: (r, j))]
        args += [a, w]
        vmem += 2 * (tm * kdim + kdim * tn) * 2
        row0 += kdim
    assert row0 == w.shape[0]
    in_specs.append(pl.BlockSpec((tm, tn), lambda i, j: (i, j)))
    args.append(h)
    vmem += 6 * tm * tn * 4 + (8 << 20)
    return pl.pallas_call(
        _oproj_body,
        grid=(n // tm, d // tn),
        in_specs=in_specs,
        out_specs=pl.BlockSpec((tm, tn), lambda i, j: (i, j)),
        out_shape=jax.ShapeDtypeStruct((n, d), F32),
        compiler_params=_params(("parallel", "parallel"), vmem),
        name="oproj",
    )(*args)


def _dilated_body(q_ref, kc_ref, kp_ref, vc_ref, vp_ref, o_ref, kbuf, vbuf, *branch_bufs):
    obufs, lbufs = branch_bufs[:3], branch_bufs[3:]
    sb = pl.program_id(2)
    sup = DIL_SUPER
    kbuf[0:sup, :] = kp_ref[...]
    kbuf[sup:2 * sup, :] = kc_ref[...]
    vbuf[0:sup, :] = vp_ref[...]
    vbuf[sup:2 * sup, :] = vc_ref[...]

    row = lax.broadcasted_iota(jnp.int32, (DIL_BLOCK, 2 * DIL_BLOCK), 0)
    col = lax.broadcasted_iota(jnp.int32, (DIL_BLOCK, 2 * DIL_BLOCK), 1)
    band = jnp.logical_and(col >= row + (DIL_BLOCK - DIL_REACH), col <= row + DIL_BLOCK)

    for br, dil in enumerate(DILATIONS):
        nsub = sup // (DIL_BLOCK * dil)
        obuf, lbuf = obufs[br], lbufs[br]

        def block(t, carry, dil=dil, nsub=nsub, obuf=obuf, lbuf=lbuf):
            res = t // nsub
            sub = t % nsub
            q0 = res + dil * DIL_BLOCK * sub
            k0 = sup + q0 - dil * DIL_BLOCK
            if dil == 1:
                q0 = pl.multiple_of(q0, DIL_BLOCK)
                k0 = pl.multiple_of(k0, DIL_BLOCK)
                qs, ks = pl.ds(q0, DIL_BLOCK), pl.ds(k0, 2 * DIL_BLOCK)
            else:
                qs, ks = pl.ds(q0, DIL_BLOCK, stride=dil), pl.ds(k0, 2 * DIL_BLOCK, stride=dil)
            qb = q_ref[qs, :].astype(BF16)
            kb = kbuf[ks, :].astype(BF16)
            vb = vbuf[ks, :].astype(BF16)
            s = lax.dot_general(qb, kb, NT_DIMS, preferred_element_type=F32)
            has_prev = jnp.logical_or(sb > 0, sub > 0)
            first_col = jnp.where(has_prev, 0, DIL_BLOCK)
            s = jnp.where(jnp.logical_and(band, col >= first_col), s, NEG_INF)
            m = jnp.max(s, axis=-1, keepdims=True)
            p = jnp.exp(s - m)
            den = jnp.sum(p, axis=-1, keepdims=True)
            o = jnp.dot(p.astype(BF16), vb, preferred_element_type=F32) / den
            obuf[qs, :] = o
            lbuf[qs, :] = jnp.broadcast_to(m + jnp.log(den), (DIL_BLOCK, HEAD_DIM))
            return carry

        lax.fori_loop(0, nsub * dil, block, 0, unroll=DIL_UNROLL)

    lses = [lbuf[...] for lbuf in lbufs]
    top = jnp.maximum(jnp.maximum(lses[0], lses[1]), lses[2])
    wts = [jnp.exp(l - top) for l in lses]
    mix = wts[0] * obufs[0][...] + wts[1] * obufs[1][...] + wts[2] * obufs[2][...]
    o_ref[...] = (mix / (wts[0] + wts[1] + wts[2])).astype(o_ref.dtype)


def _dilated(qkv, batch, seq):
    sup = DIL_SUPER
    assert seq % sup == 0
    nsb = seq // sup
    heads = N_HEADS_A

    def cur(col0):
        return pl.BlockSpec((sup, HEAD_DIM), lambda b, h, s: (b * nsb + s, col0 + h))

    def prev(col0):
        return pl.BlockSpec((sup, HEAD_DIM), lambda b, h, s: (b * nsb + jnp.maximum(s - 1, 0), col0 + h))

    blk = sup * HEAD_DIM * 4
    vmem = 2 * 5 * blk + 2 * sup * HEAD_DIM * 2 + 4 * blk + 6 * blk + 8 * blk + (8 << 20)
    return pl.pallas_call(
        _dilated_body,
        grid=(batch, heads, nsb),
        in_specs=[cur(0), cur(heads), prev(heads), cur(2 * heads), prev(2 * heads)],
        out_specs=pl.BlockSpec((sup, HEAD_DIM), lambda b, h, s: (b * nsb + s, h)),
        out_shape=jax.ShapeDtypeStruct((batch * seq, WIDTH_A), BF16),
        scratch_shapes=[pltpu.VMEM((2 * sup, HEAD_DIM), F32)] * 2 + [pltpu.VMEM((sup, HEAD_DIM), F32)] * 6,
        compiler_params=_params(("parallel", "parallel", "arbitrary"), vmem),
        name="dilated",
    )(qkv, qkv, qkv, qkv, qkv)


def _scores_t(k_blk, q):
    return lax.dot_general(k_blk, q, NT_DIMS, preferred_element_type=F32)


def _flash_t(streams, qi, tile, diag_mask, past_mask):
    own = pl.multiple_of(qi * tile, tile)
    for q, k_ref, _, p_ref, acc_ref, st_ref in streams:
        s = diag_mask(_scores_t(k_ref[pl.ds(own, tile), :], q)).astype(BF16)
        m = jnp.max(s, axis=0, keepdims=True)
        p_ref[0] = jnp.exp2(s - m)
        m = m.astype(F32)
        acc_ref[...] = jnp.zeros(acc_ref.shape, F32)
        st_ref[0] = m
        st_ref[1] = jnp.ones_like(m)

    def flush(pending, slot):
        for _, _, vt_ref, p_ref, acc_ref, st_ref in streams:
            acc_ref[...] = st_ref[1] * acc_ref[...] + jnp.dot(vt_ref[pending], p_ref[slot],
                                                              preferred_element_type=F32)

    def step(j, pending, rd, wr):
        start = pl.multiple_of(j * tile, tile)
        scores = [_scores_t(st[1][pl.ds(start, tile), :], st[0]) for st in streams]
        flush(pending, rd)
        for s, (_, _, _, p_ref, _, st_ref) in zip(scores, streams):
            s = past_mask(s, j).astype(BF16)
            m = st_ref[0]
            m_new = jnp.maximum(m, jnp.max(s, axis=0, keepdims=True).astype(F32))
            alpha = jnp.exp2(m - m_new)
            p_ref[wr] = jnp.exp2(s - m_new.astype(BF16))
            st_ref[0] = m_new
            st_ref[1] = alpha

    def pair(i, carry):
        step(2 * i, jnp.where(i == 0, qi, 2 * i - 1), 0, 1)
        step(2 * i + 1, 2 * i, 1, 0)
        return carry

    npairs = qi // 2
    lax.fori_loop(0, npairs, pair, 0)
    odd = qi % 2 == 1
    last = jnp.maximum(qi - 1, 0)

    @pl.when(odd)
    def _():
        step(last, jnp.where(qi == 1, qi, qi - 2), 0, 1)
        flush(last, 1)

    @pl.when(jnp.logical_not(odd))
    def _():
        flush(last, 0)

    outs = []
    for _, _, _, _, acc_ref, _ in streams:
        dv = acc_ref.shape[0] - ONES_ROWS
        outs.append((acc_ref[0:dv, :], acc_ref[dv:dv + 1, :]))
    return outs


def _store_transposed_blocks(v_ref, vt_ref):
    nblk, rows_t, rows = vt_ref.shape
    dv = rows_t - ONES_ROWS
    for n in range(nblk):
        vt_ref[n, 0:dv, :] = v_ref[n * rows:(n + 1) * rows, :].astype(F32).T.astype(vt_ref.dtype)
        vt_ref[n, dv:rows_t, :] = jnp.ones((ONES_ROWS, rows), vt_ref.dtype)


def _moba_body(q_ref, k_ref, v_ref, o_ref, kmean_hi, kmean_lo, vt_ref, sel_ref, p_ref, acc_ref, st_ref):
    qi = pl.program_id(2)
    blk, tile = MOBA_BLOCK, MOBA_TILE
    nb = k_ref.shape[0] // blk

    @pl.when(qi == 0)
    def _():
        for n in range(nb):
            mean = jnp.mean(k_ref[n * blk:(n + 1) * blk, :].astype(F32), axis=0, keepdims=True)
            hi = mean.astype(BF16)
            kmean_hi[n:n + 1, :] = hi
            kmean_lo[n:n + 1, :] = (mean - hi.astype(F32)).astype(BF16)
        _store_transposed_blocks(v_ref, vt_ref)

    q = q_ref[...]
    gate = _scores_t(kmean_hi[...], q) + _scores_t(kmean_lo[...], q)
    blk_id = lax.broadcasted_iota(jnp.int32, gate.shape, 0)
    q_blk = 2 * qi + lax.broadcasted_iota(jnp.int32, gate.shape, 1) // blk
    past = blk_id < q_blk
    work = jnp.where(past, gate, NEG_INF)
    chosen = jnp.zeros(gate.shape, F32)
    for _ in range(MOBA_TOPK):
        best = jnp.max(work, axis=0, keepdims=True)
        first = jnp.min(jnp.where(work == best, blk_id, nb), axis=0, keepdims=True)
        pick = blk_id == first
        chosen = jnp.where(pick, 1.0, chosen)
        work = jnp.where(pick, -jnp.inf, work)
    sel_ref[...] = jnp.where(past, chosen, 0.0)

    def picked(block):
        return sel_ref[pl.ds(block, 1), :] > 0.5

    def past_mask(s, j):
        return jnp.concatenate([jnp.where(picked(2 * j), s[:blk], NEG_INF),
                                jnp.where(picked(2 * j + 1), s[blk:], NEG_INF)], axis=0)

    def diag_mask(s):
        key = lax.broadcasted_iota(jnp.int32, (tile, tile), 0)
        qry = lax.broadcasted_iota(jnp.int32, (tile, tile), 1)
        other = jnp.where(key < blk, jnp.where(picked(2 * qi), s, NEG_INF), NEG_INF)
        own_start = (qry // blk) * blk
        return jnp.where(key <= qry, jnp.where(key >= own_start, s, other), other)

    [(acc, l)] = _flash_t([(q, k_ref, vt_ref, p_ref, acc_ref, st_ref)], qi, tile, diag_mask, past_mask)
    o_ref[...] = (acc * (1.0 / l)).T.astype(o_ref.dtype)


def _moba(qkv, batch, seq):
    blk, tile = MOBA_BLOCK, MOBA_TILE
    assert seq % tile == 0
    nq = seq // tile
    nb = seq // blk
    heads = N_HEADS_B
    vmem = (2 * 2 * seq * HEAD_DIM * 2 + seq * HEAD_DIM * 2 + 4 * tile * HEAD_DIM * 2 + 2 * tile * tile * 2
            + HEAD_DIM * tile * 4 + 10 * tile * tile * 4 + (8 << 20))
    return pl.pallas_call(
        _moba_body,
        grid=(batch, heads, nq),
        in_specs=[
            pl.BlockSpec((tile, HEAD_DIM), lambda b, h, i: (b * nq + i, h)),
            pl.BlockSpec((seq, HEAD_DIM), lambda b, h, i: (b, heads + h)),
            pl.BlockSpec((seq, HEAD_DIM), lambda b, h, i: (b, 2 * heads + h)),
        ],
        out_specs=pl.BlockSpec((tile, HEAD_DIM), lambda b, h, i: (b * nq + i, h)),
        out_shape=jax.ShapeDtypeStruct((batch * seq, WIDTH_B), BF16),
        scratch_shapes=[pltpu.VMEM((nb, HEAD_DIM), BF16)] * 2
        + [pltpu.VMEM((nq, HEAD_DIM + ONES_ROWS, tile), BF16), pltpu.VMEM((nb, tile), F32),
           pltpu.VMEM((2, tile, tile), BF16), pltpu.VMEM((HEAD_DIM + ONES_ROWS, tile), F32),
           pltpu.VMEM((2, 1, tile), F32)],
        compiler_params=_params(("parallel", "parallel", "arbitrary"), vmem),
        name="moba",
    )(qkv, qkv, qkv)


def _diff_body(lq1_ref, lk1_ref, lq2_ref, lk2_ref, g_ref, q1_ref, q2_ref, k1_ref, k2_ref, v_ref, o_ref,
               vt_ref, p1_ref, p2_ref, acc1_ref, acc2_ref, st1_ref, st2_ref, *, lambda_init):
    qi = pl.program_id(2)
    t = DIFF_BLOCK

    @pl.when(qi == 0)
    def _():
        _store_transposed_blocks(v_ref, vt_ref)

    def diag_mask(s):
        key = lax.broadcasted_iota(jnp.int32, (t, t), 0)
        qry = lax.broadcasted_iota(jnp.int32, (t, t), 1)
        return jnp.where(key <= qry, s, NEG_INF)

    streams = [(q1_ref[...], k1_ref, vt_ref, p1_ref, acc1_ref, st1_ref),
               (q2_ref[...], k2_ref, vt_ref, p2_ref, acc2_ref, st2_ref)]
    (a1, l1), (a2, l2) = _flash_t(streams, qi, t, diag_mask, lambda s, j: s)

    lam = (jnp.exp(jnp.sum(lq1_ref[...] * lk1_ref[...], axis=-1, keepdims=True))
           - jnp.exp(jnp.sum(lq2_ref[...] * lk2_ref[...], axis=-1, keepdims=True)) + lambda_init)
    o = a1 * (1.0 / l1) - lam * (a2 * (1.0 / l2))
    o = o * lax.rsqrt(jnp.mean(o * o, axis=0, keepdims=True) + SUBLN_EPS)
    o_ref[...] = (o.T * g_ref[...] * (1.0 - lambda_init)).astype(o_ref.dtype)


def _diff(qkv, lq1, lk1, lq2, lk2, subln_g, lambda_init, batch, seq):
    t = DIFF_BLOCK
    assert seq % t == 0
    nq = seq // t
    heads = N_HEADS_C
    dv = 2 * HEAD_DIM
    kcol = D_MODEL // HEAD_DIM
    vcol = 2 * D_MODEL // dv

    def vec(width):
        return pl.BlockSpec((1, width), lambda b, h, i: (0, 0))

    def q_spec(part):
        return pl.BlockSpec((t, HEAD_DIM), lambda b, h, i: (b * nq + i, 2 * h + part))

    def k_spec(part):
        return pl.BlockSpec((seq, HEAD_DIM), lambda b, h, i: (b, kcol + 2 * h + part))

    vmem = (2 * (2 * seq * HEAD_DIM + seq * dv) * 2 + seq * dv * 2 + 8 * t * HEAD_DIM * 2 + 2 * t * dv * 2
            + 2 * t * dv * 4 + 4 * t * t * 2 + 10 * t * t * 4 + (8 << 20))
    return pl.pallas_call(
        functools.partial(_diff_body, lambda_init=lambda_init),
        grid=(batch, heads, nq),
        in_specs=[vec(HEAD_DIM)] * 4 + [vec(dv), q_spec(0), q_spec(1), k_spec(0), k_spec(1),
                                        pl.BlockSpec((seq, dv), lambda b, h, i: (b, vcol + h))],
        out_specs=pl.BlockSpec((t, dv), lambda b, h, i: (b * nq + i, h)),
        out_shape=jax.ShapeDtypeStruct((batch * seq, heads * dv), BF16),
        scratch_shapes=[pltpu.VMEM((nq, dv + ONES_ROWS, t), BF16)] + [pltpu.VMEM((2, t, t), BF16)] * 2
        + [pltpu.VMEM((dv + ONES_ROWS, t), F32)] * 2 + [pltpu.VMEM((2, 1, t), F32)] * 2,
        compiler_params=_params(("parallel", "parallel", "arbitrary"), vmem),
        name="diffattn",
    )(lq1.reshape(1, -1), lk1.reshape(1, -1), lq2.reshape(1, -1), lk2.reshape(1, -1),
      subln_g.reshape(1, -1), qkv, qkv, qkv, qkv, qkv)


def _lambda_init(layer):
    return 0.8 - 0.6 * math.exp(-0.3 * layer)


def kernel(x, ffa_norm, ffa_w_in, ffa_w_out, mix_norm, even_w_in, even_w_out, odd_w_in, odd_w_out,
           lambda_q1, lambda_k1, lambda_q2, lambda_k2, subln_norm, ffb_norm, ffb_w_in, ffb_w_out, final_norm):
    batch, seq, d = x.shape
    depth = ffa_norm.shape[0]
    cos, sin = _rope_tables(seq)
    h = x.reshape(batch * seq, d)
    for layer in range(depth):
        i = layer // 2
        h = _ffn(h, ffa_norm[layer], ffa_w_in, ffa_w_out, layer)
        if layer % 2 == 0:
            w_in = even_w_in[i].astype(BF16)
            wa, wb = 3 * WIDTH_A, 3 * WIDTH_B
            qkv_a = _proj(h, mix_norm[layer], w_in, cos, sin, 0, WIDTH_A, 2 * WIDTH_A, wa, SCALE, F32)
            qkv_b = _proj(h, mix_norm[layer], w_in, cos, sin, wa, WIDTH_B, 2 * WIDTH_B, wb, SCALE * LOG2_E, BF16)
            o_a = _dilated(qkv_a, batch, seq)
            o_b = _moba(qkv_b, batch, seq)
            h = _oproj(h, [o_a, o_b], even_w_out[i].astype(BF16))
        else:
            qkv = _proj(h, mix_norm[layer], odd_w_in[i].astype(BF16), cos, sin,
                        0, D_MODEL, 2 * D_MODEL, 3 * D_MODEL, SCALE * LOG2_E, BF16)
            o = _diff(qkv, lambda_q1[i], lambda_k1[i], lambda_q2[i], lambda_k2[i], subln_norm[i],
                      _lambda_init(layer), batch, seq)
            h = _oproj(h, [o], odd_w_out[i].astype(BF16))
        last = layer == depth - 1
        h = _ffn(h, ffb_norm[layer], ffb_w_in, ffb_w_out, layer, final_norm if last else None)
    return h.reshape(batch, seq, d)
```

```python
import functools
import math

import jax
import jax.numpy as jnp
from jax import lax
from jax.experimental import pallas as pl
from jax.experimental.pallas import tpu as pltpu

F32 = jnp.float32
BF16 = jnp.bfloat16

D_MODEL = 2048
HEAD_DIM = 128
HALF_DIM = HEAD_DIM // 2
N_HEADS_A = 8
N_HEADS_B = 8
N_HEADS_C = 8
WIDTH_A = N_HEADS_A * HEAD_DIM
WIDTH_B = N_HEADS_B * HEAD_DIM
DILATIONS = (1, 4, 16)
DIL_BLOCK = 128
DIL_REACH = 128
DIL_SUPER = DIL_BLOCK * DILATIONS[-1]
DIL_UNROLL = 16
MOBA_BLOCK = 256
MOBA_TOPK = 3
ATTN_KEYS = 512
ATTN_QUERIES = 2 * ATTN_KEYS
ROPE_THETA = 10000.0
NORM_EPS = 1e-6
SUBLN_EPS = 1e-5
NEG_INF = -1e30
SCALE = HEAD_DIM ** -0.5
LOG2_E = math.log2(math.e)

VMEM_V7X_BYTES = 64 * 1024 * 1024
NT_DIMS = (((1,), (1,)), ((), ()))
ONES_ROWS = 16


def _params(semantics, vmem_bytes):
    assert vmem_bytes < VMEM_V7X_BYTES
    return pltpu.CompilerParams(dimension_semantics=semantics, vmem_limit_bytes=int(vmem_bytes))


def _rms(x, g, eps):
    return x * lax.rsqrt(jnp.mean(x * x, axis=-1, keepdims=True) + eps) * g


FFN_ROWS = 1024
FFN_COLS = 512


def _ffn_body(h_hbm, g_ref, wg_ref, wu_ref, wo_ref, *rest, final):
    if final:
        fg_ref, o_ref, xn_ref, x_buf, x_sem = rest
    else:
        o_ref, xn_ref, x_buf, x_sem = rest
    i, j = pl.program_id(0), pl.program_id(1)
    rows = x_buf.shape[0]

    def x_copy(tile):
        return pltpu.make_async_copy(h_hbm.at[pl.ds(tile * rows, rows), :], x_buf, x_sem)

    @pl.when(j == 0)
    def _():
        @pl.when(i == 0)
        def _():
            x_copy(i).start()

        x_copy(i).wait()
        x = x_buf[...]
        xn_ref[...] = _rms(x, g_ref[...], NORM_EPS).astype(BF16)
        o_ref[...] = x

    @pl.when(jnp.logical_and(j == 1, i + 1 < pl.num_programs(0)))
    def _():
        x_copy(i + 1).start()

    xn = xn_ref[...]
    gate = jnp.dot(xn, wg_ref[...].astype(BF16), preferred_element_type=F32)
    up = jnp.dot(xn, wu_ref[...].astype(BF16), preferred_element_type=F32)
    act = (0.5 * gate / (1.0 + jnp.exp(-gate))) * up
    o_ref[...] += jnp.dot(act.astype(BF16), wo_ref[...].astype(BF16), preferred_element_type=F32)

    if final:
        @pl.when(j == pl.num_programs(1) - 1)
        def _():
            o_ref[...] = _rms(o_ref[...], fg_ref[...], NORM_EPS)


def _ffn(h, g, w_in, w_out, layer, final_g=None):
    n, d = h.shape
    d_ff = w_out.shape[1]
    tm, tf = FFN_ROWS, FFN_COLS
    nff = d_ff // tf
    assert n % tm == 0 and d_ff % tf == 0 and nff >= 2
    final = final_g is not None
    in_specs = [
        pl.BlockSpec(memory_space=pl.ANY),
        pl.BlockSpec((1, d), lambda i, j: (0, 0)),
        pl.BlockSpec((None, d, tf), lambda i, j: (layer, 0, j)),
        pl.BlockSpec((None, d, tf), lambda i, j: (layer, 0, nff + j)),
        pl.BlockSpec((None, tf, d), lambda i, j: (layer, j, 0)),
    ]
    args = [h, g.reshape(1, d), w_in, w_in, w_out]
    if final:
        in_specs.append(pl.BlockSpec((1, d), lambda i, j: (0, 0)))
        args.append(final_g.reshape(1, d))
    vmem = 3 * tm * d * 4 + tm * d * 2 + 2 * (2 * d * tf + tf * d) * 4 + (8 << 20)
    return pl.pallas_call(
        functools.partial(_ffn_body, final=final),
        grid=(n // tm, nff),
        in_specs=in_specs,
        out_specs=pl.BlockSpec((tm, d), lambda i, j: (i, 0)),
        out_shape=jax.ShapeDtypeStruct((n, d), F32),
        scratch_shapes=[pltpu.VMEM((tm, d), BF16), pltpu.VMEM((tm, d), F32), pltpu.SemaphoreType.DMA(())],
        compiler_params=_params(("arbitrary", "arbitrary"), vmem),
        name="ffn",
    )(*args)


PROJ_ROWS = 1024
PROJ_COLS = 1024


def _proj_body(x_ref, g_ref, w_ref, cos_ref, sin_ref, o_ref, xn_ref, *, q_tiles, q_scale):
    j = pl.program_id(1)

    @pl.when(j == 0)
    def _():
        xn_ref[...] = _rms(x_ref[...], g_ref[...], NORM_EPS).astype(BF16)

    y = jnp.dot(xn_ref[...], w_ref[...], preferred_element_type=F32)
    scale = jnp.where(j < q_tiles, q_scale, 1.0).astype(F32)
    cos = cos_ref[...] * scale
    sin = sin_ref[...] * scale
    for c in range(y.shape[1] // HEAD_DIM):
        yh = y[:, c * HEAD_DIM:(c + 1) * HEAD_DIM]
        rot = yh * cos + pltpu.roll(yh, HALF_DIM, 1) * sin
        o_ref[:, c * HEAD_DIM:(c + 1) * HEAD_DIM] = rot.astype(o_ref.dtype)


def _proj(h, g, w, cos, sin, col0, q_width, rope_width, width, q_scale, out_dtype):
    n, d = h.shape
    seq = cos.shape[1]
    tm, tn = PROJ_ROWS, PROJ_COLS
    assert n % tm == 0 and seq % tm == 0
    assert all(c % tn == 0 for c in (col0, q_width, rope_width, width))
    pos_blocks = seq // tm
    rope_tiles = rope_width // tn
    body = functools.partial(_proj_body, q_tiles=q_width // tn, q_scale=q_scale)

    def table_spec():
        return pl.BlockSpec((None, tm, HEAD_DIM),
                            lambda i, j: (jnp.where(j < rope_tiles, 0, 1), i % pos_blocks, 0))

    vmem = 2 * tm * d * 4 + tm * d * 2 + 2 * d * tn * 2 + 4 * tm * tn * 4 + 4 * tm * HEAD_DIM * 4 + (8 << 20)
    return pl.pallas_call(
        body,
        grid=(n // tm, width // tn),
        in_specs=[
            pl.BlockSpec((tm, d), lambda i, j: (i, 0)),
            pl.BlockSpec((1, d), lambda i, j: (0, 0)),
            pl.BlockSpec((d, tn), lambda i, j: (0, col0 // tn + j)),
            table_spec(),
            table_spec(),
        ],
        out_specs=pl.BlockSpec((tm, tn), lambda i, j: (i, j)),
        out_shape=jax.ShapeDtypeStruct((n, width), out_dtype),
        scratch_shapes=[pltpu.VMEM((tm, d), BF16)],
        compiler_params=_params(("parallel", "arbitrary"), vmem),
        name="proj",
    )(h, g.reshape(1, d), w, cos, sin)


def _rope_tables(seq):
    inv_freq = ROPE_THETA ** (-jnp.arange(HALF_DIM, dtype=F32) / HALF_DIM)
    ang = jnp.arange(seq, dtype=F32)[:, None] * inv_freq[None, :]
    cos, sin = jnp.cos(ang), jnp.sin(ang)
    cos = jnp.concatenate([cos, cos], axis=-1)
    sin = jnp.concatenate([-sin, sin], axis=-1)
    return jnp.stack([cos, jnp.ones_like(cos)]), jnp.stack([sin, jnp.zeros_like(sin)])


OPROJ_ROWS = 1024
OPROJ_COLS = 1024


def _oproj_body(*refs):
    *aw, h_ref, o_ref = refs
    acc = h_ref[...]
    for a_ref, w_ref in zip(aw[0::2], aw[1::2]):
        acc = acc + jnp.dot(a_ref[...], w_ref[...], preferred_element_type=F32)
    o_ref[...] = acc


def _oproj(h, parts, w):
    n, d = h.shape
    tm, tn = OPROJ_ROWS, OPROJ_COLS
    assert n % tm == 0 and d % tn == 0
    in_specs, args, vmem, row0 = [], [], 0, 0
    for a in parts:
        kdim = a.shape[1]
        assert row0 % kdim == 0
        in_specs += [pl.BlockSpec((tm, kdim), lambda i, j: (i, 0)),
                     pl.BlockSpec((kdim, tn), lambda i, j, r=row0 // kdim: (r, j))]
        args += [a, w]
        vmem += 2 * (tm * kdim + kdim * tn) * 2
        row0 += kdim
    assert row0 == w.shape[0]
    in_specs.append(pl.BlockSpec((tm, tn), lambda i, j: (i, j)))
    args.append(h)
    vmem += 6 * tm * tn * 4 + (8 << 20)
    return pl.pallas_call(
        _oproj_body,
        grid=(n // tm, d // tn),
        in_specs=in_specs,
        out_specs=pl.BlockSpec((tm, tn), lambda i, j: (i, j)),
        out_shape=jax.ShapeDtypeStruct((n, d), F32),
        compiler_params=_params(("parallel", "parallel"), vmem),
        name="oproj",
    )(*args)


def _dilated_body(q_ref, kc_ref, kp_ref, vc_ref, vp_ref, o_ref, kbuf, vbuf, *branch_bufs):
    obufs, lbufs = branch_bufs[:3], branch_bufs[3:]
    sb = pl.program_id(2)
    sup = DIL_SUPER
    kbuf[0:sup, :] = kp_ref[...]
    kbuf[sup:2 * sup, :] = kc_ref[...]
    vbuf[0:sup, :] = vp_ref[...]
    vbuf[sup:2 * sup, :] = vc_ref[...]

    row = lax.broadcasted_iota(jnp.int32, (DIL_BLOCK, 2 * DIL_BLOCK), 0)
    col = lax.broadcasted_iota(jnp.int32, (DIL_BLOCK, 2 * DIL_BLOCK), 1)
    band = jnp.logical_and(col >= row + (DIL_BLOCK - DIL_REACH), col <= row + DIL_BLOCK)

    for br, dil in enumerate(DILATIONS):
        nsub = sup // (DIL_BLOCK * dil)
        obuf, lbuf = obufs[br], lbufs[br]

        def block(t, carry, dil=dil, nsub=nsub, obuf=obuf, lbuf=lbuf):
            res = t // nsub
            sub = t % nsub
            q0 = res + dil * DIL_BLOCK * sub
            k0 = sup + q0 - dil * DIL_BLOCK
            if dil == 1:
                q0 = pl.multiple_of(q0, DIL_BLOCK)
                k0 = pl.multiple_of(k0, DIL_BLOCK)
                qs, ks = pl.ds(q0, DIL_BLOCK), pl.ds(k0, 2 * DIL_BLOCK)
            else:
                qs, ks = pl.ds(q0, DIL_BLOCK, stride=dil), pl.ds(k0, 2 * DIL_BLOCK, stride=dil)
            qb = q_ref[qs, :].astype(BF16)
            kb = kbuf[ks, :].astype(BF16)
            vb = vbuf[ks, :].astype(BF16)
            s = lax.dot_general(qb, kb, NT_DIMS, preferred_element_type=F32)
            has_prev = jnp.logical_or(sb > 0, sub > 0)
            first_col = jnp.where(has_prev, 0, DIL_BLOCK)
            s = jnp.where(jnp.logical_and(band, col >= first_col), s, NEG_INF)
            m = jnp.max(s, axis=-1, keepdims=True)
            p = jnp.exp(s - m)
            den = jnp.sum(p, axis=-1, keepdims=True)
            o = jnp.dot(p.astype(BF16), vb, preferred_element_type=F32) / den
            obuf[qs, :] = o
            lbuf[qs, :] = jnp.broadcast_to(m + jnp.log(den), (DIL_BLOCK, HEAD_DIM))
            return carry

        lax.fori_loop(0, nsub * dil, block, 0, unroll=DIL_UNROLL)

    lses = [lbuf[...] for lbuf in lbufs]
    top = jnp.maximum(jnp.maximum(lses[0], lses[1]), lses[2])
    wts = [jnp.exp(l - top) for l in lses]
    mix = wts[0] * obufs[0][...] + wts[1] * obufs[1][...] + wts[2] * obufs[2][...]
    o_ref[...] = (mix / (wts[0] + wts[1] + wts[2])).astype(o_ref.dtype)


def _dilated(qkv, batch, seq):
    sup = DIL_SUPER
    assert seq % sup == 0
    nsb = seq // sup
    heads = N_HEADS_A

    def cur(col0):
        return pl.BlockSpec((sup, HEAD_DIM), lambda b, h, s: (b * nsb + s, col0 + h))

    def prev(col0):
        return pl.BlockSpec((sup, HEAD_DIM), lambda b, h, s: (b * nsb + jnp.maximum(s - 1, 0), col0 + h))

    blk = sup * HEAD_DIM * 4
    vmem = 2 * 5 * blk + 2 * sup * HEAD_DIM * 2 + 4 * blk + 6 * blk + 8 * blk + (8 << 20)
    return pl.pallas_call(
        _dilated_body,
        grid=(batch, heads, nsb),
        in_specs=[cur(0), cur(heads), prev(heads), cur(2 * heads), prev(2 * heads)],
        out_specs=pl.BlockSpec((sup, HEAD_DIM), lambda b, h, s: (b * nsb + s, h)),
        out_shape=jax.ShapeDtypeStruct((batch * seq, WIDTH_A), BF16),
        scratch_shapes=[pltpu.VMEM((2 * sup, HEAD_DIM), F32)] * 2 + [pltpu.VMEM((sup, HEAD_DIM), F32)] * 6,
        compiler_params=_params(("parallel", "parallel", "arbitrary"), vmem),
        name="dilated",
    )(qkv, qkv, qkv, qkv, qkv)


def _scores_t(k_blk, q):
    return lax.dot_general(k_blk, q, NT_DIMS, preferred_element_type=F32)


def _flash_t(streams, qi, tk, diag_mask, past_mask):
    tq = 2 * tk
    first = 2 * qi

    def key_tile(k_ref, j):
        return k_ref[pl.ds(pl.multiple_of(j * tk, tk), tk), :]

    for q, k_ref, _, p_ref, acc_ref, st_ref in streams:
        s = jnp.concatenate(
            [diag_mask(_scores_t(key_tile(k_ref, first + half), q[half * tk:(half + 1) * tk, :]), half)
             for half in range(2)], axis=1).astype(BF16)
        m = jnp.max(s, axis=0, keepdims=True)
        p_ref[0] = jnp.exp2(s - m)
        acc_ref[...] = jnp.zeros(acc_ref.shape, F32)
        st_ref[0] = m.astype(F32)
        st_ref[1] = jnp.ones(m.shape, F32)

    def flush_own():
        for _, _, vt_ref, p_ref, acc_ref, st_ref in streams:
            pv = jnp.concatenate(
                [jnp.dot(vt_ref[first + half], p_ref[0, :, half * tk:(half + 1) * tk],
                         preferred_element_type=F32) for half in range(2)], axis=1)
            acc_ref[...] = st_ref[1] * acc_ref[...] + pv

    def flush(pending, slot):
        for _, _, vt_ref, p_ref, acc_ref, st_ref in streams:
            acc_ref[...] = st_ref[1] * acc_ref[...] + jnp.dot(vt_ref[pending], p_ref[slot],
                                                              preferred_element_type=F32)

    def step(j, do_flush, wr, mask):
        scores = [_scores_t(key_tile(st[1], j), st[0]) for st in streams]
        do_flush()
        for s, (_, _, _, p_ref, _, st_ref) in zip(scores, streams):
            s = mask(s, j).astype(BF16)
            m = st_ref[0]
            m_new = jnp.maximum(m, jnp.max(s, axis=0, keepdims=True).astype(F32))
            p_ref[wr] = jnp.exp2(s - m_new.astype(BF16))
            st_ref[0] = m_new
            st_ref[1] = jnp.exp2(m - m_new)

    def second_half_only(s, j):
        qry = lax.broadcasted_iota(jnp.int32, (tk, tq), 1)
        return jnp.where(qry >= tk, past_mask(s, j), NEG_INF)

    step(first, flush_own, 1, second_half_only)

    def pair(i, carry):
        step(2 * i, lambda: flush(jnp.where(i == 0, first, 2 * i - 1), 1), 0, past_mask)
        step(2 * i + 1, lambda: flush(2 * i, 0), 1, past_mask)
        return carry

    lax.fori_loop(0, qi, pair, 0)
    flush(jnp.where(qi == 0, first, first - 1), 1)

    outs = []
    for _, _, _, _, acc_ref, _ in streams:
        dv = acc_ref.shape[0] - ONES_ROWS
        outs.append((acc_ref[0:dv, :], acc_ref[dv:dv + 1, :]))
    return outs


def _store_transposed_blocks(v_ref, vt_ref):
    nblk, rows_t, rows = vt_ref.shape
    dv = rows_t - ONES_ROWS
    for n in range(nblk):
        vt_ref[n, 0:dv, :] = v_ref[n * rows:(n + 1) * rows, :].astype(F32).T.astype(vt_ref.dtype)
        vt_ref[n, dv:rows_t, :] = jnp.ones((ONES_ROWS, rows), vt_ref.dtype)


def _moba_body(q_ref, k_ref, v_ref, o_ref, kmean_hi, kmean_lo, vt_ref, sel_ref, p_ref, acc_ref, st_ref):
    qi = pl.program_id(2)
    blk, tk, tq = MOBA_BLOCK, ATTN_KEYS, ATTN_QUERIES
    nb = k_ref.shape[0] // blk

    @pl.when(qi == 0)
    def _():
        for n in range(nb):
            mean = jnp.mean(k_ref[n * blk:(n + 1) * blk, :].astype(F32), axis=0, keepdims=True)
            hi = mean.astype(BF16)
            kmean_hi[n:n + 1, :] = hi
            kmean_lo[n:n + 1, :] = (mean - hi.astype(F32)).astype(BF16)
        _store_transposed_blocks(v_ref, vt_ref)

    q = q_ref[...]
    gate = _scores_t(kmean_hi[...], q) + _scores_t(kmean_lo[...], q)
    blk_id = lax.broadcasted_iota(jnp.int32, gate.shape, 0)
    q_blk = (tq // blk) * qi + lax.broadcasted_iota(jnp.int32, gate.shape, 1) // blk
    past = blk_id < q_blk
    work = jnp.where(past, gate, NEG_INF)
    chosen = jnp.zeros(gate.shape, F32)
    for _ in range(MOBA_TOPK):
        best = jnp.max(work, axis=0, keepdims=True)
        first = jnp.min(jnp.where(work == best, blk_id, nb), axis=0, keepdims=True)
        pick = blk_id == first
        chosen = jnp.where(pick, 1.0, chosen)
        work = jnp.where(pick, -jnp.inf, work)
    sel_ref[...] = jnp.where(past, chosen, 0.0)

    def picked(block):
        return sel_ref[pl.ds(block, 1), :] > 0.5

    def past_mask(s, j):
        return jnp.concatenate([jnp.where(picked(2 * j), s[:blk], NEG_INF),
                                jnp.where(picked(2 * j + 1), s[blk:], NEG_INF)], axis=0)

    def diag_mask(s, half):
        key = lax.broadcasted_iota(jnp.int32, (tk, tk), 0)
        qry = lax.broadcasted_iota(jnp.int32, (tk, tk), 1)
        first_picked = sel_ref[pl.ds(2 * (2 * qi + half), 1), half * tk:(half + 1) * tk] > 0.5
        other = jnp.where(key < blk, jnp.where(first_picked, s, NEG_INF), NEG_INF)
        own_start = (qry // blk) * blk
        return jnp.where(key <= qry, jnp.where(key >= own_start, s, other), other)

    [(acc, l)] = _flash_t([(q, k_ref, vt_ref, p_ref, acc_ref, st_ref)], qi, tk, diag_mask, past_mask)
    o_ref[...] = (acc * (1.0 / l)).T.astype(o_ref.dtype)


def _moba(qkv, batch, seq):
    blk, tk, tq = MOBA_BLOCK, ATTN_KEYS, ATTN_QUERIES
    assert seq % tq == 0
    nq = seq // tq
    nb = seq // blk
    heads = N_HEADS_B
    vmem = (2 * 2 * seq * HEAD_DIM * 2 + seq * HEAD_DIM * 2 + 4 * tq * HEAD_DIM * 2 + 2 * tk * tq * 2
            + 2 * HEAD_DIM * tq * 4 + 10 * tk * tq * 4 + (8 << 20))
    return pl.pallas_call(
        _moba_body,
        grid=(batch, heads, nq),
        in_specs=[
            pl.BlockSpec((tq, HEAD_DIM), lambda b, h, i: (b * nq + i, h)),
            pl.BlockSpec((seq, HEAD_DIM), lambda b, h, i: (b, heads + h)),
            pl.BlockSpec((seq, HEAD_DIM), lambda b, h, i: (b, 2 * heads + h)),
        ],
        out_specs=pl.BlockSpec((tq, HEAD_DIM), lambda b, h, i: (b * nq + i, h)),
        out_shape=jax.ShapeDtypeStruct((batch * seq, WIDTH_B), BF16),
        scratch_shapes=[pltpu.VMEM((nb, HEAD_DIM), BF16)] * 2
        + [pltpu.VMEM((seq // tk, HEAD_DIM + ONES_ROWS, tk), BF16), pltpu.VMEM((nb, tq), F32),
           pltpu.VMEM((2, tk, tq), BF16), pltpu.VMEM((HEAD_DIM + ONES_ROWS, tq), F32),
           pltpu.VMEM((2, 1, tq), F32)],
        compiler_params=_params(("parallel", "parallel", "arbitrary"), vmem),
        name="moba",
    )(qkv, qkv, qkv)


def _diff_body(lq1_ref, lk1_ref, lq2_ref, lk2_ref, g_ref, q1_ref, q2_ref, k1_ref, k2_ref, v_ref, o_ref,
               vt_ref, p1_ref, p2_ref, acc1_ref, acc2_ref, st1_ref, st2_ref, *, lambda_init):
    qi = pl.program_id(2)
    t = ATTN_KEYS

    @pl.when(qi == 0)
    def _():
        _store_transposed_blocks(v_ref, vt_ref)

    def diag_mask(s, half):
        key = lax.broadcasted_iota(jnp.int32, (t, t), 0)
        qry = lax.broadcasted_iota(jnp.int32, (t, t), 1)
        return jnp.where(key <= qry, s, NEG_INF)

    streams = [(q1_ref[...], k1_ref, vt_ref, p1_ref, acc1_ref, st1_ref),
               (q2_ref[...], k2_ref, vt_ref, p2_ref, acc2_ref, st2_ref)]
    (a1, l1), (a2, l2) = _flash_t(streams, qi, t, diag_mask, lambda s, j: s)

    lam = (jnp.exp(jnp.sum(lq1_ref[...] * lk1_ref[...], axis=-1, keepdims=True))
           - jnp.exp(jnp.sum(lq2_ref[...] * lk2_ref[...], axis=-1, keepdims=True)) + lambda_init)
    o = a1 * (1.0 / l1) - lam * (a2 * (1.0 / l2))
    o = o * lax.rsqrt(jnp.mean(o * o, axis=0, keepdims=True) + SUBLN_EPS)
    o_ref[...] = (o.T * g_ref[...] * (1.0 - lambda_init)).astype(o_ref.dtype)


def _diff(qkv, lq1, lk1, lq2, lk2, subln_g, lambda_init, batch, seq):
    t, tq = ATTN_KEYS, ATTN_QUERIES
    assert seq % tq == 0
    nq = seq // tq
    heads = N_HEADS_C
    dv = 2 * HEAD_DIM
    kcol = D_MODEL // HEAD_DIM
    vcol = 2 * D_MODEL // dv

    def vec(width):
        return pl.BlockSpec((1, width), lambda b, h, i: (0, 0))

    def q_spec(part):
        return pl.BlockSpec((tq, HEAD_DIM), lambda b, h, i: (b * nq + i, 2 * h + part))

    def k_spec(part):
        return pl.BlockSpec((seq, HEAD_DIM), lambda b, h, i: (b, kcol + 2 * h + part))

    vmem = (2 * (2 * seq * HEAD_DIM + seq * dv) * 2 + seq * dv * 2 + 8 * tq * HEAD_DIM * 2 + 2 * tq * dv * 2
            + 2 * tq * dv * 4 + 4 * t * tq * 2 + 10 * t * tq * 4 + (8 << 20))
    return pl.pallas_call(
        functools.partial(_diff_body, lambda_init=lambda_init),
        grid=(batch, heads, nq),
        in_specs=[vec(HEAD_DIM)] * 4 + [vec(dv), q_spec(0), q_spec(1), k_spec(0), k_spec(1),
                                        pl.BlockSpec((seq, dv), lambda b, h, i: (b, vcol + h))],
        out_specs=pl.BlockSpec((tq, dv), lambda b, h, i: (b * nq + i, h)),
        out_shape=jax.ShapeDtypeStruct((batch * seq, heads * dv), BF16),
        scratch_shapes=[pltpu.VMEM((seq // t, dv + ONES_ROWS, t), BF16)] + [pltpu.VMEM((2, t, tq), BF16)] * 2
        + [pltpu.VMEM((dv + ONES_ROWS, tq), F32)] * 2 + [pltpu.VMEM((2, 1, tq), F32)] * 2,
        compiler_params=_params(("parallel", "parallel", "arbitrary"), vmem),
        name="diffattn",
    )(lq1.reshape(1, -1), lk1.reshape(1, -1), lq2.reshape(1, -1), lk2.reshape(1, -1),
      subln_g.reshape(1, -1), qkv, qkv, qkv, qkv, qkv)


def _lambda_init(layer):
    return 0.8 - 0.6 * math.exp(-0.3 * layer)


def kernel(x, ffa_norm, ffa_w_in, ffa_w_out, mix_norm, even_w_in, even_w_out, odd_w_in, odd_w_out,
           lambda_q1, lambda_k1, lambda_q2, lambda_k2, subln_norm, ffb_norm, ffb_w_in, ffb_w_out, final_norm):
    batch, seq, d = x.shape
    depth = ffa_norm.shape[0]
    cos, sin = _rope_tables(seq)
    h = x.reshape(batch * seq, d)
    for layer in range(depth):
        i = layer // 2
        h = _ffn(h, ffa_norm[layer], ffa_w_in, ffa_w_out, layer)
        if layer % 2 == 0:
            w_in = even_w_in[i].astype(BF16)
            wa, wb = 3 * WIDTH_A, 3 * WIDTH_B
            qkv_a = _proj(h, mix_norm[layer], w_in, cos, sin, 0, WIDTH_A, 2 * WIDTH_A, wa, SCALE, F32)
            qkv_b = _proj(h, mix_norm[layer], w_in, cos, sin, wa, WIDTH_B, 2 * WIDTH_B, wb, SCALE * LOG2_E, BF16)
            o_a = _dilated(qkv_a, batch, seq)
            o_b = _moba(qkv_b, batch, seq)
            h = _oproj(h, [o_a, o_b], even_w_out[i].astype(BF16))
        else:
            qkv = _proj(h, mix_norm[layer], odd_w_in[i].astype(BF16), cos, sin,
                        0, D_MODEL, 2 * D_MODEL, 3 * D_MODEL, SCALE * LOG2_E, BF16)
            o = _diff(qkv, lambda_q1[i], lambda_k1[i], lambda_q2[i], lambda_k2[i], subln_norm[i],
                      _lambda_init(layer), batch, seq)
            h = _oproj(h, [o], odd_w_out[i].astype(BF16))
        last = layer == depth - 1
        h = _ffn(h, ffb_norm[layer], ffb_w_in, ffb_w_out, layer, final_norm if last else None)
    return h.reshape(batch, seq, d)
```

```python
import functools
import math

import jax
import jax.numpy as jnp
from jax import lax
from jax.experimental import pallas as pl
from jax.experimental.pallas import tpu as pltpu

F32 = jnp.float32
BF16 = jnp.bfloat16

D_MODEL = 2048
HEAD_DIM = 128
HALF_DIM = HEAD_DIM // 2
N_HEADS_A = 8
N_HEADS_B = 8
N_HEADS_C = 8
WIDTH_A = N_HEADS_A * HEAD_DIM
WIDTH_B = N_HEADS_B * HEAD_DIM
DILATIONS = (1, 4, 16)
DIL_BLOCK = 128
DIL_REACH = 128
DIL_SUPER = DIL_BLOCK * DILATIONS[-1]
DIL_UNROLL = 16
MOBA_BLOCK = 256
MOBA_TOPK = 3
ATTN_KEYS = 512
MOBA_HALVES = 2
DIFF_HALVES = 1
ROPE_THETA = 10000.0
NORM_EPS = 1e-6
SUBLN_EPS = 1e-5
NEG_INF = -1e30
SCALE = HEAD_DIM ** -0.5
LOG2_E = math.log2(math.e)

VMEM_V7X_BYTES = 64 * 1024 * 1024
NT_DIMS = (((1,), (1,)), ((), ()))
ONES_ROWS = 16


def _params(semantics, vmem_bytes):
    assert vmem_bytes < VMEM_V7X_BYTES
    return pltpu.CompilerParams(dimension_semantics=semantics, vmem_limit_bytes=int(vmem_bytes))


def _rms(x, g, eps):
    return x * lax.rsqrt(jnp.mean(x * x, axis=-1, keepdims=True) + eps) * g


FFN_ROWS = 1024
FFN_COLS = 512


def _ffn_body(h_hbm, g_ref, wg_ref, wu_ref, wo_ref, *rest, final):
    if final:
        fg_ref, o_ref, xn_ref, x_buf, x_sem = rest
    else:
        o_ref, xn_ref, x_buf, x_sem = rest
    i, j = pl.program_id(0), pl.program_id(1)
    rows = x_buf.shape[0]

    def x_copy(tile):
        return pltpu.make_async_copy(h_hbm.at[pl.ds(tile * rows, rows), :], x_buf, x_sem)

    @pl.when(j == 0)
    def _():
        @pl.when(i == 0)
        def _():
            x_copy(i).start()

        x_copy(i).wait()
        x = x_buf[...]
        xn_ref[...] = _rms(x, g_ref[...], NORM_EPS).astype(BF16)
        o_ref[...] = x

    @pl.when(jnp.logical_and(j == 1, i + 1 < pl.num_programs(0)))
    def _():
        x_copy(i + 1).start()

    xn = xn_ref[...]
    gate = jnp.dot(xn, wg_ref[...].astype(BF16), preferred_element_type=F32)
    up = jnp.dot(xn, wu_ref[...].astype(BF16), preferred_element_type=F32)
    act = (0.5 * gate / (1.0 + jnp.exp(-gate))) * up
    o_ref[...] += jnp.dot(act.astype(BF16), wo_ref[...].astype(BF16), preferred_element_type=F32)

    if final:
        @pl.when(j == pl.num_programs(1) - 1)
        def _():
            o_ref[...] = _rms(o_ref[...], fg_ref[...], NORM_EPS)


def _ffn(h, g, w_in, w_out, layer, final_g=None):
    n, d = h.shape
    d_ff = w_out.shape[1]
    tm, tf = FFN_ROWS, FFN_COLS
    nff = d_ff // tf
    assert n % tm == 0 and d_ff % tf == 0 and nff >= 2
    final = final_g is not None
    in_specs = [
        pl.BlockSpec(memory_space=pl.ANY),
        pl.BlockSpec((1, d), lambda i, j: (0, 0)),
        pl.BlockSpec((None, d, tf), lambda i, j: (layer, 0, j)),
        pl.BlockSpec((None, d, tf), lambda i, j: (layer, 0, nff + j)),
        pl.BlockSpec((None, tf, d), lambda i, j: (layer, j, 0)),
    ]
    args = [h, g.reshape(1, d), w_in, w_in, w_out]
    if final:
        in_specs.append(pl.BlockSpec((1, d), lambda i, j: (0, 0)))
        args.append(final_g.reshape(1, d))
    vmem = 3 * tm * d * 4 + tm * d * 2 + 2 * (2 * d * tf + tf * d) * 4 + (8 << 20)
    return pl.pallas_call(
        functools.partial(_ffn_body, final=final),
        grid=(n // tm, nff),
        in_specs=in_specs,
        out_specs=pl.BlockSpec((tm, d), lambda i, j: (i, 0)),
        out_shape=jax.ShapeDtypeStruct((n, d), F32),
        scratch_shapes=[pltpu.VMEM((tm, d), BF16), pltpu.VMEM((tm, d), F32), pltpu.SemaphoreType.DMA(())],
        compiler_params=_params(("arbitrary", "arbitrary"), vmem),
        name="ffn",
    )(*args)


PROJ_ROWS = 1024
PROJ_COLS = 1024


def _proj_body(x_ref, g_ref, w_ref, cos_ref, sin_ref, o_ref, xn_ref, *, q_tiles, q_scale):
    j = pl.program_id(1)

    @pl.when(j == 0)
    def _():
        xn_ref[...] = _rms(x_ref[...], g_ref[...], NORM_EPS).astype(BF16)

    y = jnp.dot(xn_ref[...], w_ref[...], preferred_element_type=F32)
    scale = jnp.where(j < q_tiles, q_scale, 1.0).astype(F32)
    cos = cos_ref[...] * scale
    sin = sin_ref[...] * scale
    for c in range(y.shape[1] // HEAD_DIM):
        yh = y[:, c * HEAD_DIM:(c + 1) * HEAD_DIM]
        rot = yh * cos + pltpu.roll(yh, HALF_DIM, 1) * sin
        o_ref[:, c * HEAD_DIM:(c + 1) * HEAD_DIM] = rot.astype(o_ref.dtype)


def _proj(h, g, w, cos, sin, col0, q_width, rope_width, width, q_scale, out_dtype):
    n, d = h.shape
    seq = cos.shape[1]
    tm, tn = PROJ_ROWS, PROJ_COLS
    assert n % tm == 0 and seq % tm == 0
    assert all(c % tn == 0 for c in (col0, q_width, rope_width, width))
    pos_blocks = seq // tm
    rope_tiles = rope_width // tn
    body = functools.partial(_proj_body, q_tiles=q_width // tn, q_scale=q_scale)

    def table_spec():
        return pl.BlockSpec((None, tm, HEAD_DIM),
                            lambda i, j: (jnp.where(j < rope_tiles, 0, 1), i % pos_blocks, 0))

    vmem = 2 * tm * d * 4 + tm * d * 2 + 2 * d * tn * 2 + 4 * tm * tn * 4 + 4 * tm * HEAD_DIM * 4 + (8 << 20)
    return pl.pallas_call(
        body,
        grid=(n // tm, width // tn),
        in_specs=[
            pl.BlockSpec((tm, d), lambda i, j: (i, 0)),
            pl.BlockSpec((1, d), lambda i, j: (0, 0)),
            pl.BlockSpec((d, tn), lambda i, j: (0, col0 // tn + j)),
            table_spec(),
            table_spec(),
        ],
        out_specs=pl.BlockSpec((tm, tn), lambda i, j: (i, j)),
        out_shape=jax.ShapeDtypeStruct((n, width), out_dtype),
        scratch_shapes=[pltpu.VMEM((tm, d), BF16)],
        compiler_params=_params(("parallel", "arbitrary"), vmem),
        name="proj",
    )(h, g.reshape(1, d), w, cos, sin)


def _rope_tables(seq):
    inv_freq = ROPE_THETA ** (-jnp.arange(HALF_DIM, dtype=F32) / HALF_DIM)
    ang = jnp.arange(seq, dtype=F32)[:, None] * inv_freq[None, :]
    cos, sin = jnp.cos(ang), jnp.sin(ang)
    cos = jnp.concatenate([cos, cos], axis=-1)
    sin = jnp.concatenate([-sin, sin], axis=-1)
    return jnp.stack([cos, jnp.ones_like(cos)]), jnp.stack([sin, jnp.zeros_like(sin)])


OPROJ_ROWS = 1024
OPROJ_COLS = 1024


def _oproj_body(*refs):
    *aw, h_ref, o_ref = refs
    acc = h_ref[...]
    for a_ref, w_ref in zip(aw[0::2], aw[1::2]):
        acc = acc + jnp.dot(a_ref[...], w_ref[...], preferred_element_type=F32)
    o_ref[...] = acc


def _oproj(h, parts, w):
    n, d = h.shape
    tm, tn = OPROJ_ROWS, OPROJ_COLS
    assert n % tm == 0 and d % tn == 0
    in_specs, args, vmem, row0 = [], [], 0, 0
    for a in parts:
        kdim = a.shape[1]
        assert row0 % kdim == 0
        in_specs += [pl.BlockSpec((tm, kdim), lambda i, j: (i, 0)),
                     pl.BlockSpec((kdim, tn), lambda i, j, r=row0 // kdim: (r, j))]
        args += [a, w]
        vmem += 2 * (tm * kdim + kdim * tn) * 2
        row0 += kdim
    assert row0 == w.shape[0]
    in_specs.append(pl.BlockSpec((tm, tn), lambda i, j: (i, j)))
    args.append(h)
    vmem += 6 * tm * tn * 4 + (8 << 20)
    return pl.pallas_call(
        _oproj_body,
        grid=(n // tm, d // tn),
        in_specs=in_specs,
        out_specs=pl.BlockSpec((tm, tn), lambda i, j: (i, j)),
        out_shape=jax.ShapeDtypeStruct((n, d), F32),
        compiler_params=_params(("parallel", "parallel"), vmem),
        name="oproj",
    )(*args)


def _dilated_body(q_ref, kc_ref, kp_ref, vc_ref, vp_ref, o_ref, kbuf, vbuf, *branch_bufs):
    obufs, lbufs = branch_bufs[:3], branch_bufs[3:]
    sb = pl.program_id(2)
    sup = DIL_SUPER
    kbuf[0:sup, :] = kp_ref[...]
    kbuf[sup:2 * sup, :] = kc_ref[...]
    vbuf[0:sup, :] = vp_ref[...]
    vbuf[sup:2 * sup, :] = vc_ref[...]

    row = lax.broadcasted_iota(jnp.int32, (DIL_BLOCK, 2 * DIL_BLOCK), 0)
    col = lax.broadcasted_iota(jnp.int32, (DIL_BLOCK, 2 * DIL_BLOCK), 1)
    band = jnp.logical_and(col >= row + (DIL_BLOCK - DIL_REACH), col <= row + DIL_BLOCK)

    for br, dil in enumerate(DILATIONS):
        nsub = sup // (DIL_BLOCK * dil)
        obuf, lbuf = obufs[br], lbufs[br]

        def block(t, carry, dil=dil, nsub=nsub, obuf=obuf, lbuf=lbuf):
            res = t // nsub
            sub = t % nsub
            q0 = res + dil * DIL_BLOCK * sub
            k0 = sup + q0 - dil * DIL_BLOCK
            if dil == 1:
                q0 = pl.multiple_of(q0, DIL_BLOCK)
                k0 = pl.multiple_of(k0, DIL_BLOCK)
                qs, ks = pl.ds(q0, DIL_BLOCK), pl.ds(k0, 2 * DIL_BLOCK)
            else:
                qs, ks = pl.ds(q0, DIL_BLOCK, stride=dil), pl.ds(k0, 2 * DIL_BLOCK, stride=dil)
            qb = q_ref[qs, :].astype(BF16)
            kb = kbuf[ks, :].astype(BF16)
            vb = vbuf[ks, :].astype(BF16)
            s = lax.dot_general(qb, kb, NT_DIMS, preferred_element_type=F32)
            has_prev = jnp.logical_or(sb > 0, sub > 0)
            first_col = jnp.where(has_prev, 0, DIL_BLOCK)
            s = jnp.where(jnp.logical_and(band, col >= first_col), s, NEG_INF)
            m = jnp.max(s, axis=-1, keepdims=True)
            p = jnp.exp(s - m)
            den = jnp.sum(p, axis=-1, keepdims=True)
            o = jnp.dot(p.astype(BF16), vb, preferred_element_type=F32) / den
            obuf[qs, :] = o
            lbuf[qs, :] = jnp.broadcast_to(m + jnp.log(den), (DIL_BLOCK, HEAD_DIM))
            return carry

        lax.fori_loop(0, nsub * dil, block, 0, unroll=DIL_UNROLL)

    lses = [lbuf[...] for lbuf in lbufs]
    top = jnp.maximum(jnp.maximum(lses[0], lses[1]), lses[2])
    wts = [jnp.exp(l - top) for l in lses]
    mix = wts[0] * obufs[0][...] + wts[1] * obufs[1][...] + wts[2] * obufs[2][...]
    o_ref[...] = (mix / (wts[0] + wts[1] + wts[2])).astype(o_ref.dtype)


def _dilated(qkv, batch, seq):
    sup = DIL_SUPER
    assert seq % sup == 0
    nsb = seq // sup
    heads = N_HEADS_A

    def cur(col0):
        return pl.BlockSpec((sup, HEAD_DIM), lambda b, h, s: (b * nsb + s, col0 + h))

    def prev(col0):
        return pl.BlockSpec((sup, HEAD_DIM), lambda b, h, s: (b * nsb + jnp.maximum(s - 1, 0), col0 + h))

    blk = sup * HEAD_DIM * 4
    vmem = 2 * 5 * blk + 2 * sup * HEAD_DIM * 2 + 4 * blk + 6 * blk + 8 * blk + (8 << 20)
    return pl.pallas_call(
        _dilated_body,
        grid=(batch, heads, nsb),
        in_specs=[cur(0), cur(heads), prev(heads), cur(2 * heads), prev(2 * heads)],
        out_specs=pl.BlockSpec((sup, HEAD_DIM), lambda b, h, s: (b * nsb + s, h)),
        out_shape=jax.ShapeDtypeStruct((batch * seq, WIDTH_A), BF16),
        scratch_shapes=[pltpu.VMEM((2 * sup, HEAD_DIM), F32)] * 2 + [pltpu.VMEM((sup, HEAD_DIM), F32)] * 6,
        compiler_params=_params(("parallel", "parallel", "arbitrary"), vmem),
        name="dilated",
    )(qkv, qkv, qkv, qkv, qkv)


def _scores_t(k_blk, q):
    return lax.dot_general(k_blk, q, NT_DIMS, preferred_element_type=F32)


def _flash_t(streams, qi, tk, halves, diag_mask, past_mask):
    assert halves in (1, 2)
    tq = halves * tk
    first = halves * qi

    def key_tile(k_ref, j):
        return k_ref[pl.ds(pl.multiple_of(j * tk, tk), tk), :]

    for q, k_ref, _, p_ref, acc_ref, st_ref in streams:
        s = jnp.concatenate(
            [diag_mask(_scores_t(key_tile(k_ref, first + half), q[half * tk:(half + 1) * tk, :]), half)
             for half in range(halves)], axis=1).astype(BF16)
        m = jnp.max(s, axis=0, keepdims=True)
        p_ref[0] = jnp.exp2(s - m)
        acc_ref[...] = jnp.zeros(acc_ref.shape, F32)
        st_ref[0] = m.astype(F32)
        st_ref[1] = jnp.ones(m.shape, F32)

    def flush_own():
        for _, _, vt_ref, p_ref, acc_ref, st_ref in streams:
            pv = jnp.concatenate(
                [jnp.dot(vt_ref[first + half], p_ref[0, :, half * tk:(half + 1) * tk],
                         preferred_element_type=F32) for half in range(halves)], axis=1)
            acc_ref[...] = st_ref[1] * acc_ref[...] + pv

    def flush(pending, slot):
        for _, _, vt_ref, p_ref, acc_ref, st_ref in streams:
            acc_ref[...] = st_ref[1] * acc_ref[...] + jnp.dot(vt_ref[pending], p_ref[slot],
                                                              preferred_element_type=F32)

    def step(j, do_flush, wr, mask):
        scores = [_scores_t(key_tile(st[1], j), st[0]) for st in streams]
        do_flush()
        for s, (_, _, _, p_ref, _, st_ref) in zip(scores, streams):
            s = mask(s, j).astype(BF16)
            m = st_ref[0]
            m_new = jnp.maximum(m, jnp.max(s, axis=0, keepdims=True).astype(F32))
            p_ref[wr] = jnp.exp2(s - m_new.astype(BF16))
            st_ref[0] = m_new
            st_ref[1] = jnp.exp2(m - m_new)

    if halves == 2:
        def second_half_only(s, j):
            qry = lax.broadcasted_iota(jnp.int32, (tk, tq), 1)
            return jnp.where(qry >= tk, past_mask(s, j), NEG_INF)

        step(first, flush_own, 1, second_half_only)

        def pair(i, carry):
            step(2 * i, lambda: flush(jnp.where(i == 0, first, 2 * i - 1), 1), 0, past_mask)
            step(2 * i + 1, lambda: flush(2 * i, 0), 1, past_mask)
            return carry

        lax.fori_loop(0, qi, pair, 0)
        flush(jnp.where(qi == 0, first, first - 1), 1)
    else:
        def pair(i, carry):
            step(2 * i, lambda: flush(jnp.where(i == 0, first, 2 * i - 1), 0), 1, past_mask)
            step(2 * i + 1, lambda: flush(2 * i, 1), 0, past_mask)
            return carry

        lax.fori_loop(0, qi // 2, pair, 0)
        odd = qi % 2 == 1
        last = jnp.maximum(qi - 1, 0)

        @pl.when(odd)
        def _():
            step(last, lambda: flush(jnp.where(qi == 1, first, qi - 2), 0), 1, past_mask)
            flush(last, 1)

        @pl.when(jnp.logical_not(odd))
        def _():
            flush(last, 0)

    outs = []
    for _, _, _, _, acc_ref, _ in streams:
        dv = acc_ref.shape[0] - ONES_ROWS
        outs.append((acc_ref[0:dv, :], acc_ref[dv:dv + 1, :]))
    return outs


def _store_transposed_blocks(v_ref, vt_ref):
    nblk, rows_t, rows = vt_ref.shape
    dv = rows_t - ONES_ROWS
    for n in range(nblk):
        vt_ref[n, 0:dv, :] = v_ref[n * rows:(n + 1) * rows, :].astype(F32).T.astype(vt_ref.dtype)
        vt_ref[n, dv:rows_t, :] = jnp.ones((ONES_ROWS, rows), vt_ref.dtype)


def _moba_body(q_ref, k_ref, v_ref, o_ref, kmean_hi, kmean_lo, vt_ref, sel_ref, p_ref, acc_ref, st_ref):
    qi = pl.program_id(2)
    blk, tk, tq = MOBA_BLOCK, ATTN_KEYS, MOBA_HALVES * ATTN_KEYS
    nb = k_ref.shape[0] // blk

    @pl.when(qi == 0)
    def _():
        for n in range(nb):
            mean = jnp.mean(k_ref[n * blk:(n + 1) * blk, :].astype(F32), axis=0, keepdims=True)
            hi = mean.astype(BF16)
            kmean_hi[n:n + 1, :] = hi
            kmean_lo[n:n + 1, :] = (mean - hi.astype(F32)).astype(BF16)
        _store_transposed_blocks(v_ref, vt_ref)

    q = q_ref[...]
    gate = _scores_t(kmean_hi[...], q) + _scores_t(kmean_lo[...], q)
    blk_id = lax.broadcasted_iota(jnp.int32, gate.shape, 0)
    q_blk = (tq // blk) * qi + lax.broadcasted_iota(jnp.int32, gate.shape, 1) // blk
    past = blk_id < q_blk
    work = jnp.where(past, gate, NEG_INF)
    chosen = jnp.zeros(gate.shape, F32)
    for _ in range(MOBA_TOPK):
        best = jnp.max(work, axis=0, keepdims=True)
        first = jnp.min(jnp.where(work == best, blk_id, nb), axis=0, keepdims=True)
        pick = blk_id == first
        chosen = jnp.where(pick, 1.0, chosen)
        work = jnp.where(pick, -jnp.inf, work)
    sel_ref[...] = jnp.where(past, chosen, 0.0)

    def picked(block):
        return sel_ref[pl.ds(block, 1), :] > 0.5

    def past_mask(s, j):
        return jnp.concatenate([jnp.where(picked(2 * j), s[:blk], NEG_INF),
                                jnp.where(picked(2 * j + 1), s[blk:], NEG_INF)], axis=0)

    def diag_mask(s, half):
        key = lax.broadcasted_iota(jnp.int32, (tk, tk), 0)
        qry = lax.broadcasted_iota(jnp.int32, (tk, tk), 1)
        first_picked = sel_ref[pl.ds(2 * (MOBA_HALVES * qi + half), 1), half * tk:(half + 1) * tk] > 0.5
        other = jnp.where(key < blk, jnp.where(first_picked, s, NEG_INF), NEG_INF)
        own_start = (qry // blk) * blk
        return jnp.where(key <= qry, jnp.where(key >= own_start, s, other), other)

    [(acc, l)] = _flash_t([(q, k_ref, vt_ref, p_ref, acc_ref, st_ref)], qi, tk, MOBA_HALVES, diag_mask, past_mask)
    o_ref[...] = (acc * (1.0 / l)).T.astype(o_ref.dtype)


def _moba(qkv, batch, seq):
    blk, tk, tq = MOBA_BLOCK, ATTN_KEYS, MOBA_HALVES * ATTN_KEYS
    assert seq % tq == 0
    nq = seq // tq
    nb = seq // blk
    heads = N_HEADS_B
    vmem = (2 * 2 * seq * HEAD_DIM * 2 + seq * HEAD_DIM * 2 + 4 * tq * HEAD_DIM * 2 + 2 * tk * tq * 2
            + 2 * HEAD_DIM * tq * 4 + 10 * tk * tq * 4 + (8 << 20))
    return pl.pallas_call(
        _moba_body,
        grid=(batch, heads, nq),
        in_specs=[
            pl.BlockSpec((tq, HEAD_DIM), lambda b, h, i: (b * nq + i, h)),
            pl.BlockSpec((seq, HEAD_DIM), lambda b, h, i: (b, heads + h)),
            pl.BlockSpec((seq, HEAD_DIM), lambda b, h, i: (b, 2 * heads + h)),
        ],
        out_specs=pl.BlockSpec((tq, HEAD_DIM), lambda b, h, i: (b * nq + i, h)),
        out_shape=jax.ShapeDtypeStruct((batch * seq, WIDTH_B), BF16),
        scratch_shapes=[pltpu.VMEM((nb, HEAD_DIM), BF16)] * 2
        + [pltpu.VMEM((seq // tk, HEAD_DIM + ONES_ROWS, tk), BF16), pltpu.VMEM((nb, tq), F32),
           pltpu.VMEM((2, tk, tq), BF16), pltpu.VMEM((HEAD_DIM + ONES_ROWS, tq), F32),
           pltpu.VMEM((2, 1, tq), F32)],
        compiler_params=_params(("parallel", "parallel", "arbitrary"), vmem),
        name="moba",
    )(qkv, qkv, qkv)


def _diff_body(lq1_ref, lk1_ref, lq2_ref, lk2_ref, g_ref, q1_ref, q2_ref, k1_ref, k2_ref, v_ref, o_ref,
               vt_ref, p1_ref, p2_ref, acc1_ref, acc2_ref, st1_ref, st2_ref, *, lambda_init):
    qi = pl.program_id(2)
    t = ATTN_KEYS

    @pl.when(qi == 0)
    def _():
        _store_transposed_blocks(v_ref, vt_ref)

    def diag_mask(s, half):
        key = lax.broadcasted_iota(jnp.int32, (t, t), 0)
        qry = lax.broadcasted_iota(jnp.int32, (t, t), 1)
        return jnp.where(key <= qry, s, NEG_INF)

    streams = [(q1_ref[...], k1_ref, vt_ref, p1_ref, acc1_ref, st1_ref),
               (q2_ref[...], k2_ref, vt_ref, p2_ref, acc2_ref, st2_ref)]
    (a1, l1), (a2, l2) = _flash_t(streams, qi, t, DIFF_HALVES, diag_mask, lambda s, j: s)

    lam = (jnp.exp(jnp.sum(lq1_ref[...] * lk1_ref[...], axis=-1, keepdims=True))
           - jnp.exp(jnp.sum(lq2_ref[...] * lk2_ref[...], axis=-1, keepdims=True)) + lambda_init)
    o = a1 * (1.0 / l1) - lam * (a2 * (1.0 / l2))
    o = o * lax.rsqrt(jnp.mean(o * o, axis=0, keepdims=True) + SUBLN_EPS)
    o_ref[...] = (o.T * g_ref[...] * (1.0 - lambda_init)).astype(o_ref.dtype)


def _diff(qkv, lq1, lk1, lq2, lk2, subln_g, lambda_init, batch, seq):
    t, tq = ATTN_KEYS, DIFF_HALVES * ATTN_KEYS
    assert seq % tq == 0
    nq = seq // tq
    heads = N_HEADS_C
    dv = 2 * HEAD_DIM
    kcol = D_MODEL // HEAD_DIM
    vcol = 2 * D_MODEL // dv

    def vec(width):
        return pl.BlockSpec((1, width), lambda b, h, i: (0, 0))

    def q_spec(part):
        return pl.BlockSpec((tq, HEAD_DIM), lambda b, h, i: (b * nq + i, 2 * h + part))

    def k_spec(part):
        return pl.BlockSpec((seq, HEAD_DIM), lambda b, h, i: (b, kcol + 2 * h + part))

    vmem = (2 * (2 * seq * HEAD_DIM + seq * dv) * 2 + seq * dv * 2 + 8 * tq * HEAD_DIM * 2 + 2 * tq * dv * 2
            + 2 * tq * dv * 4 + 4 * t * tq * 2 + 10 * t * tq * 4 + (8 << 20))
    return pl.pallas_call(
        functools.partial(_diff_body, lambda_init=lambda_init),
        grid=(batch, heads, nq),
        in_specs=[vec(HEAD_DIM)] * 4 + [vec(dv), q_spec(0), q_spec(1), k_spec(0), k_spec(1),
                                        pl.BlockSpec((seq, dv), lambda b, h, i: (b, vcol + h))],
        out_specs=pl.BlockSpec((tq, dv), lambda b, h, i: (b * nq + i, h)),
        out_shape=jax.ShapeDtypeStruct((batch * seq, heads * dv), BF16),
        scratch_shapes=[pltpu.VMEM((seq // t, dv + ONES_ROWS, t), BF16)] + [pltpu.VMEM((2, t, tq), BF16)] * 2
        + [pltpu.VMEM((dv + ONES_ROWS, tq), F32)] * 2 + [pltpu.VMEM((2, 1, tq), F32)] * 2,
        compiler_params=_params(("parallel", "parallel", "arbitrary"), vmem),
        name="diffattn",
    )(lq1.reshape(1, -1), lk1.reshape(1, -1), lq2.reshape(1, -1), lk2.reshape(1, -1),
      subln_g.reshape(1, -1), qkv, qkv, qkv, qkv, qkv)


def _lambda_init(layer):
    return 0.8 - 0.6 * math.exp(-0.3 * layer)


def kernel(x, ffa_norm, ffa_w_in, ffa_w_out, mix_norm, even_w_in, even_w_out, odd_w_in, odd_w_out,
           lambda_q1, lambda_k1, lambda_q2, lambda_k2, subln_norm, ffb_norm, ffb_w_in, ffb_w_out, final_norm):
    batch, seq, d = x.shape
    depth = ffa_norm.shape[0]
    cos, sin = _rope_tables(seq)
    h = x.reshape(batch * seq, d)
    for layer in range(depth):
        i = layer // 2
        h = _ffn(h, ffa_norm[layer], ffa_w_in, ffa_w_out, layer)
        if layer % 2 == 0:
            w_in = even_w_in[i].astype(BF16)
            wa, wb = 3 * WIDTH_A, 3 * WIDTH_B
            qkv_a = _proj(h, mix_norm[layer], w_in, cos, sin, 0, WIDTH_A, 2 * WIDTH_A, wa, SCALE, F32)
            qkv_b = _proj(h, mix_norm[layer], w_in, cos, sin, wa, WIDTH_B, 2 * WIDTH_B, wb, SCALE * LOG2_E, BF16)
            o_a = _dilated(qkv_a, batch, seq)
            o_b = _moba(qkv_b, batch, seq)
            h = _oproj(h, [o_a, o_b], even_w_out[i].astype(BF16))
        else:
            qkv = _proj(h, mix_norm[layer], odd_w_in[i].astype(BF16), cos, sin,
                        0, D_MODEL, 2 * D_MODEL, 3 * D_MODEL, SCALE * LOG2_E, BF16)
            o = _diff(qkv, lambda_q1[i], lambda_k1[i], lambda_q2[i], lambda_k2[i], subln_norm[i],
                      _lambda_init(layer), batch, seq)
            h = _oproj(h, [o], odd_w_out[i].astype(BF16))
        last = layer == depth - 1
        h = _ffn(h, ffb_norm[layer], ffb_w_in, ffb_w_out, layer, final_norm if last else None)
    return h.reshape(batch, seq, d)
```

```python
import functools
import math

import jax
import jax.numpy as jnp
from jax import lax
from jax.experimental import pallas as pl
from jax.experimental.pallas import tpu as pltpu

F32 = jnp.float32
BF16 = jnp.bfloat16

D_MODEL = 2048
HEAD_DIM = 128
HALF_DIM = HEAD_DIM // 2
N_HEADS_A = 8
N_HEADS_B = 8
N_HEADS_C = 8
WIDTH_A = N_HEADS_A * HEAD_DIM
WIDTH_B = N_HEADS_B * HEAD_DIM
DILATIONS = (1, 4, 16)
DIL_BLOCK = 128
DIL_REACH = 128
DIL_SUPER = DIL_BLOCK * DILATIONS[-1]
DIL_UNROLL = 16
MOBA_BLOCK = 256
MOBA_TOPK = 3
ATTN_KEYS = 512
MOBA_HALVES = 2
DIFF_HALVES = 1
ROPE_THETA = 10000.0
NORM_EPS = 1e-6
SUBLN_EPS = 1e-5
NEG_INF = -1e30
SCALE = HEAD_DIM ** -0.5
LOG2_E = math.log2(math.e)

VMEM_V7X_BYTES = 64 * 1024 * 1024
NT_DIMS = (((1,), (1,)), ((), ()))
ONES_ROWS = 16


def _params(semantics, vmem_bytes):
    assert vmem_bytes < VMEM_V7X_BYTES
    return pltpu.CompilerParams(dimension_semantics=semantics, vmem_limit_bytes=int(vmem_bytes))


def _rms(x, g, eps):
    return x * lax.rsqrt(jnp.mean(x * x, axis=-1, keepdims=True) + eps) * g


FFN_ROWS = 1024
FFN_COLS = 512


def _ffn_body(h_hbm, g_ref, wg_ref, wu_ref, wo_ref, *rest, final):
    if final:
        fg_ref, o_ref, xn_ref, x_buf, x_sem = rest
    else:
        o_ref, xn_ref, x_buf, x_sem = rest
    i, j = pl.program_id(0), pl.program_id(1)
    rows = x_buf.shape[0]

    def x_copy(tile):
        return pltpu.make_async_copy(h_hbm.at[pl.ds(tile * rows, rows), :], x_buf, x_sem)

    @pl.when(j == 0)
    def _():
        @pl.when(i == 0)
        def _():
            x_copy(i).start()

        x_copy(i).wait()
        x = x_buf[...]
        xn_ref[...] = _rms(x, g_ref[...], NORM_EPS).astype(BF16)
        o_ref[...] = x

    @pl.when(jnp.logical_and(j == 1, i + 1 < pl.num_programs(0)))
    def _():
        x_copy(i + 1).start()

    xn = xn_ref[...]
    gate = jnp.dot(xn, wg_ref[...].astype(BF16), preferred_element_type=F32)
    up = jnp.dot(xn, wu_ref[...].astype(BF16), preferred_element_type=F32)
    act = (0.5 * gate / (1.0 + jnp.exp(-gate))) * up
    o_ref[...] += jnp.dot(act.astype(BF16), wo_ref[...].astype(BF16), preferred_element_type=F32)

    if final:
        @pl.when(j == pl.num_programs(1) - 1)
        def _():
            o_ref[...] = _rms(o_ref[...], fg_ref[...], NORM_EPS)


def _ffn(h, g, w_in, w_out, layer, final_g=None):
    n, d = h.shape
    d_ff = w_out.shape[1]
    tm, tf = FFN_ROWS, FFN_COLS
    nff = d_ff // tf
    assert n % tm == 0 and d_ff % tf == 0 and nff >= 2
    final = final_g is not None
    in_specs = [
        pl.BlockSpec(memory_space=pl.ANY),
        pl.BlockSpec((1, d), lambda i, j: (0, 0)),
        pl.BlockSpec((None, d, tf), lambda i, j: (layer, 0, j)),
        pl.BlockSpec((None, d, tf), lambda i, j: (layer, 0, nff + j)),
        pl.BlockSpec((None, tf, d), lambda i, j: (layer, j, 0)),
    ]
    args = [h, g.reshape(1, d), w_in, w_in, w_out]
    if final:
        in_specs.append(pl.BlockSpec((1, d), lambda i, j: (0, 0)))
        args.append(final_g.reshape(1, d))
    vmem = 3 * tm * d * 4 + tm * d * 2 + 2 * (2 * d * tf + tf * d) * 4 + (8 << 20)
    return pl.pallas_call(
        functools.partial(_ffn_body, final=final),
        grid=(n // tm, nff),
        in_specs=in_specs,
        out_specs=pl.BlockSpec((tm, d), lambda i, j: (i, 0)),
        out_shape=jax.ShapeDtypeStruct((n, d), F32),
        scratch_shapes=[pltpu.VMEM((tm, d), BF16), pltpu.VMEM((tm, d), F32), pltpu.SemaphoreType.DMA(())],
        compiler_params=_params(("arbitrary", "arbitrary"), vmem),
        name="ffn",
    )(*args)


PROJ_ROWS = 1024
PROJ_COLS = 1024


def _proj_body(x_ref, g_ref, w_ref, cos_ref, sin_ref, o_ref, xn_ref, *, q_tiles, q_scale):
    j = pl.program_id(1)

    @pl.when(j == 0)
    def _():
        xn_ref[...] = _rms(x_ref[...], g_ref[...], NORM_EPS).astype(BF16)

    y = jnp.dot(xn_ref[...], w_ref[...], preferred_element_type=F32)
    scale = jnp.where(j < q_tiles, q_scale, 1.0).astype(F32)
    cos = cos_ref[...] * scale
    sin = sin_ref[...] * scale
    for c in range(y.shape[1] // HEAD_DIM):
        yh = y[:, c * HEAD_DIM:(c + 1) * HEAD_DIM]
        rot = yh * cos + pltpu.roll(yh, HALF_DIM, 1) * sin
        o_ref[:, c * HEAD_DIM:(c + 1) * HEAD_DIM] = rot.astype(o_ref.dtype)


def _proj(h, g, w, cos, sin, col0, q_width, rope_width, width, q_scale, out_dtype):
    n, d = h.shape
    seq = cos.shape[1]
    tm, tn = PROJ_ROWS, PROJ_COLS
    assert n % tm == 0 and seq % tm == 0
    assert all(c % tn == 0 for c in (col0, q_width, rope_width, width))
    pos_blocks = seq // tm
    rope_tiles = rope_width // tn
    body = functools.partial(_proj_body, q_tiles=q_width // tn, q_scale=q_scale)

    def table_spec():
        return pl.BlockSpec((None, tm, HEAD_DIM),
                            lambda i, j: (jnp.where(j < rope_tiles, 0, 1), i % pos_blocks, 0))

    vmem = 2 * tm * d * 4 + tm * d * 2 + 2 * d * tn * 2 + 4 * tm * tn * 4 + 4 * tm * HEAD_DIM * 4 + (8 << 20)
    return pl.pallas_call(
        body,
        grid=(n // tm, width // tn),
        in_specs=[
            pl.BlockSpec((tm, d), lambda i, j: (i, 0)),
            pl.BlockSpec((1, d), lambda i, j: (0, 0)),
            pl.BlockSpec((d, tn), lambda i, j: (0, col0 // tn + j)),
            table_spec(),
            table_spec(),
        ],
        out_specs=pl.BlockSpec((tm, tn), lambda i, j: (i, j)),
        out_shape=jax.ShapeDtypeStruct((n, width), out_dtype),
        scratch_shapes=[pltpu.VMEM((tm, d), BF16)],
        compiler_params=_params(("parallel", "arbitrary"), vmem),
        name="proj",
    )(h, g.reshape(1, d), w, cos, sin)


def _rope_tables(seq):
    inv_freq = ROPE_THETA ** (-jnp.arange(HALF_DIM, dtype=F32) / HALF_DIM)
    ang = jnp.arange(seq, dtype=F32)[:, None] * inv_freq[None, :]
    cos, sin = jnp.cos(ang), jnp.sin(ang)
    cos = jnp.concatenate([cos, cos], axis=-1)
    sin = jnp.concatenate([-sin, sin], axis=-1)
    return jnp.stack([cos, jnp.ones_like(cos)]), jnp.stack([sin, jnp.zeros_like(sin)])


OPROJ_ROWS = 1024
OPROJ_COLS = 1024


def _oproj_body(*refs):
    *aw, h_ref, o_ref = refs
    acc = h_ref[...]
    for a_ref, w_ref in zip(aw[0::2], aw[1::2]):
        acc = acc + jnp.dot(a_ref[...], w_ref[...], preferred_element_type=F32)
    o_ref[...] = acc


def _oproj(h, parts, w):
    n, d = h.shape
    tm, tn = OPROJ_ROWS, OPROJ_COLS
    assert n % tm == 0 and d % tn == 0
    in_specs, args, vmem, row0 = [], [], 0, 0
    for a in parts:
        kdim = a.shape[1]
        assert row0 % kdim == 0
        in_specs += [pl.BlockSpec((tm, kdim), lambda i, j: (i, 0)),
                     pl.BlockSpec((kdim, tn), lambda i, j, r=row0 // kdim: (r, j))]
        args += [a, w]
        vmem += 2 * (tm * kdim + kdim * tn) * 2
        row0 += kdim
    assert row0 == w.shape[0]
    in_specs.append(pl.BlockSpec((tm, tn), lambda i, j: (i, j)))
    args.append(h)
    vmem += 6 * tm * tn * 4 + (8 << 20)
    return pl.pallas_call(
        _oproj_body,
        grid=(n // tm, d // tn),
        in_specs=in_specs,
        out_specs=pl.BlockSpec((tm, tn), lambda i, j: (i, j)),
        out_shape=jax.ShapeDtypeStruct((n, d), F32),
        compiler_params=_params(("parallel", "parallel"), vmem),
        name="oproj",
    )(*args)


def _dilated_body(q_ref, kc_ref, kp_ref, vc_ref, vp_ref, o_ref, kbuf, vbuf, *branch_bufs):
    obufs, lbufs = branch_bufs[:3], branch_bufs[3:]
    sb = pl.program_id(2)
    sup = DIL_SUPER
    kbuf[0:sup, :] = kp_ref[...]
    kbuf[sup:2 * sup, :] = kc_ref[...]
    vbuf[0:sup, :] = vp_ref[...]
    vbuf[sup:2 * sup, :] = vc_ref[...]

    row = lax.broadcasted_iota(jnp.int32, (DIL_BLOCK, 2 * DIL_BLOCK), 0)
    col = lax.broadcasted_iota(jnp.int32, (DIL_BLOCK, 2 * DIL_BLOCK), 1)
    band = jnp.logical_and(col >= row + (DIL_BLOCK - DIL_REACH), col <= row + DIL_BLOCK)

    for br, dil in enumerate(DILATIONS):
        nsub = sup // (DIL_BLOCK * dil)
        obuf, lbuf = obufs[br], lbufs[br]

        def block(t, carry, dil=dil, nsub=nsub, obuf=obuf, lbuf=lbuf):
            res = t // nsub
            sub = t % nsub
            q0 = res + dil * DIL_BLOCK * sub
            k0 = sup + q0 - dil * DIL_BLOCK
            if dil == 1:
                q0 = pl.multiple_of(q0, DIL_BLOCK)
                k0 = pl.multiple_of(k0, DIL_BLOCK)
                qs, ks = pl.ds(q0, DIL_BLOCK), pl.ds(k0, 2 * DIL_BLOCK)
            else:
                qs, ks = pl.ds(q0, DIL_BLOCK, stride=dil), pl.ds(k0, 2 * DIL_BLOCK, stride=dil)
            qb = q_ref[qs, :].astype(BF16)
            kb = kbuf[ks, :].astype(BF16)
            vb = vbuf[ks, :].astype(BF16)
            s = lax.dot_general(qb, kb, NT_DIMS, preferred_element_type=F32)
            has_prev = jnp.logical_or(sb > 0, sub > 0)
            first_col = jnp.where(has_prev, 0, DIL_BLOCK)
            s = jnp.where(jnp.logical_and(band, col >= first_col), s, NEG_INF)
            m = jnp.max(s, axis=-1, keepdims=True)
            p = jnp.exp2((s - m).astype(BF16))
            pv = jnp.dot(p, jnp.concatenate([vb, jnp.ones_like(vb)], axis=1), preferred_element_type=F32)
            den = pv[:, HEAD_DIM:]
            obuf[qs, :] = pv[:, :HEAD_DIM] / den
            lbuf[qs, :] = m + jnp.log2(den)
            return carry

        lax.fori_loop(0, nsub * dil, block, 0, unroll=DIL_UNROLL)

    lses = [lbuf[...] for lbuf in lbufs]
    top = jnp.maximum(jnp.maximum(lses[0], lses[1]), lses[2])
    wts = [jnp.exp2(l - top) for l in lses]
    mix = wts[0] * obufs[0][...] + wts[1] * obufs[1][...] + wts[2] * obufs[2][...]
    o_ref[...] = (mix / (wts[0] + wts[1] + wts[2])).astype(o_ref.dtype)


def _dilated(qkv, batch, seq):
    sup = DIL_SUPER
    assert seq % sup == 0
    nsb = seq // sup
    heads = N_HEADS_A

    def cur(col0):
        return pl.BlockSpec((sup, HEAD_DIM), lambda b, h, s: (b * nsb + s, col0 + h))

    def prev(col0):
        return pl.BlockSpec((sup, HEAD_DIM), lambda b, h, s: (b * nsb + jnp.maximum(s - 1, 0), col0 + h))

    blk = sup * HEAD_DIM * 4
    vmem = 2 * 5 * blk + 2 * sup * HEAD_DIM * 2 + 4 * blk + 6 * blk + 8 * blk + (8 << 20)
    return pl.pallas_call(
        _dilated_body,
        grid=(batch, heads, nsb),
        in_specs=[cur(0), cur(heads), prev(heads), cur(2 * heads), prev(2 * heads)],
        out_specs=pl.BlockSpec((sup, HEAD_DIM), lambda b, h, s: (b * nsb + s, h)),
        out_shape=jax.ShapeDtypeStruct((batch * seq, WIDTH_A), BF16),
        scratch_shapes=[pltpu.VMEM((2 * sup, HEAD_DIM), F32)] * 2 + [pltpu.VMEM((sup, HEAD_DIM), F32)] * 6,
        compiler_params=_params(("parallel", "parallel", "arbitrary"), vmem),
        name="dilated",
    )(qkv, qkv, qkv, qkv, qkv)


def _scores_t(k_blk, q):
    return lax.dot_general(k_blk, q, NT_DIMS, preferred_element_type=F32)


def _flash_t(streams, qi, tk, halves, diag_mask, past_mask):
    assert halves in (1, 2)
    tq = halves * tk
    first = halves * qi

    def key_tile(k_ref, j):
        return k_ref[pl.ds(pl.multiple_of(j * tk, tk), tk), :]

    for q, k_ref, _, p_ref, acc_ref, st_ref in streams:
        s = jnp.concatenate(
            [diag_mask(_scores_t(key_tile(k_ref, first + half), q[half * tk:(half + 1) * tk, :]), half)
             for half in range(halves)], axis=1).astype(BF16)
        m = jnp.max(s, axis=0, keepdims=True)
        p_ref[0] = jnp.exp2(s - m)
        acc_ref[...] = jnp.zeros(acc_ref.shape, F32)
        st_ref[0] = m.astype(F32)
        st_ref[1] = jnp.ones(m.shape, F32)

    def flush_own():
        for _, _, vt_ref, p_ref, acc_ref, st_ref in streams:
            pv = jnp.concatenate(
                [jnp.dot(vt_ref[first + half], p_ref[0, :, half * tk:(half + 1) * tk],
                         preferred_element_type=F32) for half in range(halves)], axis=1)
            acc_ref[...] = st_ref[1] * acc_ref[...] + pv

    def flush(pending, slot):
        for _, _, vt_ref, p_ref, acc_ref, st_ref in streams:
            acc_ref[...] = st_ref[1] * acc_ref[...] + jnp.dot(vt_ref[pending], p_ref[slot],
                                                              preferred_element_type=F32)

    def step(j, do_flush, wr, mask):
        scores = [_scores_t(key_tile(st[1], j), st[0]) for st in streams]
        do_flush()
        for s, (_, _, _, p_ref, _, st_ref) in zip(scores, streams):
            s = mask(s, j).astype(BF16)
            m = st_ref[0]
            m_new = jnp.maximum(m, jnp.max(s, axis=0, keepdims=True).astype(F32))
            p_ref[wr] = jnp.exp2(s - m_new.astype(BF16))
            st_ref[0] = m_new
            st_ref[1] = jnp.exp2(m - m_new)

    if halves == 2:
        def second_half_only(s, j):
            qry = lax.broadcasted_iota(jnp.int32, (tk, tq), 1)
            return jnp.where(qry >= tk, past_mask(s, j), NEG_INF)

        step(first, flush_own, 1, second_half_only)

        def pair(i, carry):
            step(2 * i, lambda: flush(jnp.where(i == 0, first, 2 * i - 1), 1), 0, past_mask)
            step(2 * i + 1, lambda: flush(2 * i, 0), 1, past_mask)
            return carry

        lax.fori_loop(0, qi, pair, 0)
        flush(jnp.where(qi == 0, first, first - 1), 1)
    else:
        def pair(i, carry):
            step(2 * i, lambda: flush(jnp.where(i == 0, first, 2 * i - 1), 0), 1, past_mask)
            step(2 * i + 1, lambda: flush(2 * i, 1), 0, past_mask)
            return carry

        lax.fori_loop(0, qi // 2, pair, 0)
        odd = qi % 2 == 1
        last = jnp.maximum(qi - 1, 0)

        @pl.when(odd)
        def _():
            step(last, lambda: flush(jnp.where(qi == 1, first, qi - 2), 0), 1, past_mask)
            flush(last, 1)

        @pl.when(jnp.logical_not(odd))
        def _():
            flush(last, 0)

    outs = []
    for _, _, _, _, acc_ref, _ in streams:
        dv = acc_ref.shape[0] - ONES_ROWS
        outs.append((acc_ref[0:dv, :], acc_ref[dv:dv + 1, :]))
    return outs


def _store_transposed_blocks(v_ref, vt_ref):
    nblk, rows_t, rows = vt_ref.shape
    dv = rows_t - ONES_ROWS
    for n in range(nblk):
        vt_ref[n, 0:dv, :] = v_ref[n * rows:(n + 1) * rows, :].astype(F32).T.astype(vt_ref.dtype)
        vt_ref[n, dv:rows_t, :] = jnp.ones((ONES_ROWS, rows), vt_ref.dtype)


def _moba_body(q_ref, k_ref, v_ref, o_ref, kmean_hi, kmean_lo, vt_ref, sel_ref, p_ref, acc_ref, st_ref):
    qi = pl.program_id(2)
    blk, tk, tq = MOBA_BLOCK, ATTN_KEYS, MOBA_HALVES * ATTN_KEYS
    nb = k_ref.shape[0] // blk

    @pl.when(qi == 0)
    def _():
        for n in range(nb):
            mean = jnp.mean(k_ref[n * blk:(n + 1) * blk, :].astype(F32), axis=0, keepdims=True)
            hi = mean.astype(BF16)
            kmean_hi[n:n + 1, :] = hi
            kmean_lo[n:n + 1, :] = (mean - hi.astype(F32)).astype(BF16)
        _store_transposed_blocks(v_ref, vt_ref)

    q = q_ref[...]
    gate = _scores_t(kmean_hi[...], q) + _scores_t(kmean_lo[...], q)
    blk_id = lax.broadcasted_iota(jnp.int32, gate.shape, 0)
    q_blk = (tq // blk) * qi + lax.broadcasted_iota(jnp.int32, gate.shape, 1) // blk
    past = blk_id < q_blk
    work = jnp.where(past, gate, NEG_INF)
    chosen = jnp.zeros(gate.shape, F32)
    for _ in range(MOBA_TOPK):
        best = jnp.max(work, axis=0, keepdims=True)
        first = jnp.min(jnp.where(work == best, blk_id, nb), axis=0, keepdims=True)
        pick = blk_id == first
        chosen = jnp.where(pick, 1.0, chosen)
        work = jnp.where(pick, -jnp.inf, work)
    sel_ref[...] = jnp.where(past, chosen, 0.0)

    def picked(block):
        return sel_ref[pl.ds(block, 1), :] > 0.5

    def past_mask(s, j):
        return jnp.concatenate([jnp.where(picked(2 * j), s[:blk], NEG_INF),
                                jnp.where(picked(2 * j + 1), s[blk:], NEG_INF)], axis=0)

    def diag_mask(s, half):
        key = lax.broadcasted_iota(jnp.int32, (tk, tk), 0)
        qry = lax.broadcasted_iota(jnp.int32, (tk, tk), 1)
        first_picked = sel_ref[pl.ds(2 * (MOBA_HALVES * qi + half), 1), half * tk:(half + 1) * tk] > 0.5
        other = jnp.where(key < blk, jnp.where(first_picked, s, NEG_INF), NEG_INF)
        own_start = (qry // blk) * blk
        return jnp.where(key <= qry, jnp.where(key >= own_start, s, other), other)

    [(acc, l)] = _flash_t([(q, k_ref, vt_ref, p_ref, acc_ref, st_ref)], qi, tk, MOBA_HALVES, diag_mask, past_mask)
    o_ref[...] = (acc * (1.0 / l)).T.astype(o_ref.dtype)


def _moba(qkv, batch, seq):
    blk, tk, tq = MOBA_BLOCK, ATTN_KEYS, MOBA_HALVES * ATTN_KEYS
    assert seq % tq == 0
    nq = seq // tq
    nb = seq // blk
    heads = N_HEADS_B
    vmem = (2 * 2 * seq * HEAD_DIM * 2 + seq * HEAD_DIM * 2 + 4 * tq * HEAD_DIM * 2 + 2 * tk * tq * 2
            + 2 * HEAD_DIM * tq * 4 + 10 * tk * tq * 4 + (8 << 20))
    return pl.pallas_call(
        _moba_body,
        grid=(batch, heads, nq),
        in_specs=[
            pl.BlockSpec((tq, HEAD_DIM), lambda b, h, i: (b * nq + i, h)),
            pl.BlockSpec((seq, HEAD_DIM), lambda b, h, i: (b, heads + h)),
            pl.BlockSpec((seq, HEAD_DIM), lambda b, h, i: (b, 2 * heads + h)),
        ],
        out_specs=pl.BlockSpec((tq, HEAD_DIM), lambda b, h, i: (b * nq + i, h)),
        out_shape=jax.ShapeDtypeStruct((batch * seq, WIDTH_B), BF16),
        scratch_shapes=[pltpu.VMEM((nb, HEAD_DIM), BF16)] * 2
        + [pltpu.VMEM((seq // tk, HEAD_DIM + ONES_ROWS, tk), BF16), pltpu.VMEM((nb, tq), F32),
           pltpu.VMEM((2, tk, tq), BF16), pltpu.VMEM((HEAD_DIM + ONES_ROWS, tq), F32),
           pltpu.VMEM((2, 1, tq), F32)],
        compiler_params=_params(("parallel", "parallel", "arbitrary"), vmem),
        name="moba",
    )(qkv, qkv, qkv)


def _diff_body(lq1_ref, lk1_ref, lq2_ref, lk2_ref, g_ref, q1_ref, q2_ref, k1_ref, k2_ref, v_ref, o_ref,
               vt_ref, p1_ref, p2_ref, acc1_ref, acc2_ref, st1_ref, st2_ref, *, lambda_init):
    qi = pl.program_id(2)
    t = ATTN_KEYS

    @pl.when(qi == 0)
    def _():
        _store_transposed_blocks(v_ref, vt_ref)

    def diag_mask(s, half):
        key = lax.broadcasted_iota(jnp.int32, (t, t), 0)
        qry = lax.broadcasted_iota(jnp.int32, (t, t), 1)
        return jnp.where(key <= qry, s, NEG_INF)

    streams = [(q1_ref[...], k1_ref, vt_ref, p1_ref, acc1_ref, st1_ref),
               (q2_ref[...], k2_ref, vt_ref, p2_ref, acc2_ref, st2_ref)]
    (a1, l1), (a2, l2) = _flash_t(streams, qi, t, DIFF_HALVES, diag_mask, lambda s, j: s)

    lam = (jnp.exp(jnp.sum(lq1_ref[...] * lk1_ref[...], axis=-1, keepdims=True))
           - jnp.exp(jnp.sum(lq2_ref[...] * lk2_ref[...], axis=-1, keepdims=True)) + lambda_init)
    o = a1 * (1.0 / l1) - lam * (a2 * (1.0 / l2))
    o = o * lax.rsqrt(jnp.mean(o * o, axis=0, keepdims=True) + SUBLN_EPS)
    o_ref[...] = (o.T * g_ref[...] * (1.0 - lambda_init)).astype(o_ref.dtype)


def _diff(qkv, lq1, lk1, lq2, lk2, subln_g, lambda_init, batch, seq):
    t, tq = ATTN_KEYS, DIFF_HALVES * ATTN_KEYS
    assert seq % tq == 0
    nq = seq // tq
    heads = N_HEADS_C
    dv = 2 * HEAD_DIM
    kcol = D_MODEL // HEAD_DIM
    vcol = 2 * D_MODEL // dv

    def vec(width):
        return pl.BlockSpec((1, width), lambda b, h, i: (0, 0))

    def q_spec(part):
        return pl.BlockSpec((tq, HEAD_DIM), lambda b, h, i: (b * nq + i, 2 * h + part))

    def k_spec(part):
        return pl.BlockSpec((seq, HEAD_DIM), lambda b, h, i: (b, kcol + 2 * h + part))

    vmem = (2 * (2 * seq * HEAD_DIM + seq * dv) * 2 + seq * dv * 2 + 8 * tq * HEAD_DIM * 2 + 2 * tq * dv * 2
            + 2 * tq * dv * 4 + 4 * t * tq * 2 + 10 * t * tq * 4 + (8 << 20))
    return pl.pallas_call(
        functools.partial(_diff_body, lambda_init=lambda_init),
        grid=(batch, heads, nq),
        in_specs=[vec(HEAD_DIM)] * 4 + [vec(dv), q_spec(0), q_spec(1), k_spec(0), k_spec(1),
                                        pl.BlockSpec((seq, dv), lambda b, h, i: (b, vcol + h))],
        out_specs=pl.BlockSpec((tq, dv), lambda b, h, i: (b * nq + i, h)),
        out_shape=jax.ShapeDtypeStruct((batch * seq, heads * dv), BF16),
        scratch_shapes=[pltpu.VMEM((seq // t, dv + ONES_ROWS, t), BF16)] + [pltpu.VMEM((2, t, tq), BF16)] * 2
        + [pltpu.VMEM((dv + ONES_ROWS, tq), F32)] * 2 + [pltpu.VMEM((2, 1, tq), F32)] * 2,
        compiler_params=_params(("parallel", "parallel", "arbitrary"), vmem),
        name="diffattn",
    )(lq1.reshape(1, -1), lk1.reshape(1, -1), lq2.reshape(1, -1), lk2.reshape(1, -1),
      subln_g.reshape(1, -1), qkv, qkv, qkv, qkv, qkv)


def _lambda_init(layer):
    return 0.8 - 0.6 * math.exp(-0.3 * layer)


def kernel(x, ffa_norm, ffa_w_in, ffa_w_out, mix_norm, even_w_in, even_w_out, odd_w_in, odd_w_out,
           lambda_q1, lambda_k1, lambda_q2, lambda_k2, subln_norm, ffb_norm, ffb_w_in, ffb_w_out, final_norm):
    batch, seq, d = x.shape
    depth = ffa_norm.shape[0]
    cos, sin = _rope_tables(seq)
    h = x.reshape(batch * seq, d)
    for layer in range(depth):
        i = layer // 2
        h = _ffn(h, ffa_norm[layer], ffa_w_in, ffa_w_out, layer)
        if layer % 2 == 0:
            w_in = even_w_in[i].astype(BF16)
            wa, wb = 3 * WIDTH_A, 3 * WIDTH_B
            qkv_a = _proj(h, mix_norm[layer], w_in, cos, sin, 0, WIDTH_A, 2 * WIDTH_A, wa, SCALE * LOG2_E, F32)
            qkv_b = _proj(h, mix_norm[layer], w_in, cos, sin, wa, WIDTH_B, 2 * WIDTH_B, wb, SCALE * LOG2_E, BF16)
            o_a = _dilated(qkv_a, batch, seq)
            o_b = _moba(qkv_b, batch, seq)
            h = _oproj(h, [o_a, o_b], even_w_out[i].astype(BF16))
        else:
            qkv = _proj(h, mix_norm[layer], odd_w_in[i].astype(BF16), cos, sin,
                        0, D_MODEL, 2 * D_MODEL, 3 * D_MODEL, SCALE * LOG2_E, BF16)
            o = _diff(qkv, lambda_q1[i], lambda_k1[i], lambda_q2[i], lambda_k2[i], subln_norm[i],
                      _lambda_init(layer), batch, seq)
            h = _oproj(h, [o], odd_w_out[i].astype(BF16))
        last = layer == depth - 1
        h = _ffn(h, ffb_norm[layer], ffb_w_in, ffb_w_out, layer, final_norm if last else None)
    return h.reshape(batch, seq, d)
```

```python
import functools
import math

import jax
import jax.numpy as jnp
from jax import lax
from jax.experimental import pallas as pl
from jax.experimental.pallas import tpu as pltpu

F32 = jnp.float32
BF16 = jnp.bfloat16

D_MODEL = 2048
HEAD_DIM = 128
HALF_DIM = HEAD_DIM // 2
N_HEADS_A = 8
N_HEADS_B = 8
N_HEADS_C = 8
WIDTH_A = N_HEADS_A * HEAD_DIM
WIDTH_B = N_HEADS_B * HEAD_DIM
DILATIONS = (1, 4, 16)
DIL_BLOCK = 128
DIL_REACH = 128
DIL_SUPER = DIL_BLOCK * DILATIONS[-1]
DIL_UNROLL = 16
MOBA_BLOCK = 256
MOBA_TOPK = 3
ATTN_KEYS = 512
MOBA_HALVES = 2
DIFF_HALVES = 1
ROPE_THETA = 10000.0
NORM_EPS = 1e-6
SUBLN_EPS = 1e-5
NEG_INF = -1e30
SCALE = HEAD_DIM ** -0.5
LOG2_E = math.log2(math.e)

VMEM_V7X_BYTES = 64 * 1024 * 1024
NT_DIMS = (((1,), (1,)), ((), ()))
ONES_ROWS = 16
PAST_UNROLL = 4


def _params(semantics, vmem_bytes):
    assert vmem_bytes < VMEM_V7X_BYTES
    return pltpu.CompilerParams(dimension_semantics=semantics, vmem_limit_bytes=int(vmem_bytes))


def _rms(x, g, eps):
    return x * lax.rsqrt(jnp.mean(x * x, axis=-1, keepdims=True) + eps) * g


FFN_ROWS = 1024
FFN_COLS = 512


def _ffn_body(h_hbm, g_ref, wg_ref, wu_ref, wo_ref, *rest, final):
    if final:
        fg_ref, o_ref, xn_ref, x_buf, x_sem = rest
    else:
        o_ref, xn_ref, x_buf, x_sem = rest
    i, j = pl.program_id(0), pl.program_id(1)
    rows = x_buf.shape[0]

    def x_copy(tile):
        return pltpu.make_async_copy(h_hbm.at[pl.ds(tile * rows, rows), :], x_buf, x_sem)

    @pl.when(j == 0)
    def _():
        @pl.when(i == 0)
        def _():
            x_copy(i).start()

        x_copy(i).wait()
        x = x_buf[...]
        xn_ref[...] = _rms(x, g_ref[...], NORM_EPS).astype(BF16)
        o_ref[...] = x

    @pl.when(jnp.logical_and(j == 1, i + 1 < pl.num_programs(0)))
    def _():
        x_copy(i + 1).start()

    xn = xn_ref[...]
    gate = jnp.dot(xn, wg_ref[...].astype(BF16), preferred_element_type=F32)
    up = jnp.dot(xn, wu_ref[...].astype(BF16), preferred_element_type=F32)
    act = (0.5 * gate / (1.0 + jnp.exp(-gate))) * up
    o_ref[...] += jnp.dot(act.astype(BF16), wo_ref[...].astype(BF16), preferred_element_type=F32)

    if final:
        @pl.when(j == pl.num_programs(1) - 1)
        def _():
            o_ref[...] = _rms(o_ref[...], fg_ref[...], NORM_EPS)


def _ffn(h, g, w_in, w_out, layer, final_g=None):
    n, d = h.shape
    d_ff = w_out.shape[1]
    tm, tf = FFN_ROWS, FFN_COLS
    nff = d_ff // tf
    assert n % tm == 0 and d_ff % tf == 0 and nff >= 2
    final = final_g is not None
    in_specs = [
        pl.BlockSpec(memory_space=pl.ANY),
        pl.BlockSpec((1, d), lambda i, j: (0, 0)),
        pl.BlockSpec((None, d, tf), lambda i, j: (layer, 0, j)),
        pl.BlockSpec((None, d, tf), lambda i, j: (layer, 0, nff + j)),
        pl.BlockSpec((None, tf, d), lambda i, j: (layer, j, 0)),
    ]
    args = [h, g.reshape(1, d), w_in, w_in, w_out]
    if final:
        in_specs.append(pl.BlockSpec((1, d), lambda i, j: (0, 0)))
        args.append(final_g.reshape(1, d))
    vmem = 3 * tm * d * 4 + tm * d * 2 + 2 * (2 * d * tf + tf * d) * 4 + (8 << 20)
    return pl.pallas_call(
        functools.partial(_ffn_body, final=final),
        grid=(n // tm, nff),
        in_specs=in_specs,
        out_specs=pl.BlockSpec((tm, d), lambda i, j: (i, 0)),
        out_shape=jax.ShapeDtypeStruct((n, d), F32),
        scratch_shapes=[pltpu.VMEM((tm, d), BF16), pltpu.VMEM((tm, d), F32), pltpu.SemaphoreType.DMA(())],
        compiler_params=_params(("arbitrary", "arbitrary"), vmem),
        name="ffn",
    )(*args)


PROJ_ROWS = 1024
PROJ_COLS = 1024


def _proj_body(x_ref, g_ref, w_ref, cos_ref, sin_ref, o_ref, xn_ref, *, q_tiles, q_scale):
    j = pl.program_id(1)

    @pl.when(j == 0)
    def _():
        xn_ref[...] = _rms(x_ref[...], g_ref[...], NORM_EPS).astype(BF16)

    y = jnp.dot(xn_ref[...], w_ref[...], preferred_element_type=F32)
    scale = jnp.where(j < q_tiles, q_scale, 1.0).astype(F32)
    cos = cos_ref[...] * scale
    sin = sin_ref[...] * scale
    for c in range(y.shape[1] // HEAD_DIM):
        yh = y[:, c * HEAD_DIM:(c + 1) * HEAD_DIM]
        rot = yh * cos + pltpu.roll(yh, HALF_DIM, 1) * sin
        o_ref[:, c * HEAD_DIM:(c + 1) * HEAD_DIM] = rot.astype(o_ref.dtype)


def _proj(h, g, w, cos, sin, col0, q_width, rope_width, width, q_scale, out_dtype):
    n, d = h.shape
    seq = cos.shape[1]
    tm, tn = PROJ_ROWS, PROJ_COLS
    assert n % tm == 0 and seq % tm == 0
    assert all(c % tn == 0 for c in (col0, q_width, rope_width, width))
    pos_blocks = seq // tm
    rope_tiles = rope_width // tn
    body = functools.partial(_proj_body, q_tiles=q_width // tn, q_scale=q_scale)

    def table_spec():
        return pl.BlockSpec((None, tm, HEAD_DIM),
                            lambda i, j: (jnp.where(j < rope_tiles, 0, 1), i % pos_blocks, 0))

    vmem = 2 * tm * d * 4 + tm * d * 2 + 2 * d * tn * 2 + 4 * tm * tn * 4 + 4 * tm * HEAD_DIM * 4 + (8 << 20)
    return pl.pallas_call(
        body,
        grid=(n // tm, width // tn),
        in_specs=[
            pl.BlockSpec((tm, d), lambda i, j: (i, 0)),
            pl.BlockSpec((1, d), lambda i, j: (0, 0)),
            pl.BlockSpec((d, tn), lambda i, j: (0, col0 // tn + j)),
            table_spec(),
            table_spec(),
        ],
        out_specs=pl.BlockSpec((tm, tn), lambda i, j: (i, j)),
        out_shape=jax.ShapeDtypeStruct((n, width), out_dtype),
        scratch_shapes=[pltpu.VMEM((tm, d), BF16)],
        compiler_params=_params(("parallel", "arbitrary"), vmem),
        name="proj",
    )(h, g.reshape(1, d), w, cos, sin)


def _rope_tables(seq):
    inv_freq = ROPE_THETA ** (-jnp.arange(HALF_DIM, dtype=F32) / HALF_DIM)
    ang = jnp.arange(seq, dtype=F32)[:, None] * inv_freq[None, :]
    cos, sin = jnp.cos(ang), jnp.sin(ang)
    cos = jnp.concatenate([cos, cos], axis=-1)
    sin = jnp.concatenate([-sin, sin], axis=-1)
    return jnp.stack([cos, jnp.ones_like(cos)]), jnp.stack([sin, jnp.zeros_like(sin)])


OPROJ_ROWS = 1024
OPROJ_COLS = 1024


def _oproj_body(*refs):
    *aw, h_ref, o_ref = refs
    acc = h_ref[...]
    for a_ref, w_ref in zip(aw[0::2], aw[1::2]):
        acc = acc + jnp.dot(a_ref[...], w_ref[...], preferred_element_type=F32)
    o_ref[...] = acc


def _oproj(h, parts, w):
    n, d = h.shape
    tm, tn = OPROJ_ROWS, OPROJ_COLS
    assert n % tm == 0 and d % tn == 0
    in_specs, args, vmem, row0 = [], [], 0, 0
    for a in parts:
        kdim = a.shape[1]
        assert row0 % kdim == 0
        in_specs += [pl.BlockSpec((tm, kdim), lambda i, j: (i, 0)),
                     pl.BlockSpec((kdim, tn), lambda i, j, r=row0 // kdim: (r, j))]
        args += [a, w]
        vmem += 2 * (tm * kdim + kdim * tn) * 2
        row0 += kdim
    assert row0 == w.shape[0]
    in_specs.append(pl.BlockSpec((tm, tn), lambda i, j: (i, j)))
    args.append(h)
    vmem += 6 * tm * tn * 4 + (8 << 20)
    return pl.pallas_call(
        _oproj_body,
        grid=(n // tm, d // tn),
        in_specs=in_specs,
        out_specs=pl.BlockSpec((tm, tn), lambda i, j: (i, j)),
        out_shape=jax.ShapeDtypeStruct((n, d), F32),
        compiler_params=_params(("parallel", "parallel"), vmem),
        name="oproj",
    )(*args)


def _dilated_body(q_ref, kc_ref, kp_ref, vc_ref, vp_ref, o_ref, kbuf, vbuf, *branch_bufs):
    obufs, lbufs = branch_bufs[:3], branch_bufs[3:]
    sb = pl.program_id(2)
    sup = DIL_SUPER
    kbuf[0:sup, :] = kp_ref[...]
    kbuf[sup:2 * sup, :] = kc_ref[...]
    vbuf[0:sup, :] = vp_ref[...]
    vbuf[sup:2 * sup, :] = vc_ref[...]

    row = lax.broadcasted_iota(jnp.int32, (DIL_BLOCK, 2 * DIL_BLOCK), 0)
    col = lax.broadcasted_iota(jnp.int32, (DIL_BLOCK, 2 * DIL_BLOCK), 1)
    band = jnp.logical_and(col >= row + (DIL_BLOCK - DIL_REACH), col <= row + DIL_BLOCK)

    for br, dil in enumerate(DILATIONS):
        nsub = sup // (DIL_BLOCK * dil)
        obuf, lbuf = obufs[br], lbufs[br]

        def block(t, carry, dil=dil, nsub=nsub, obuf=obuf, lbuf=lbuf):
            res = t // nsub
            sub = t % nsub
            q0 = res + dil * DIL_BLOCK * sub
            k0 = sup + q0 - dil * DIL_BLOCK
            if dil == 1:
                q0 = pl.multiple_of(q0, DIL_BLOCK)
                k0 = pl.multiple_of(k0, DIL_BLOCK)
                qs, ks = pl.ds(q0, DIL_BLOCK), pl.ds(k0, 2 * DIL_BLOCK)
            else:
                qs, ks = pl.ds(q0, DIL_BLOCK, stride=dil), pl.ds(k0, 2 * DIL_BLOCK, stride=dil)
            qb = q_ref[qs, :].astype(BF16)
            kb = kbuf[ks, :].astype(BF16)
            vb = vbuf[ks, :].astype(BF16)
            s = lax.dot_general(qb, kb, NT_DIMS, preferred_element_type=F32)
            has_prev = jnp.logical_or(sb > 0, sub > 0)
            first_col = jnp.where(has_prev, 0, DIL_BLOCK)
            s = jnp.where(jnp.logical_and(band, col >= first_col), s, NEG_INF)
            m = jnp.max(s, axis=-1, keepdims=True)
            p = jnp.exp2((s - m).astype(BF16))
            pv = jnp.dot(p, jnp.concatenate([vb, jnp.ones_like(vb)], axis=1), preferred_element_type=F32)
            den = pv[:, HEAD_DIM:]
            obuf[qs, :] = pv[:, :HEAD_DIM] / den
            lbuf[qs, :] = m + jnp.log2(den)
            return carry

        lax.fori_loop(0, nsub * dil, block, 0, unroll=DIL_UNROLL)

    lses = [lbuf[...] for lbuf in lbufs]
    top = jnp.maximum(jnp.maximum(lses[0], lses[1]), lses[2])
    wts = [jnp.exp2(l - top) for l in lses]
    mix = wts[0] * obufs[0][...] + wts[1] * obufs[1][...] + wts[2] * obufs[2][...]
    o_ref[...] = (mix / (wts[0] + wts[1] + wts[2])).astype(o_ref.dtype)


def _dilated(qkv, batch, seq):
    sup = DIL_SUPER
    assert seq % sup == 0
    nsb = seq // sup
    heads = N_HEADS_A

    def cur(col0):
        return pl.BlockSpec((sup, HEAD_DIM), lambda b, h, s: (b * nsb + s, col0 + h))

    def prev(col0):
        return pl.BlockSpec((sup, HEAD_DIM), lambda b, h, s: (b * nsb + jnp.maximum(s - 1, 0), col0 + h))

    blk = sup * HEAD_DIM * 4
    vmem = 2 * 5 * blk + 2 * sup * HEAD_DIM * 2 + 4 * blk + 6 * blk + 8 * blk + (8 << 20)
    return pl.pallas_call(
        _dilated_body,
        grid=(batch, heads, nsb),
        in_specs=[cur(0), cur(heads), prev(heads), cur(2 * heads), prev(2 * heads)],
        out_specs=pl.BlockSpec((sup, HEAD_DIM), lambda b, h, s: (b * nsb + s, h)),
        out_shape=jax.ShapeDtypeStruct((batch * seq, WIDTH_A), BF16),
        scratch_shapes=[pltpu.VMEM((2 * sup, HEAD_DIM), F32)] * 2 + [pltpu.VMEM((sup, HEAD_DIM), F32)] * 6,
        compiler_params=_params(("parallel", "parallel", "arbitrary"), vmem),
        name="dilated",
    )(qkv, qkv, qkv, qkv, qkv)


def _scores_t(k_blk, q):
    return lax.dot_general(k_blk, q, NT_DIMS, preferred_element_type=F32)


def _flash_t(streams, qi, tk, halves, diag_mask, past_mask):
    assert halves in (1, 2)
    tq = halves * tk
    first = halves * qi

    def key_tile(k_ref, j):
        return k_ref[pl.ds(pl.multiple_of(j * tk, tk), tk), :]

    for q, k_ref, _, p_ref, acc_ref, st_ref in streams:
        s = jnp.concatenate(
            [diag_mask(_scores_t(key_tile(k_ref, first + half), q[half * tk:(half + 1) * tk, :]), half)
             for half in range(halves)], axis=1).astype(BF16)
        m = jnp.max(s, axis=0, keepdims=True)
        p_ref[0] = jnp.exp2(s - m)
        acc_ref[...] = jnp.zeros(acc_ref.shape, F32)
        st_ref[0] = m.astype(F32)
        st_ref[1] = jnp.ones(m.shape, F32)

    def flush_own():
        for _, _, vt_ref, p_ref, acc_ref, st_ref in streams:
            pv = jnp.concatenate(
                [jnp.dot(vt_ref[first + half], p_ref[0, :, half * tk:(half + 1) * tk],
                         preferred_element_type=F32) for half in range(halves)], axis=1)
            acc_ref[...] = st_ref[1] * acc_ref[...] + pv

    def flush(pending, slot):
        for _, _, vt_ref, p_ref, acc_ref, st_ref in streams:
            acc_ref[...] = st_ref[1] * acc_ref[...] + jnp.dot(vt_ref[pending], p_ref[slot],
                                                              preferred_element_type=F32)

    def step(j, do_flush, wr, mask):
        scores = [_scores_t(key_tile(st[1], j), st[0]) for st in streams]
        do_flush()
        for s, (_, _, _, p_ref, _, st_ref) in zip(scores, streams):
            s = mask(s, j).astype(BF16)
            m = st_ref[0]
            m_new = jnp.maximum(m, jnp.max(s, axis=0, keepdims=True).astype(F32))
            p_ref[wr] = jnp.exp2(s - m_new.astype(BF16))
            st_ref[0] = m_new
            st_ref[1] = jnp.exp2(m - m_new)

    if halves == 2:
        def second_half_only(s, j):
            qry = lax.broadcasted_iota(jnp.int32, (tk, tq), 1)
            return jnp.where(qry >= tk, past_mask(s, j), NEG_INF)

        step(first, flush_own, 1, second_half_only)
    slot0 = halves - 1
    n_past = halves * qi

    def past_step(j, parity):
        rd = slot0 ^ parity
        step(j, lambda: flush(jnp.where(j == 0, first, j - 1), rd), 1 - rd, past_mask)

    def four(i, carry):
        for u in range(PAST_UNROLL):
            past_step(PAST_UNROLL * i + u, u % 2)
        return carry

    lax.fori_loop(0, n_past // PAST_UNROLL, four, 0)
    rest = n_past % PAST_UNROLL
    base = n_past - rest

    @pl.when(rest >= 2)
    def _():
        past_step(base, 0)
        past_step(base + 1, 1)

    last = jnp.where(n_past == 0, first, n_past - 1)
    if halves == 2:
        flush(last, slot0)
    else:
        @pl.when(rest % 2 == 1)
        def _():
            past_step(n_past - 1, 0)
            flush(last, 1 - slot0)

        @pl.when(rest % 2 == 0)
        def _():
            flush(last, slot0)

    outs = []
    for _, _, _, _, acc_ref, _ in streams:
        dv = acc_ref.shape[0] - ONES_ROWS
        outs.append((acc_ref[0:dv, :], acc_ref[dv:dv + 1, :]))
    return outs


def _store_transposed_blocks(v_ref, vt_ref):
    nblk, rows_t, rows = vt_ref.shape
    dv = rows_t - ONES_ROWS
    for n in range(nblk):
        vt_ref[n, 0:dv, :] = v_ref[n * rows:(n + 1) * rows, :].astype(F32).T.astype(vt_ref.dtype)
        vt_ref[n, dv:rows_t, :] = jnp.ones((ONES_ROWS, rows), vt_ref.dtype)


def _moba_body(q_ref, k_ref, v_ref, o_ref, kmean_hi, kmean_lo, vt_ref, sel_ref, p_ref, acc_ref, st_ref):
    qi = pl.program_id(2)
    blk, tk, tq = MOBA_BLOCK, ATTN_KEYS, MOBA_HALVES * ATTN_KEYS
    nb = k_ref.shape[0] // blk

    @pl.when(qi == 0)
    def _():
        for n in range(nb):
            mean = jnp.mean(k_ref[n * blk:(n + 1) * blk, :].astype(F32), axis=0, keepdims=True)
            hi = mean.astype(BF16)
            kmean_hi[n:n + 1, :] = hi
            kmean_lo[n:n + 1, :] = (mean - hi.astype(F32)).astype(BF16)
        _store_transposed_blocks(v_ref, vt_ref)

    q = q_ref[...]
    gate = _scores_t(kmean_hi[...], q) + _scores_t(kmean_lo[...], q)
    blk_id = lax.broadcasted_iota(jnp.int32, gate.shape, 0)
    q_blk = (tq // blk) * qi + lax.broadcasted_iota(jnp.int32, gate.shape, 1) // blk
    past = blk_id < q_blk
    work = jnp.where(past, gate, NEG_INF)
    chosen = jnp.zeros(gate.shape, F32)
    for _ in range(MOBA_TOPK):
        best = jnp.max(work, axis=0, keepdims=True)
        first = jnp.min(jnp.where(work == best, blk_id, nb), axis=0, keepdims=True)
        pick = blk_id == first
        chosen = jnp.where(pick, 1.0, chosen)
        work = jnp.where(pick, -jnp.inf, work)
    sel_ref[...] = jnp.where(past, chosen, 0.0)

    def picked(block):
        return sel_ref[pl.ds(block, 1), :] > 0.5

    def past_mask(s, j):
        return jnp.concatenate([jnp.where(picked(2 * j), s[:blk], NEG_INF),
                                jnp.where(picked(2 * j + 1), s[blk:], NEG_INF)], axis=0)

    def diag_mask(s, half):
        key = lax.broadcasted_iota(jnp.int32, (tk, tk), 0)
        qry = lax.broadcasted_iota(jnp.int32, (tk, tk), 1)
        first_picked = sel_ref[pl.ds(2 * (MOBA_HALVES * qi + half), 1), half * tk:(half + 1) * tk] > 0.5
        other = jnp.where(key < blk, jnp.where(first_picked, s, NEG_INF), NEG_INF)
        own_start = (qry // blk) * blk
        return jnp.where(key <= qry, jnp.where(key >= own_start, s, other), other)

    [(acc, l)] = _flash_t([(q, k_ref, vt_ref, p_ref, acc_ref, st_ref)], qi, tk, MOBA_HALVES, diag_mask, past_mask)
    o_ref[...] = (acc * (1.0 / l)).T.astype(o_ref.dtype)


def _moba(qkv, batch, seq):
    blk, tk, tq = MOBA_BLOCK, ATTN_KEYS, MOBA_HALVES * ATTN_KEYS
    assert seq % tq == 0
    nq = seq // tq
    nb = seq // blk
    heads = N_HEADS_B
    vmem = (2 * 2 * seq * HEAD_DIM * 2 + seq * HEAD_DIM * 2 + 4 * tq * HEAD_DIM * 2 + 2 * tk * tq * 2
            + 2 * HEAD_DIM * tq * 4 + 10 * tk * tq * 4 + (8 << 20))
    return pl.pallas_call(
        _moba_body,
        grid=(batch, heads, nq),
        in_specs=[
            pl.BlockSpec((tq, HEAD_DIM), lambda b, h, i: (b * nq + i, h)),
            pl.BlockSpec((seq, HEAD_DIM), lambda b, h, i: (b, heads + h)),
            pl.BlockSpec((seq, HEAD_DIM), lambda b, h, i: (b, 2 * heads + h)),
        ],
        out_specs=pl.BlockSpec((tq, HEAD_DIM), lambda b, h, i: (b * nq + i, h)),
        out_shape=jax.ShapeDtypeStruct((batch * seq, WIDTH_B), BF16),
        scratch_shapes=[pltpu.VMEM((nb, HEAD_DIM), BF16)] * 2
        + [pltpu.VMEM((seq // tk, HEAD_DIM + ONES_ROWS, tk), BF16), pltpu.VMEM((nb, tq), F32),
           pltpu.VMEM((2, tk, tq), BF16), pltpu.VMEM((HEAD_DIM + ONES_ROWS, tq), F32),
           pltpu.VMEM((2, 1, tq), F32)],
        compiler_params=_params(("parallel", "parallel", "arbitrary"), vmem),
        name="moba",
    )(qkv, qkv, qkv)


def _diff_body(lq1_ref, lk1_ref, lq2_ref, lk2_ref, g_ref, q1_ref, q2_ref, k1_ref, k2_ref, v_ref, o_ref,
               vt_ref, p1_ref, p2_ref, acc1_ref, acc2_ref, st1_ref, st2_ref, *, lambda_init):
    qi = pl.program_id(2)
    t = ATTN_KEYS

    @pl.when(qi == 0)
    def _():
        _store_transposed_blocks(v_ref, vt_ref)

    def diag_mask(s, half):
        key = lax.broadcasted_iota(jnp.int32, (t, t), 0)
        qry = lax.broadcasted_iota(jnp.int32, (t, t), 1)
        return jnp.where(key <= qry, s, NEG_INF)

    streams = [(q1_ref[...], k1_ref, vt_ref, p1_ref, acc1_ref, st1_ref),
               (q2_ref[...], k2_ref, vt_ref, p2_ref, acc2_ref, st2_ref)]
    (a1, l1), (a2, l2) = _flash_t(streams, qi, t, DIFF_HALVES, diag_mask, lambda s, j: s)

    lam = (jnp.exp(jnp.sum(lq1_ref[...] * lk1_ref[...], axis=-1, keepdims=True))
           - jnp.exp(jnp.sum(lq2_ref[...] * lk2_ref[...], axis=-1, keepdims=True)) + lambda_init)
    o = a1 * (1.0 / l1) - lam * (a2 * (1.0 / l2))
    o = o * lax.rsqrt(jnp.mean(o * o, axis=0, keepdims=True) + SUBLN_EPS)
    o_ref[...] = (o.T * g_ref[...] * (1.0 - lambda_init)).astype(o_ref.dtype)


def _diff(qkv, lq1, lk1, lq2, lk2, subln_g, lambda_init, batch, seq):
    t, tq = ATTN_KEYS, DIFF_HALVES * ATTN_KEYS
    assert seq % tq == 0
    nq = seq // tq
    heads = N_HEADS_C
    dv = 2 * HEAD_DIM
    kcol = D_MODEL // HEAD_DIM
    vcol = 2 * D_MODEL // dv

    def vec(width):
        return pl.BlockSpec((1, width), lambda b, h, i: (0, 0))

    def q_spec(part):
        return pl.BlockSpec((tq, HEAD_DIM), lambda b, h, i: (b * nq + i, 2 * h + part))

    def k_spec(part):
        return pl.BlockSpec((seq, HEAD_DIM), lambda b, h, i: (b, kcol + 2 * h + part))

    vmem = (2 * (2 * seq * HEAD_DIM + seq * dv) * 2 + seq * dv * 2 + 8 * tq * HEAD_DIM * 2 + 2 * tq * dv * 2
            + 2 * tq * dv * 4 + 4 * t * tq * 2 + 10 * t * tq * 4 + (8 << 20))
    return pl.pallas_call(
        functools.partial(_diff_body, lambda_init=lambda_init),
        grid=(batch, heads, nq),
        in_specs=[vec(HEAD_DIM)] * 4 + [vec(dv), q_spec(0), q_spec(1), k_spec(0), k_spec(1),
                                        pl.BlockSpec((seq, dv), lambda b, h, i: (b, vcol + h))],
        out_specs=pl.BlockSpec((tq, dv), lambda b, h, i: (b * nq + i, h)),
        out_shape=jax.ShapeDtypeStruct((batch * seq, heads * dv), BF16),
        scratch_shapes=[pltpu.VMEM((seq // t, dv + ONES_ROWS, t), BF16)] + [pltpu.VMEM((2, t, tq), BF16)] * 2
        + [pltpu.VMEM((dv + ONES_ROWS, tq), F32)] * 2 + [pltpu.VMEM((2, 1, tq), F32)] * 2,
        compiler_params=_params(("parallel", "parallel", "arbitrary"), vmem),
        name="diffattn",
    )(lq1.reshape(1, -1), lk1.reshape(1, -1), lq2.reshape(1, -1), lk2.reshape(1, -1),
      subln_g.reshape(1, -1), qkv, qkv, qkv, qkv, qkv)


def _lambda_init(layer):
    return 0.8 - 0.6 * math.exp(-0.3 * layer)


def kernel(x, ffa_norm, ffa_w_in, ffa_w_out, mix_norm, even_w_in, even_w_out, odd_w_in, odd_w_out,
           lambda_q1, lambda_k1, lambda_q2, lambda_k2, subln_norm, ffb_norm, ffb_w_in, ffb_w_out, final_norm):
    batch, seq, d = x.shape
    depth = ffa_norm.shape[0]
    cos, sin = _rope_tables(seq)
    h = x.reshape(batch * seq, d)
    for layer in range(depth):
        i = layer // 2
        h = _ffn(h, ffa_norm[layer], ffa_w_in, ffa_w_out, layer)
        if layer % 2 == 0:
            w_in = even_w_in[i].astype(BF16)
            wa, wb = 3 * WIDTH_A, 3 * WIDTH_B
            qkv_a = _proj(h, mix_norm[layer], w_in, cos, sin, 0, WIDTH_A, 2 * WIDTH_A, wa, SCALE * LOG2_E, F32)
            qkv_b = _proj(h, mix_norm[layer], w_in, cos, sin, wa, WIDTH_B, 2 * WIDTH_B, wb, SCALE * LOG2_E, BF16)
            o_a = _dilated(qkv_a, batch, seq)
            o_b = _moba(qkv_b, batch, seq)
            h = _oproj(h, [o_a, o_b], even_w_out[i].astype(BF16))
        else:
            qkv = _proj(h, mix_norm[layer], odd_w_in[i].astype(BF16), cos, sin,
                        0, D_MODEL, 2 * D_MODEL, 3 * D_MODEL, SCALE * LOG2_E, BF16)
            o = _diff(qkv, lambda_q1[i], lambda_k1[i], lambda_q2[i], lambda_k2[i], subln_norm[i],
                      _lambda_init(layer), batch, seq)
            h = _oproj(h, [o], odd_w_out[i].astype(BF16))
        last = layer == depth - 1
        h = _ffn(h, ffb_norm[layer], ffb_w_in, ffb_w_out, layer, final_norm if last else None)
    return h.reshape(batch, seq, d)
```

```python
import functools
import math

import jax
import jax.numpy as jnp
from jax import lax
from jax.experimental import pallas as pl
from jax.experimental.pallas import tpu as pltpu

F32 = jnp.float32
BF16 = jnp.bfloat16

D_MODEL = 2048
HEAD_DIM = 128
HALF_DIM = HEAD_DIM // 2
N_HEADS_A = 8
N_HEADS_B = 8
N_HEADS_C = 8
WIDTH_A = N_HEADS_A * HEAD_DIM
WIDTH_B = N_HEADS_B * HEAD_DIM
DILATIONS = (1, 4, 16)
DIL_BLOCK = 128
DIL_REACH = 128
DIL_SUPER = DIL_BLOCK * DILATIONS[-1]
DIL_UNROLL = 16
MOBA_BLOCK = 256
MOBA_TOPK = 3
ATTN_KEYS = 512
MOBA_HALVES = 2
DIFF_HALVES = 1
ROPE_THETA = 10000.0
NORM_EPS = 1e-6
SUBLN_EPS = 1e-5
NEG_INF = -1e30
SCALE = HEAD_DIM ** -0.5
LOG2_E = math.log2(math.e)

VMEM_V7X_BYTES = 64 * 1024 * 1024
NT_DIMS = (((1,), (1,)), ((), ()))
ONES_ROWS = 16
PAST_UNROLL = 4


def _params(semantics, vmem_bytes):
    assert vmem_bytes < VMEM_V7X_BYTES
    return pltpu.CompilerParams(dimension_semantics=semantics, vmem_limit_bytes=int(vmem_bytes))


def _rms(x, g, eps):
    return x * lax.rsqrt(jnp.mean(x * x, axis=-1, keepdims=True) + eps) * g


FFN_ROWS = 1024
FFN_COLS = 512


def _ffn_body(h_hbm, g_ref, wg_ref, wu_ref, wo_ref, *rest, final):
    if final:
        fg_ref, o_ref, xn_ref, x_buf, x_sem = rest
    else:
        o_ref, xn_ref, x_buf, x_sem = rest
    i, j = pl.program_id(0), pl.program_id(1)
    rows = x_buf.shape[0]

    def x_copy(tile):
        return pltpu.make_async_copy(h_hbm.at[pl.ds(tile * rows, rows), :], x_buf, x_sem)

    @pl.when(j == 0)
    def _():
        @pl.when(i == 0)
        def _():
            x_copy(i).start()

        x_copy(i).wait()
        x = x_buf[...]
        xn_ref[...] = _rms(x, g_ref[...], NORM_EPS).astype(BF16)
        o_ref[...] = x

    @pl.when(jnp.logical_and(j == 1, i + 1 < pl.num_programs(0)))
    def _():
        x_copy(i + 1).start()

    xn = xn_ref[...]
    gate = jnp.dot(xn, wg_ref[...].astype(BF16), preferred_element_type=F32)
    up = jnp.dot(xn, wu_ref[...].astype(BF16), preferred_element_type=F32)
    act = (0.5 * gate / (1.0 + jnp.exp(-gate))) * up
    o_ref[...] += jnp.dot(act.astype(BF16), wo_ref[...].astype(BF16), preferred_element_type=F32)

    if final:
        @pl.when(j == pl.num_programs(1) - 1)
        def _():
            o_ref[...] = _rms(o_ref[...], fg_ref[...], NORM_EPS)


def _ffn(h, g, w_in, w_out, layer, final_g=None):
    n, d = h.shape
    d_ff = w_out.shape[1]
    tm, tf = FFN_ROWS, FFN_COLS
    nff = d_ff // tf
    assert n % tm == 0 and d_ff % tf == 0 and nff >= 2
    final = final_g is not None
    in_specs = [
        pl.BlockSpec(memory_space=pl.ANY),
        pl.BlockSpec((1, d), lambda i, j: (0, 0)),
        pl.BlockSpec((None, d, tf), lambda i, j: (layer, 0, j)),
        pl.BlockSpec((None, d, tf), lambda i, j: (layer, 0, nff + j)),
        pl.BlockSpec((None, tf, d), lambda i, j: (layer, j, 0)),
    ]
    args = [h, g.reshape(1, d), w_in, w_in, w_out]
    if final:
        in_specs.append(pl.BlockSpec((1, d), lambda i, j: (0, 0)))
        args.append(final_g.reshape(1, d))
    vmem = 3 * tm * d * 4 + tm * d * 2 + 2 * (2 * d * tf + tf * d) * 4 + (8 << 20)
    return pl.pallas_call(
        functools.partial(_ffn_body, final=final),
        grid=(n // tm, nff),
        in_specs=in_specs,
        out_specs=pl.BlockSpec((tm, d), lambda i, j: (i, 0)),
        out_shape=jax.ShapeDtypeStruct((n, d), F32),
        scratch_shapes=[pltpu.VMEM((tm, d), BF16), pltpu.VMEM((tm, d), F32), pltpu.SemaphoreType.DMA(())],
        compiler_params=_params(("arbitrary", "arbitrary"), vmem),
        name="ffn",
    )(*args)


PROJ_ROWS = 1024
PROJ_COLS = 1024


def _proj_body(x_ref, g_ref, w_ref, cos_ref, sin_ref, o_ref, xn_ref, *, q_tiles, q_scale):
    j = pl.program_id(1)

    @pl.when(j == 0)
    def _():
        xn_ref[...] = _rms(x_ref[...], g_ref[...], NORM_EPS).astype(BF16)

    y = jnp.dot(xn_ref[...], w_ref[...].astype(BF16), preferred_element_type=F32)
    scale = jnp.where(j < q_tiles, q_scale, 1.0).astype(F32)
    cos = cos_ref[...] * scale
    sin = sin_ref[...] * scale
    for c in range(y.shape[1] // HEAD_DIM):
        yh = y[:, c * HEAD_DIM:(c + 1) * HEAD_DIM]
        rot = yh * cos + pltpu.roll(yh, HALF_DIM, 1) * sin
        o_ref[:, c * HEAD_DIM:(c + 1) * HEAD_DIM] = rot.astype(o_ref.dtype)


def _proj(h, g, w, layer, cos, sin, col0, q_width, rope_width, width, q_scale, out_dtype):
    n, d = h.shape
    seq = cos.shape[1]
    tm, tn = PROJ_ROWS, PROJ_COLS
    assert n % tm == 0 and seq % tm == 0
    assert all(c % tn == 0 for c in (col0, q_width, rope_width, width))
    pos_blocks = seq // tm
    rope_tiles = rope_width // tn
    body = functools.partial(_proj_body, q_tiles=q_width // tn, q_scale=q_scale)

    def table_spec():
        return pl.BlockSpec((None, tm, HEAD_DIM),
                            lambda i, j: (jnp.where(j < rope_tiles, 0, 1), i % pos_blocks, 0))

    vmem = 2 * tm * d * 4 + tm * d * 2 + 2 * d * tn * 4 + 3 * tm * tn * 4 + 4 * tm * HEAD_DIM * 4 + (8 << 20)
    return pl.pallas_call(
        body,
        grid=(n // tm, width // tn),
        in_specs=[
            pl.BlockSpec((tm, d), lambda i, j: (i, 0)),
            pl.BlockSpec((1, d), lambda i, j: (0, 0)),
            pl.BlockSpec((None, d, tn), lambda i, j: (layer, 0, col0 // tn + j)),
            table_spec(),
            table_spec(),
        ],
        out_specs=pl.BlockSpec((tm, tn), lambda i, j: (i, j)),
        out_shape=jax.ShapeDtypeStruct((n, width), out_dtype),
        scratch_shapes=[pltpu.VMEM((tm, d), BF16)],
        compiler_params=_params(("parallel", "arbitrary"), vmem),
        name="proj",
    )(h, g.reshape(1, d), w, cos, sin)


def _rope_tables(seq):
    inv_freq = ROPE_THETA ** (-jnp.arange(HALF_DIM, dtype=F32) / HALF_DIM)
    ang = jnp.arange(seq, dtype=F32)[:, None] * inv_freq[None, :]
    cos, sin = jnp.cos(ang), jnp.sin(ang)
    cos = jnp.concatenate([cos, cos], axis=-1)
    sin = jnp.concatenate([-sin, sin], axis=-1)
    return jnp.stack([cos, jnp.ones_like(cos)]), jnp.stack([sin, jnp.zeros_like(sin)])


OPROJ_ROWS = 1024
OPROJ_COLS = 1024


def _oproj_body(*refs):
    *aw, h_ref, o_ref = refs
    acc = h_ref[...]
    for a_ref, w_ref in zip(aw[0::2], aw[1::2]):
        acc = acc + jnp.dot(a_ref[...], w_ref[...].astype(BF16), preferred_element_type=F32)
    o_ref[...] = acc


def _oproj(h, parts, w, layer):
    n, d = h.shape
    tm, tn = OPROJ_ROWS, OPROJ_COLS
    assert n % tm == 0 and d % tn == 0
    in_specs, args, vmem, row0 = [], [], 0, 0
    for a in parts:
        kdim = a.shape[1]
        assert row0 % kdim == 0
        in_specs += [pl.BlockSpec((tm, kdim), lambda i, j: (i, 0)),
                     pl.BlockSpec((None, kdim, tn), lambda i, j, r=row0 // kdim: (layer, r, j))]
        args += [a, w]
        vmem += 2 * (tm * kdim * 2 + kdim * tn * 4)
        row0 += kdim
    assert row0 == w.shape[1]
    in_specs.append(pl.BlockSpec((tm, tn), lambda i, j: (i, j)))
    args.append(h)
    vmem += 6 * tm * tn * 4 + (8 << 20)
    return pl.pallas_call(
        _oproj_body,
        grid=(n // tm, d // tn),
        in_specs=in_specs,
        out_specs=pl.BlockSpec((tm, tn), lambda i, j: (i, j)),
        out_shape=jax.ShapeDtypeStruct((n, d), F32),
        compiler_params=_params(("parallel", "parallel"), vmem),
        name="oproj",
    )(*args)


def _dilated_body(q_ref, kc_ref, kp_ref, vc_ref, vp_ref, o_ref, kbuf, vbuf, *branch_bufs):
    obufs, lbufs = branch_bufs[:3], branch_bufs[3:]
    sb = pl.program_id(2)
    sup = DIL_SUPER
    kbuf[0:sup, :] = kp_ref[...]
    kbuf[sup:2 * sup, :] = kc_ref[...]
    vbuf[0:sup, :] = vp_ref[...]
    vbuf[sup:2 * sup, :] = vc_ref[...]

    row = lax.broadcasted_iota(jnp.int32, (DIL_BLOCK, 2 * DIL_BLOCK), 0)
    col = lax.broadcasted_iota(jnp.int32, (DIL_BLOCK, 2 * DIL_BLOCK), 1)
    band = jnp.logical_and(col >= row + (DIL_BLOCK - DIL_REACH), col <= row + DIL_BLOCK)

    for br, dil in enumerate(DILATIONS):
        nsub = sup // (DIL_BLOCK * dil)
        obuf, lbuf = obufs[br], lbufs[br]

        def block(t, carry, dil=dil, nsub=nsub, obuf=obuf, lbuf=lbuf):
            res = t // nsub
            sub = t % nsub
            q0 = res + dil * DIL_BLOCK * sub
            k0 = sup + q0 - dil * DIL_BLOCK
            if dil == 1:
                q0 = pl.multiple_of(q0, DIL_BLOCK)
                k0 = pl.multiple_of(k0, DIL_BLOCK)
                qs, ks = pl.ds(q0, DIL_BLOCK), pl.ds(k0, 2 * DIL_BLOCK)
            else:
                qs, ks = pl.ds(q0, DIL_BLOCK, stride=dil), pl.ds(k0, 2 * DIL_BLOCK, stride=dil)
            qb = q_ref[qs, :].astype(BF16)
            kb = kbuf[ks, :].astype(BF16)
            vb = vbuf[ks, :].astype(BF16)
            s = lax.dot_general(qb, kb, NT_DIMS, preferred_element_type=F32)
            has_prev = jnp.logical_or(sb > 0, sub > 0)
            first_col = jnp.where(has_prev, 0, DIL_BLOCK)
            s = jnp.where(jnp.logical_and(band, col >= first_col), s, NEG_INF)
            m = jnp.max(s, axis=-1, keepdims=True)
            p = jnp.exp2((s - m).astype(BF16))
            pv = jnp.dot(p, jnp.concatenate([vb, jnp.ones_like(vb)], axis=1), preferred_element_type=F32)
            den = pv[:, HEAD_DIM:]
            obuf[qs, :] = pv[:, :HEAD_DIM] / den
            lbuf[qs, :] = m + jnp.log2(den)
            return carry

        lax.fori_loop(0, nsub * dil, block, 0, unroll=DIL_UNROLL)

    lses = [lbuf[...] for lbuf in lbufs]
    top = jnp.maximum(jnp.maximum(lses[0], lses[1]), lses[2])
    wts = [jnp.exp2(l - top) for l in lses]
    mix = wts[0] * obufs[0][...] + wts[1] * obufs[1][...] + wts[2] * obufs[2][...]
    o_ref[...] = (mix / (wts[0] + wts[1] + wts[2])).astype(o_ref.dtype)


def _dilated(qkv, batch, seq):
    sup = DIL_SUPER
    assert seq % sup == 0
    nsb = seq // sup
    heads = N_HEADS_A

    def cur(col0):
        return pl.BlockSpec((sup, HEAD_DIM), lambda b, h, s: (b * nsb + s, col0 + h))

    def prev(col0):
        return pl.BlockSpec((sup, HEAD_DIM), lambda b, h, s: (b * nsb + jnp.maximum(s - 1, 0), col0 + h))

    blk = sup * HEAD_DIM * 4
    vmem = 2 * 5 * blk + 2 * sup * HEAD_DIM * 2 + 4 * blk + 6 * blk + 8 * blk + (8 << 20)
    return pl.pallas_call(
        _dilated_body,
        grid=(batch, heads, nsb),
        in_specs=[cur(0), cur(heads), prev(heads), cur(2 * heads), prev(2 * heads)],
        out_specs=pl.BlockSpec((sup, HEAD_DIM), lambda b, h, s: (b * nsb + s, h)),
        out_shape=jax.ShapeDtypeStruct((batch * seq, WIDTH_A), BF16),
        scratch_shapes=[pltpu.VMEM((2 * sup, HEAD_DIM), F32)] * 2 + [pltpu.VMEM((sup, HEAD_DIM), F32)] * 6,
        compiler_params=_params(("parallel", "parallel", "arbitrary"), vmem),
        name="dilated",
    )(qkv, qkv, qkv, qkv, qkv)


def _scores_t(k_blk, q):
    return lax.dot_general(k_blk, q, NT_DIMS, preferred_element_type=F32)


def _flash_t(streams, qi, tk, halves, diag_mask, past_mask):
    assert halves in (1, 2)
    tq = halves * tk
    first = halves * qi

    def key_tile(k_ref, j):
        return k_ref[pl.ds(pl.multiple_of(j * tk, tk), tk), :]

    for q, k_ref, _, p_ref, acc_ref, st_ref in streams:
        s = jnp.concatenate(
            [diag_mask(_scores_t(key_tile(k_ref, first + half), q[half * tk:(half + 1) * tk, :]), half)
             for half in range(halves)], axis=1).astype(BF16)
        m = jnp.max(s, axis=0, keepdims=True)
        p_ref[0] = jnp.exp2(s - m)
        acc_ref[...] = jnp.zeros(acc_ref.shape, F32)
        st_ref[0] = m.astype(F32)
        st_ref[1] = jnp.ones(m.shape, F32)

    def flush_own():
        for _, _, vt_ref, p_ref, acc_ref, st_ref in streams:
            pv = jnp.concatenate(
                [jnp.dot(vt_ref[first + half], p_ref[0, :, half * tk:(half + 1) * tk],
                         preferred_element_type=F32) for half in range(halves)], axis=1)
            acc_ref[...] = st_ref[1] * acc_ref[...] + pv

    def flush(pending, slot):
        for _, _, vt_ref, p_ref, acc_ref, st_ref in streams:
            acc_ref[...] = st_ref[1] * acc_ref[...] + jnp.dot(vt_ref[pending], p_ref[slot],
                                                              preferred_element_type=F32)

    def step(j, do_flush, wr, mask):
        scores = [_scores_t(key_tile(st[1], j), st[0]) for st in streams]
        do_flush()
        for s, (_, _, _, p_ref, _, st_ref) in zip(scores, streams):
            s = mask(s, j).astype(BF16)
            m = st_ref[0]
            m_new = jnp.maximum(m, jnp.max(s, axis=0, keepdims=True).astype(F32))
            p_ref[wr] = jnp.exp2(s - m_new.astype(BF16))
            st_ref[0] = m_new
            st_ref[1] = jnp.exp2(m - m_new)

    if halves == 2:
        def second_half_only(s, j):
            qry = lax.broadcasted_iota(jnp.int32, (tk, tq), 1)
            return jnp.where(qry >= tk, past_mask(s, j), NEG_INF)

        step(first, flush_own, 1, second_half_only)
    slot0 = halves - 1
    n_past = halves * qi

    def past_step(j, parity):
        rd = slot0 ^ parity
        step(j, lambda: flush(jnp.where(j == 0, first, j - 1), rd), 1 - rd, past_mask)

    def four(i, carry):
        for u in range(PAST_UNROLL):
            past_step(PAST_UNROLL * i + u, u % 2)
        return carry

    lax.fori_loop(0, n_past // PAST_UNROLL, four, 0)
    rest = n_past % PAST_UNROLL
    base = n_past - rest

    @pl.when(rest >= 2)
    def _():
        past_step(base, 0)
        past_step(base + 1, 1)

    last = jnp.where(n_past == 0, first, n_past - 1)
    if halves == 2:
        flush(last, slot0)
    else:
        @pl.when(rest % 2 == 1)
        def _():
            past_step(n_past - 1, 0)
            flush(last, 1 - slot0)

        @pl.when(rest % 2 == 0)
        def _():
            flush(last, slot0)

    outs = []
    for _, _, _, _, acc_ref, _ in streams:
        dv = acc_ref.shape[0] - ONES_ROWS
        outs.append((acc_ref[0:dv, :], acc_ref[dv:dv + 1, :]))
    return outs


def _store_transposed_blocks(v_ref, vt_ref):
    nblk, rows_t, rows = vt_ref.shape
    dv = rows_t - ONES_ROWS
    for n in range(nblk):
        vt_ref[n, 0:dv, :] = v_ref[n * rows:(n + 1) * rows, :].astype(F32).T.astype(vt_ref.dtype)
        vt_ref[n, dv:rows_t, :] = jnp.ones((ONES_ROWS, rows), vt_ref.dtype)


def _moba_body(q_ref, k_ref, v_ref, o_ref, kmean_hi, kmean_lo, vt_ref, sel_ref, p_ref, acc_ref, st_ref):
    qi = pl.program_id(2)
    blk, tk, tq = MOBA_BLOCK, ATTN_KEYS, MOBA_HALVES * ATTN_KEYS
    nb = k_ref.shape[0] // blk

    @pl.when(qi == 0)
    def _():
        for n in range(nb):
            mean = jnp.mean(k_ref[n * blk:(n + 1) * blk, :].astype(F32), axis=0, keepdims=True)
            hi = mean.astype(BF16)
            kmean_hi[n:n + 1, :] = hi
            kmean_lo[n:n + 1, :] = (mean - hi.astype(F32)).astype(BF16)
        _store_transposed_blocks(v_ref, vt_ref)

    q = q_ref[...]
    gate = _scores_t(kmean_hi[...], q) + _scores_t(kmean_lo[...], q)
    blk_id = lax.broadcasted_iota(jnp.int32, gate.shape, 0)
    q_blk = (tq // blk) * qi + lax.broadcasted_iota(jnp.int32, gate.shape, 1) // blk
    past = blk_id < q_blk
    work = jnp.where(past, gate, NEG_INF)
    chosen = jnp.zeros(gate.shape, F32)
    for _ in range(MOBA_TOPK):
        best = jnp.max(work, axis=0, keepdims=True)
        first = jnp.min(jnp.where(work == best, blk_id, nb), axis=0, keepdims=True)
        pick = blk_id == first
        chosen = jnp.where(pick, 1.0, chosen)
        work = jnp.where(pick, -jnp.inf, work)
    sel_ref[...] = jnp.where(past, chosen, 0.0)

    def picked(block):
        return sel_ref[pl.ds(block, 1), :] > 0.5

    def past_mask(s, j):
        return jnp.concatenate([jnp.where(picked(2 * j), s[:blk], NEG_INF),
                                jnp.where(picked(2 * j + 1), s[blk:], NEG_INF)], axis=0)

    def diag_mask(s, half):
        key = lax.broadcasted_iota(jnp.int32, (tk, tk), 0)
        qry = lax.broadcasted_iota(jnp.int32, (tk, tk), 1)
        first_picked = sel_ref[pl.ds(2 * (MOBA_HALVES * qi + half), 1), half * tk:(half + 1) * tk] > 0.5
        other = jnp.where(key < blk, jnp.where(first_picked, s, NEG_INF), NEG_INF)
        own_start = (qry // blk) * blk
        return jnp.where(key <= qry, jnp.where(key >= own_start, s, other), other)

    [(acc, l)] = _flash_t([(q, k_ref, vt_ref, p_ref, acc_ref, st_ref)], qi, tk, MOBA_HALVES, diag_mask, past_mask)
    o_ref[...] = (acc * (1.0 / l)).T.astype(o_ref.dtype)


def _moba(qkv, batch, seq):
    blk, tk, tq = MOBA_BLOCK, ATTN_KEYS, MOBA_HALVES * ATTN_KEYS
    assert seq % tq == 0
    nq = seq // tq
    nb = seq // blk
    heads = N_HEADS_B
    vmem = (2 * 2 * seq * HEAD_DIM * 2 + seq * HEAD_DIM * 2 + 4 * tq * HEAD_DIM * 2 + 2 * tk * tq * 2
            + 2 * HEAD_DIM * tq * 4 + 10 * tk * tq * 4 + (8 << 20))
    return pl.pallas_call(
        _moba_body,
        grid=(batch, heads, nq),
        in_specs=[
            pl.BlockSpec((tq, HEAD_DIM), lambda b, h, i: (b * nq + i, h)),
            pl.BlockSpec((seq, HEAD_DIM), lambda b, h, i: (b, heads + h)),
            pl.BlockSpec((seq, HEAD_DIM), lambda b, h, i: (b, 2 * heads + h)),
        ],
        out_specs=pl.BlockSpec((tq, HEAD_DIM), lambda b, h, i: (b * nq + i, h)),
        out_shape=jax.ShapeDtypeStruct((batch * seq, WIDTH_B), BF16),
        scratch_shapes=[pltpu.VMEM((nb, HEAD_DIM), BF16)] * 2
        + [pltpu.VMEM((seq // tk, HEAD_DIM + ONES_ROWS, tk), BF16), pltpu.VMEM((nb, tq), F32),
           pltpu.VMEM((2, tk, tq), BF16), pltpu.VMEM((HEAD_DIM + ONES_ROWS, tq), F32),
           pltpu.VMEM((2, 1, tq), F32)],
        compiler_params=_params(("parallel", "parallel", "arbitrary"), vmem),
        name="moba",
    )(qkv, qkv, qkv)


def _diff_body(lq1_ref, lk1_ref, lq2_ref, lk2_ref, g_ref, q1_ref, q2_ref, k1_ref, k2_ref, v_ref, o_ref,
               vt_ref, p1_ref, p2_ref, acc1_ref, acc2_ref, st1_ref, st2_ref, *, lambda_init):
    qi = pl.program_id(2)
    t = ATTN_KEYS

    @pl.when(qi == 0)
    def _():
        _store_transposed_blocks(v_ref, vt_ref)

    def diag_mask(s, half):
        key = lax.broadcasted_iota(jnp.int32, (t, t), 0)
        qry = lax.broadcasted_iota(jnp.int32, (t, t), 1)
        return jnp.where(key <= qry, s, NEG_INF)

    streams = [(q1_ref[...], k1_ref, vt_ref, p1_ref, acc1_ref, st1_ref),
               (q2_ref[...], k2_ref, vt_ref, p2_ref, acc2_ref, st2_ref)]
    (a1, l1), (a2, l2) = _flash_t(streams, qi, t, DIFF_HALVES, diag_mask, lambda s, j: s)

    lam = (jnp.exp(jnp.sum(lq1_ref[...] * lk1_ref[...], axis=-1, keepdims=True))
           - jnp.exp(jnp.sum(lq2_ref[...] * lk2_ref[...], axis=-1, keepdims=True)) + lambda_init)
    o = a1 * (1.0 / l1) - lam * (a2 * (1.0 / l2))
    o = o * lax.rsqrt(jnp.mean(o * o, axis=0, keepdims=True) + SUBLN_EPS)
    o_ref[...] = (o.T * g_ref[...] * (1.0 - lambda_init)).astype(o_ref.dtype)


def _diff(qkv, lq1, lk1, lq2, lk2, subln_g, lambda_init, batch, seq):
    t, tq = ATTN_KEYS, DIFF_HALVES * ATTN_KEYS
    assert seq % tq == 0
    nq = seq // tq
    heads = N_HEADS_C
    dv = 2 * HEAD_DIM
    kcol = D_MODEL // HEAD_DIM
    vcol = 2 * D_MODEL // dv

    def vec(width):
        return pl.BlockSpec((1, width), lambda b, h, i: (0, 0))

    def q_spec(part):
        return pl.BlockSpec((tq, HEAD_DIM), lambda b, h, i: (b * nq + i, 2 * h + part))

    def k_spec(part):
        return pl.BlockSpec((seq, HEAD_DIM), lambda b, h, i: (b, kcol + 2 * h + part))

    vmem = (2 * (2 * seq * HEAD_DIM + seq * dv) * 2 + seq * dv * 2 + 8 * tq * HEAD_DIM * 2 + 2 * tq * dv * 2
            + 2 * tq * dv * 4 + 4 * t * tq * 2 + 10 * t * tq * 4 + (8 << 20))
    return pl.pallas_call(
        functools.partial(_diff_body, lambda_init=lambda_init),
        grid=(batch, heads, nq),
        in_specs=[vec(HEAD_DIM)] * 4 + [vec(dv), q_spec(0), q_spec(1), k_spec(0), k_spec(1),
                                        pl.BlockSpec((seq, dv), lambda b, h, i: (b, vcol + h))],
        out_specs=pl.BlockSpec((tq, dv), lambda b, h, i: (b * nq + i, h)),
        out_shape=jax.ShapeDtypeStruct((batch * seq, heads * dv), BF16),
        scratch_shapes=[pltpu.VMEM((seq // t, dv + ONES_ROWS, t), BF16)] + [pltpu.VMEM((2, t, tq), BF16)] * 2
        + [pltpu.VMEM((dv + ONES_ROWS, tq), F32)] * 2 + [pltpu.VMEM((2, 1, tq), F32)] * 2,
        compiler_params=_params(("parallel", "parallel", "arbitrary"), vmem),
        name="diffattn",
    )(lq1.reshape(1, -1), lk1.reshape(1, -1), lq2.reshape(1, -1), lk2.reshape(1, -1),
      subln_g.reshape(1, -1), qkv, qkv, qkv, qkv, qkv)


def _lambda_init(layer):
    return 0.8 - 0.6 * math.exp(-0.3 * layer)


def kernel(x, ffa_norm, ffa_w_in, ffa_w_out, mix_norm, even_w_in, even_w_out, odd_w_in, odd_w_out,
           lambda_q1, lambda_k1, lambda_q2, lambda_k2, subln_norm, ffb_norm, ffb_w_in, ffb_w_out, final_norm):
    batch, seq, d = x.shape
    depth = ffa_norm.shape[0]
    cos, sin = _rope_tables(seq)
    h = x.reshape(batch * seq, d)
    for layer in range(depth):
        i = layer // 2
        h = _ffn(h, ffa_norm[layer], ffa_w_in, ffa_w_out, layer)
        if layer % 2 == 0:
            wa, wb = 3 * WIDTH_A, 3 * WIDTH_B
            qkv_a = _proj(h, mix_norm[layer], even_w_in, i, cos, sin,
                          0, WIDTH_A, 2 * WIDTH_A, wa, SCALE * LOG2_E, F32)
            qkv_b = _proj(h, mix_norm[layer], even_w_in, i, cos, sin,
                          wa, WIDTH_B, 2 * WIDTH_B, wb, SCALE * LOG2_E, BF16)
            o_a = _dilated(qkv_a, batch, seq)
            o_b = _moba(qkv_b, batch, seq)
            h = _oproj(h, [o_a, o_b], even_w_out, i)
        else:
            qkv = _proj(h, mix_norm[layer], odd_w_in, i, cos, sin,
                        0, D_MODEL, 2 * D_MODEL, 3 * D_MODEL, SCALE * LOG2_E, BF16)
            o = _diff(qkv, lambda_q1[i], lambda_k1[i], lambda_q2[i], lambda_k2[i], subln_norm[i],
                      _lambda_init(layer), batch, seq)
            h = _oproj(h, [o], odd_w_out, i)
        last = layer == depth - 1
        h = _ffn(h, ffb_norm[layer], ffb_w_in, ffb_w_out, layer, final_norm if last else None)
    return h.reshape(batch, seq, d)
```

```python
import functools
import math

import jax
import jax.numpy as jnp
from jax import lax
from jax.experimental import pallas as pl
from jax.experimental.pallas import tpu as pltpu

F32 = jnp.float32
BF16 = jnp.bfloat16

D_MODEL = 2048
HEAD_DIM = 128
HALF_DIM = HEAD_DIM // 2
N_HEADS_A = 8
N_HEADS_B = 8
N_HEADS_C = 8
WIDTH_A = N_HEADS_A * HEAD_DIM
WIDTH_B = N_HEADS_B * HEAD_DIM
DILATIONS = (1, 4, 16)
DIL_BLOCK = 128
DIL_REACH = 128
DIL_SUPER = DIL_BLOCK * DILATIONS[-1]
DIL_UNROLL = 16
MOBA_BLOCK = 256
MOBA_TOPK = 3
ATTN_KEYS = 512
MOBA_HALVES = 2
DIFF_HALVES = 1
ROPE_THETA = 10000.0
NORM_EPS = 1e-6
SUBLN_EPS = 1e-5
NEG_INF = -1e30
SCALE = HEAD_DIM ** -0.5
LOG2_E = math.log2(math.e)

VMEM_V7X_BYTES = 64 * 1024 * 1024
NT_DIMS = (((1,), (1,)), ((), ()))
ONES_ROWS = 16
PAST_UNROLL = 4


def _params(semantics, vmem_bytes):
    assert vmem_bytes < VMEM_V7X_BYTES
    return pltpu.CompilerParams(dimension_semantics=semantics, vmem_limit_bytes=int(vmem_bytes))


def _rms(x, g, eps):
    return x * lax.rsqrt(jnp.mean(x * x, axis=-1, keepdims=True) + eps) * g


FFN_ROWS = 1024
FFN_COLS = 512


def _ffn_body(h_hbm, g_ref, wg_ref, wu_ref, wo_ref, *rest, final):
    if final:
        fg_ref, o_ref, xn_ref, x_buf, x_sem = rest
    else:
        o_ref, xn_ref, x_buf, x_sem = rest
    i, j = pl.program_id(0), pl.program_id(1)
    rows = x_buf.shape[0]

    def x_copy(tile):
        return pltpu.make_async_copy(h_hbm.at[pl.ds(tile * rows, rows), :], x_buf, x_sem)

    @pl.when(j == 0)
    def _():
        @pl.when(i == 0)
        def _():
            x_copy(i).start()

        x_copy(i).wait()
        x = x_buf[...]
        xn_ref[...] = _rms(x, g_ref[...], NORM_EPS).astype(BF16)
        o_ref[...] = x

    @pl.when(jnp.logical_and(j == 1, i + 1 < pl.num_programs(0)))
    def _():
        x_copy(i + 1).start()

    xn = xn_ref[...]
    gate = jnp.dot(xn, wg_ref[...].astype(BF16), preferred_element_type=F32)
    up = jnp.dot(xn, wu_ref[...].astype(BF16), preferred_element_type=F32)
    act = (0.5 * gate / (1.0 + jnp.exp(-gate))) * up
    o_ref[...] += jnp.dot(act.astype(BF16), wo_ref[...].astype(BF16), preferred_element_type=F32)

    if final:
        @pl.when(j == pl.num_programs(1) - 1)
        def _():
            o_ref[...] = _rms(o_ref[...], fg_ref[...], NORM_EPS)


def _ffn(h, g, w_in, w_out, layer, final_g=None):
    n, d = h.shape
    d_ff = w_out.shape[1]
    tm, tf = FFN_ROWS, FFN_COLS
    nff = d_ff // tf
    assert n % tm == 0 and d_ff % tf == 0 and nff >= 2
    final = final_g is not None
    in_specs = [
        pl.BlockSpec(memory_space=pl.ANY),
        pl.BlockSpec((1, d), lambda i, j: (0, 0)),
        pl.BlockSpec((None, d, tf), lambda i, j: (layer, 0, j)),
        pl.BlockSpec((None, d, tf), lambda i, j: (layer, 0, nff + j)),
        pl.BlockSpec((None, tf, d), lambda i, j: (layer, j, 0)),
    ]
    args = [h, g.reshape(1, d), w_in, w_in, w_out]
    if final:
        in_specs.append(pl.BlockSpec((1, d), lambda i, j: (0, 0)))
        args.append(final_g.reshape(1, d))
    vmem = 3 * tm * d * 4 + tm * d * 2 + 2 * (2 * d * tf + tf * d) * 4 + (8 << 20)
    return pl.pallas_call(
        functools.partial(_ffn_body, final=final),
        grid=(n // tm, nff),
        in_specs=in_specs,
        out_specs=pl.BlockSpec((tm, d), lambda i, j: (i, 0)),
        out_shape=jax.ShapeDtypeStruct((n, d), F32),
        scratch_shapes=[pltpu.VMEM((tm, d), BF16), pltpu.VMEM((tm, d), F32), pltpu.SemaphoreType.DMA(())],
        compiler_params=_params(("arbitrary", "arbitrary"), vmem),
        name="ffn",
    )(*args)


PROJ_ROWS = 1024
PROJ_COLS = 1024


def _proj_body(x_ref, g_ref, w_ref, cos_ref, sin_ref, o_ref, xn_ref, *, q_tiles, q_scale):
    j = pl.program_id(1)

    @pl.when(j == 0)
    def _():
        xn_ref[...] = _rms(x_ref[...], g_ref[...], NORM_EPS).astype(BF16)

    y = jnp.dot(xn_ref[...], w_ref[...], preferred_element_type=F32)
    scale = jnp.where(j < q_tiles, q_scale, 1.0).astype(F32)
    cos = cos_ref[...] * scale
    sin = sin_ref[...] * scale
    for c in range(y.shape[1] // HEAD_DIM):
        yh = y[:, c * HEAD_DIM:(c + 1) * HEAD_DIM]
        rot = yh * cos + pltpu.roll(yh, HALF_DIM, 1) * sin
        o_ref[:, c * HEAD_DIM:(c + 1) * HEAD_DIM] = rot.astype(o_ref.dtype)


def _proj(h, g, w, cos, sin, col0, q_width, rope_width, width, q_scale, out_dtype):
    n, d = h.shape
    seq = cos.shape[1]
    tm, tn = PROJ_ROWS, PROJ_COLS
    assert n % tm == 0 and seq % tm == 0
    assert all(c % tn == 0 for c in (col0, q_width, rope_width, width))
    pos_blocks = seq // tm
    rope_tiles = rope_width // tn
    body = functools.partial(_proj_body, q_tiles=q_width // tn, q_scale=q_scale)

    def table_spec():
        return pl.BlockSpec((None, tm, HEAD_DIM),
                            lambda i, j: (jnp.where(j < rope_tiles, 0, 1), i % pos_blocks, 0))

    vmem = 2 * tm * d * 4 + tm * d * 2 + 2 * d * tn * 2 + 4 * tm * tn * 4 + 4 * tm * HEAD_DIM * 4 + (8 << 20)
    return pl.pallas_call(
        body,
        grid=(n // tm, width // tn),
        in_specs=[
            pl.BlockSpec((tm, d), lambda i, j: (i, 0)),
            pl.BlockSpec((1, d), lambda i, j: (0, 0)),
            pl.BlockSpec((d, tn), lambda i, j: (0, col0 // tn + j)),
            table_spec(),
            table_spec(),
        ],
        out_specs=pl.BlockSpec((tm, tn), lambda i, j: (i, j)),
        out_shape=jax.ShapeDtypeStruct((n, width), out_dtype),
        scratch_shapes=[pltpu.VMEM((tm, d), BF16)],
        compiler_params=_params(("parallel", "arbitrary"), vmem),
        name="proj",
    )(h, g.reshape(1, d), w, cos, sin)


def _rope_tables(seq):
    inv_freq = ROPE_THETA ** (-jnp.arange(HALF_DIM, dtype=F32) / HALF_DIM)
    ang = jnp.arange(seq, dtype=F32)[:, None] * inv_freq[None, :]
    cos, sin = jnp.cos(ang), jnp.sin(ang)
    cos = jnp.concatenate([cos, cos], axis=-1)
    sin = jnp.concatenate([-sin, sin], axis=-1)
    return jnp.stack([cos, jnp.ones_like(cos)]), jnp.stack([sin, jnp.zeros_like(sin)])


OPROJ_ROWS = 512
OPROJ_COLS = 2048


def _oproj_body(*refs):
    *aw, h_ref, o_ref = refs
    acc = h_ref[...]
    for a_ref, w_ref in zip(aw[0::2], aw[1::2]):
        acc = acc + jnp.dot(a_ref[...], w_ref[...], preferred_element_type=F32)
    o_ref[...] = acc


def _oproj(h, parts, w):
    n, d = h.shape
    tm, tn = OPROJ_ROWS, OPROJ_COLS
    assert n % tm == 0 and d % tn == 0
    in_specs, args, vmem, row0 = [], [], 0, 0
    for a in parts:
        kdim = a.shape[1]
        assert row0 % kdim == 0
        in_specs += [pl.BlockSpec((tm, kdim), lambda i, j: (i, 0)),
                     pl.BlockSpec((kdim, tn), lambda i, j, r=row0 // kdim: (r, j))]
        args += [a, w]
        vmem += 2 * (tm * kdim + kdim * tn) * 2
        row0 += kdim
    assert row0 == w.shape[0]
    in_specs.append(pl.BlockSpec((tm, tn), lambda i, j: (i, j)))
    args.append(h)
    vmem += 6 * tm * tn * 4 + (8 << 20)
    return pl.pallas_call(
        _oproj_body,
        grid=(n // tm, d // tn),
        in_specs=in_specs,
        out_specs=pl.BlockSpec((tm, tn), lambda i, j: (i, j)),
        out_shape=jax.ShapeDtypeStruct((n, d), F32),
        compiler_params=_params(("parallel", "parallel"), vmem),
        name="oproj",
    )(*args)


def _dilated_body(q_ref, kc_ref, kp_ref, vc_ref, vp_ref, o_ref, kbuf, vbuf, *branch_bufs):
    obufs, lbufs = branch_bufs[:3], branch_bufs[3:]
    sb = pl.program_id(2)
    sup = DIL_SUPER
    kbuf[0:sup, :] = kp_ref[...]
    kbuf[sup:2 * sup, :] = kc_ref[...]
    vbuf[0:sup, :] = vp_ref[...]
    vbuf[sup:2 * sup, :] = vc_ref[...]

    row = lax.broadcasted_iota(jnp.int32, (DIL_BLOCK, 2 * DIL_BLOCK), 0)
    col = lax.broadcasted_iota(jnp.int32, (DIL_BLOCK, 2 * DIL_BLOCK), 1)
    band = jnp.logical_and(col >= row + (DIL_BLOCK - DIL_REACH), col <= row + DIL_BLOCK)

    for br, dil in enumerate(DILATIONS):
        nsub = sup // (DIL_BLOCK * dil)
        obuf, lbuf = obufs[br], lbufs[br]

        def block(t, carry, dil=dil, nsub=nsub, obuf=obuf, lbuf=lbuf):
            res = t // nsub
            sub = t % nsub
            q0 = res + dil * DIL_BLOCK * sub
            k0 = sup + q0 - dil * DIL_BLOCK
            if dil == 1:
                q0 = pl.multiple_of(q0, DIL_BLOCK)
                k0 = pl.multiple_of(k0, DIL_BLOCK)
                qs, ks = pl.ds(q0, DIL_BLOCK), pl.ds(k0, 2 * DIL_BLOCK)
            else:
                qs, ks = pl.ds(q0, DIL_BLOCK, stride=dil), pl.ds(k0, 2 * DIL_BLOCK, stride=dil)
            qb = q_ref[qs, :].astype(BF16)
            kb = kbuf[ks, :].astype(BF16)
            vb = vbuf[ks, :].astype(BF16)
            s = lax.dot_general(qb, kb, NT_DIMS, preferred_element_type=F32)
            has_prev = jnp.logical_or(sb > 0, sub > 0)
            first_col = jnp.where(has_prev, 0, DIL_BLOCK)
            s = jnp.where(jnp.logical_and(band, col >= first_col), s, NEG_INF)
            m = jnp.max(s, axis=-1, keepdims=True)
            p = jnp.exp2((s - m).astype(BF16))
            pv = jnp.dot(p, jnp.concatenate([vb, jnp.ones_like(vb)], axis=1), preferred_element_type=F32)
            den = pv[:, HEAD_DIM:]
            obuf[qs, :] = pv[:, :HEAD_DIM] / den
            lbuf[qs, :] = m + jnp.log2(den)
            return carry

        lax.fori_loop(0, nsub * dil, block, 0, unroll=DIL_UNROLL)

    lses = [lbuf[...] for lbuf in lbufs]
    top = jnp.maximum(jnp.maximum(lses[0], lses[1]), lses[2])
    wts = [jnp.exp2(l - top) for l in lses]
    mix = wts[0] * obufs[0][...] + wts[1] * obufs[1][...] + wts[2] * obufs[2][...]
    o_ref[...] = (mix / (wts[0] + wts[1] + wts[2])).astype(o_ref.dtype)


def _dilated(qkv, batch, seq):
    sup = DIL_SUPER
    assert seq % sup == 0
    nsb = seq // sup
    heads = N_HEADS_A

    def cur(col0):
        return pl.BlockSpec((sup, HEAD_DIM), lambda b, h, s: (b * nsb + s, col0 + h))

    def prev(col0):
        return pl.BlockSpec((sup, HEAD_DIM), lambda b, h, s: (b * nsb + jnp.maximum(s - 1, 0), col0 + h))

    blk = sup * HEAD_DIM * 4
    vmem = 2 * 5 * blk + 2 * sup * HEAD_DIM * 2 + 4 * blk + 6 * blk + 8 * blk + (8 << 20)
    return pl.pallas_call(
        _dilated_body,
        grid=(batch, heads, nsb),
        in_specs=[cur(0), cur(heads), prev(heads), cur(2 * heads), prev(2 * heads)],
        out_specs=pl.BlockSpec((sup, HEAD_DIM), lambda b, h, s: (b * nsb + s, h)),
        out_shape=jax.ShapeDtypeStruct((batch * seq, WIDTH_A), BF16),
        scratch_shapes=[pltpu.VMEM((2 * sup, HEAD_DIM), F32)] * 2 + [pltpu.VMEM((sup, HEAD_DIM), F32)] * 6,
        compiler_params=_params(("parallel", "parallel", "arbitrary"), vmem),
        name="dilated",
    )(qkv, qkv, qkv, qkv, qkv)


def _scores_t(k_blk, q):
    return lax.dot_general(k_blk, q, NT_DIMS, preferred_element_type=F32)


def _flash_t(streams, qi, tk, halves, diag_mask, past_mask):
    assert halves in (1, 2)
    tq = halves * tk
    first = halves * qi

    def key_tile(k_ref, j):
        return k_ref[pl.ds(pl.multiple_of(j * tk, tk), tk), :]

    for q, k_ref, _, p_ref, acc_ref, st_ref in streams:
        s = jnp.concatenate(
            [diag_mask(_scores_t(key_tile(k_ref, first + half), q[half * tk:(half + 1) * tk, :]), half)
             for half in range(halves)], axis=1).astype(BF16)
        m = jnp.max(s, axis=0, keepdims=True)
        p_ref[0] = jnp.exp2(s - m)
        acc_ref[...] = jnp.zeros(acc_ref.shape, F32)
        st_ref[0] = m.astype(F32)
        st_ref[1] = jnp.ones(m.shape, F32)

    def flush_own():
        for _, _, vt_ref, p_ref, acc_ref, st_ref in streams:
            pv = jnp.concatenate(
                [jnp.dot(vt_ref[first + half], p_ref[0, :, half * tk:(half + 1) * tk],
                         preferred_element_type=F32) for half in range(halves)], axis=1)
            acc_ref[...] = st_ref[1] * acc_ref[...] + pv

    def flush(pending, slot):
        for _, _, vt_ref, p_ref, acc_ref, st_ref in streams:
            acc_ref[...] = st_ref[1] * acc_ref[...] + jnp.dot(vt_ref[pending], p_ref[slot],
                                                              preferred_element_type=F32)

    def step(j, do_flush, wr, mask):
        scores = [_scores_t(key_tile(st[1], j), st[0]) for st in streams]
        do_flush()
        for s, (_, _, _, p_ref, _, st_ref) in zip(scores, streams):
            s = mask(s, j).astype(BF16)
            m = st_ref[0]
            m_new = jnp.maximum(m, jnp.max(s, axis=0, keepdims=True).astype(F32))
            p_ref[wr] = jnp.exp2(s - m_new.astype(BF16))
            st_ref[0] = m_new
            st_ref[1] = jnp.exp2(m - m_new)

    if halves == 2:
        def second_half_only(s, j):
            qry = lax.broadcasted_iota(jnp.int32, (tk, tq), 1)
            return jnp.where(qry >= tk, past_mask(s, j), NEG_INF)

        step(first, flush_own, 1, second_half_only)
    slot0 = halves - 1
    n_past = halves * qi

    def past_step(j, parity):
        rd = slot0 ^ parity
        step(j, lambda: flush(jnp.where(j == 0, first, j - 1), rd), 1 - rd, past_mask)

    def four(i, carry):
        for u in range(PAST_UNROLL):
            past_step(PAST_UNROLL * i + u, u % 2)
        return carry

    lax.fori_loop(0, n_past // PAST_UNROLL, four, 0)
    rest = n_past % PAST_UNROLL
    base = n_past - rest

    @pl.when(rest >= 2)
    def _():
        past_step(base, 0)
        past_step(base + 1, 1)

    last = jnp.where(n_past == 0, first, n_past - 1)
    if halves == 2:
        flush(last, slot0)
    else:
        @pl.when(rest % 2 == 1)
        def _():
            past_step(n_past - 1, 0)
            flush(last, 1 - slot0)

        @pl.when(rest % 2 == 0)
        def _():
            flush(last, slot0)

    outs = []
    for _, _, _, _, acc_ref, _ in streams:
        dv = acc_ref.shape[0] - ONES_ROWS
        outs.append((acc_ref[0:dv, :], acc_ref[dv:dv + 1, :]))
    return outs


def _store_transposed_blocks(v_ref, vt_ref):
    nblk, rows_t, rows = vt_ref.shape
    dv = rows_t - ONES_ROWS
    for n in range(nblk):
        vt_ref[n, 0:dv, :] = v_ref[n * rows:(n + 1) * rows, :].astype(F32).T.astype(vt_ref.dtype)
        vt_ref[n, dv:rows_t, :] = jnp.ones((ONES_ROWS, rows), vt_ref.dtype)


def _moba_body(q_ref, k_ref, v_ref, o_ref, kmean_hi, kmean_lo, vt_ref, sel_ref, p_ref, acc_ref, st_ref):
    qi = pl.program_id(2)
    blk, tk, tq = MOBA_BLOCK, ATTN_KEYS, MOBA_HALVES * ATTN_KEYS
    nb = k_ref.shape[0] // blk

    @pl.when(qi == 0)
    def _():
        for n in range(nb):
            mean = jnp.mean(k_ref[n * blk:(n + 1) * blk, :].astype(F32), axis=0, keepdims=True)
            hi = mean.astype(BF16)
            kmean_hi[n:n + 1, :] = hi
            kmean_lo[n:n + 1, :] = (mean - hi.astype(F32)).astype(BF16)
        _store_transposed_blocks(v_ref, vt_ref)

    q = q_ref[...]
    gate = _scores_t(kmean_hi[...], q) + _scores_t(kmean_lo[...], q)
    blk_id = lax.broadcasted_iota(jnp.int32, gate.shape, 0)
    q_blk = (tq // blk) * qi + lax.broadcasted_iota(jnp.int32, gate.shape, 1) // blk
    past = blk_id < q_blk
    work = jnp.where(past, gate, NEG_INF)
    chosen = jnp.zeros(gate.shape, F32)
    for _ in range(MOBA_TOPK):
        best = jnp.max(work, axis=0, keepdims=True)
        first = jnp.min(jnp.where(work == best, blk_id, nb), axis=0, keepdims=True)
        pick = blk_id == first
        chosen = jnp.where(pick, 1.0, chosen)
        work = jnp.where(pick, -jnp.inf, work)
    sel_ref[...] = jnp.where(past, chosen, 0.0)

    def picked(block):
        return sel_ref[pl.ds(block, 1), :] > 0.5

    def past_mask(s, j):
        return jnp.concatenate([jnp.where(picked(2 * j), s[:blk], NEG_INF),
                                jnp.where(picked(2 * j + 1), s[blk:], NEG_INF)], axis=0)

    def diag_mask(s, half):
        key = lax.broadcasted_iota(jnp.int32, (tk, tk), 0)
        qry = lax.broadcasted_iota(jnp.int32, (tk, tk), 1)
        first_picked = sel_ref[pl.ds(2 * (MOBA_HALVES * qi + half), 1), half * tk:(half + 1) * tk] > 0.5
        other = jnp.where(key < blk, jnp.where(first_picked, s, NEG_INF), NEG_INF)
        own_start = (qry // blk) * blk
        return jnp.where(key <= qry, jnp.where(key >= own_start, s, other), other)

    [(acc, l)] = _flash_t([(q, k_ref, vt_ref, p_ref, acc_ref, st_ref)], qi, tk, MOBA_HALVES, diag_mask, past_mask)
    o_ref[...] = (acc * (1.0 / l)).T.astype(o_ref.dtype)


def _moba(qkv, batch, seq):
    blk, tk, tq = MOBA_BLOCK, ATTN_KEYS, MOBA_HALVES * ATTN_KEYS
    assert seq % tq == 0
    nq = seq // tq
    nb = seq // blk
    heads = N_HEADS_B
    vmem = (2 * 2 * seq * HEAD_DIM * 2 + seq * HEAD_DIM * 2 + 4 * tq * HEAD_DIM * 2 + 2 * tk * tq * 2
            + 2 * HEAD_DIM * tq * 4 + 10 * tk * tq * 4 + (8 << 20))
    return pl.pallas_call(
        _moba_body,
        grid=(batch, heads, nq),
        in_specs=[
            pl.BlockSpec((tq, HEAD_DIM), lambda b, h, i: (b * nq + i, h)),
            pl.BlockSpec((seq, HEAD_DIM), lambda b, h, i: (b, heads + h)),
            pl.BlockSpec((seq, HEAD_DIM), lambda b, h, i: (b, 2 * heads + h)),
        ],
        out_specs=pl.BlockSpec((tq, HEAD_DIM), lambda b, h, i: (b * nq + i, h)),
        out_shape=jax.ShapeDtypeStruct((batch * seq, WIDTH_B), BF16),
        scratch_shapes=[pltpu.VMEM((nb, HEAD_DIM), BF16)] * 2
        + [pltpu.VMEM((seq // tk, HEAD_DIM + ONES_ROWS, tk), BF16), pltpu.VMEM((nb, tq), F32),
           pltpu.VMEM((2, tk, tq), BF16), pltpu.VMEM((HEAD_DIM + ONES_ROWS, tq), F32),
           pltpu.VMEM((2, 1, tq), F32)],
        compiler_params=_params(("parallel", "parallel", "arbitrary"), vmem),
        name="moba",
    )(qkv, qkv, qkv)


def _diff_body(lq1_ref, lk1_ref, lq2_ref, lk2_ref, g_ref, q1_ref, q2_ref, k1_ref, k2_ref, v_ref, o_ref,
               vt_ref, p1_ref, p2_ref, acc1_ref, acc2_ref, st1_ref, st2_ref, *, lambda_init):
    qi = pl.program_id(2)
    t = ATTN_KEYS

    @pl.when(qi == 0)
    def _():
        _store_transposed_blocks(v_ref, vt_ref)

    def diag_mask(s, half):
        key = lax.broadcasted_iota(jnp.int32, (t, t), 0)
        qry = lax.broadcasted_iota(jnp.int32, (t, t), 1)
        return jnp.where(key <= qry, s, NEG_INF)

    streams = [(q1_ref[...], k1_ref, vt_ref, p1_ref, acc1_ref, st1_ref),
               (q2_ref[...], k2_ref, vt_ref, p2_ref, acc2_ref, st2_ref)]
    (a1, l1), (a2, l2) = _flash_t(streams, qi, t, DIFF_HALVES, diag_mask, lambda s, j: s)

    lam = (jnp.exp(jnp.sum(lq1_ref[...] * lk1_ref[...], axis=-1, keepdims=True))
           - jnp.exp(jnp.sum(lq2_ref[...] * lk2_ref[...], axis=-1, keepdims=True)) + lambda_init)
    o = a1 * (1.0 / l1) - lam * (a2 * (1.0 / l2))
    o = o * lax.rsqrt(jnp.mean(o * o, axis=0, keepdims=True) + SUBLN_EPS)
    o_ref[...] = (o.T * g_ref[...] * (1.0 - lambda_init)).astype(o_ref.dtype)


def _diff(qkv, lq1, lk1, lq2, lk2, subln_g, lambda_init, batch, seq):
    t, tq = ATTN_KEYS, DIFF_HALVES * ATTN_KEYS
    assert seq % tq == 0
    nq = seq // tq
    heads = N_HEADS_C
    dv = 2 * HEAD_DIM
    kcol = D_MODEL // HEAD_DIM
    vcol = 2 * D_MODEL // dv

    def vec(width):
        return pl.BlockSpec((1, width), lambda b, h, i: (0, 0))

    def q_spec(part):
        return pl.BlockSpec((tq, HEAD_DIM), lambda b, h, i: (b * nq + i, 2 * h + part))

    def k_spec(part):
        return pl.BlockSpec((seq, HEAD_DIM), lambda b, h, i: (b, kcol + 2 * h + part))

    vmem = (2 * (2 * seq * HEAD_DIM + seq * dv) * 2 + seq * dv * 2 + 8 * tq * HEAD_DIM * 2 + 2 * tq * dv * 2
            + 2 * tq * dv * 4 + 4 * t * tq * 2 + 10 * t * tq * 4 + (8 << 20))
    return pl.pallas_call(
        functools.partial(_diff_body, lambda_init=lambda_init),
        grid=(batch, heads, nq),
        in_specs=[vec(HEAD_DIM)] * 4 + [vec(dv), q_spec(0), q_spec(1), k_spec(0), k_spec(1),
                                        pl.BlockSpec((seq, dv), lambda b, h, i: (b, vcol + h))],
        out_specs=pl.BlockSpec((tq, dv), lambda b, h, i: (b * nq + i, h)),
        out_shape=jax.ShapeDtypeStruct((batch * seq, heads * dv), BF16),
        scratch_shapes=[pltpu.VMEM((seq // t, dv + ONES_ROWS, t), BF16)] + [pltpu.VMEM((2, t, tq), BF16)] * 2
        + [pltpu.VMEM((dv + ONES_ROWS, tq), F32)] * 2 + [pltpu.VMEM((2, 1, tq), F32)] * 2,
        compiler_params=_params(("parallel", "parallel", "arbitrary"), vmem),
        name="diffattn",
    )(lq1.reshape(1, -1), lk1.reshape(1, -1), lq2.reshape(1, -1), lk2.reshape(1, -1),
      subln_g.reshape(1, -1), qkv, qkv, qkv, qkv, qkv)


def _lambda_init(layer):
    return 0.8 - 0.6 * math.exp(-0.3 * layer)


def kernel(x, ffa_norm, ffa_w_in, ffa_w_out, mix_norm, even_w_in, even_w_out, odd_w_in, odd_w_out,
           lambda_q1, lambda_k1, lambda_q2, lambda_k2, subln_norm, ffb_norm, ffb_w_in, ffb_w_out, final_norm):
    batch, seq, d = x.shape
    depth = ffa_norm.shape[0]
    cos, sin = _rope_tables(seq)
    h = x.reshape(batch * seq, d)
    for layer in range(depth):
        i = layer // 2
        h = _ffn(h, ffa_norm[layer], ffa_w_in, ffa_w_out, layer)
        if layer % 2 == 0:
            w_in = even_w_in[i].astype(BF16)
            wa, wb = 3 * WIDTH_A, 3 * WIDTH_B
            qkv_a = _proj(h, mix_norm[layer], w_in, cos, sin, 0, WIDTH_A, 2 * WIDTH_A, wa, SCALE * LOG2_E, F32)
            qkv_b = _proj(h, mix_norm[layer], w_in, cos, sin, wa, WIDTH_B, 2 * WIDTH_B, wb, SCALE * LOG2_E, BF16)
            o_a = _dilated(qkv_a, batch, seq)
            o_b = _moba(qkv_b, batch, seq)
            h = _oproj(h, [o_a, o_b], even_w_out[i].astype(BF16))
        else:
            qkv = _proj(h, mix_norm[layer], odd_w_in[i].astype(BF16), cos, sin,
                        0, D_MODEL, 2 * D_MODEL, 3 * D_MODEL, SCALE * LOG2_E, BF16)
            o = _diff(qkv, lambda_q1[i], lambda_k1[i], lambda_q2[i], lambda_k2[i], subln_norm[i],
                      _lambda_init(layer), batch, seq)
            h = _oproj(h, [o], odd_w_out[i].astype(BF16))
        last = layer == depth - 1
        h = _ffn(h, ffb_norm[layer], ffb_w_in, ffb_w_out, layer, final_norm if last else None)
    return h.reshape(batch, seq, d)
```

```python
import functools
import math

import jax
import jax.numpy as jnp
from jax import lax
from jax.experimental import pallas as pl
from jax.experimental.pallas import tpu as pltpu

F32 = jnp.float32
BF16 = jnp.bfloat16

D_MODEL = 2048
HEAD_DIM = 128
HALF_DIM = HEAD_DIM // 2
N_HEADS_A = 8
N_HEADS_B = 8
N_HEADS_C = 8
WIDTH_A = N_HEADS_A * HEAD_DIM
WIDTH_B = N_HEADS_B * HEAD_DIM
DILATIONS = (1, 4, 16)
DIL_BLOCK = 128
DIL_REACH = 128
DIL_SUPER = DIL_BLOCK * DILATIONS[-1]
DIL_UNROLL = 16
MOBA_BLOCK = 256
MOBA_TOPK = 3
ATTN_KEYS = 512
MOBA_HALVES = 2
DIFF_HALVES = 1
ROPE_THETA = 10000.0
NORM_EPS = 1e-6
SUBLN_EPS = 1e-5
NEG_INF = -1e30
SCALE = HEAD_DIM ** -0.5
LOG2_E = math.log2(math.e)

VMEM_V7X_BYTES = 64 * 1024 * 1024
NT_DIMS = (((1,), (1,)), ((), ()))
ONES_ROWS = 16
PAST_UNROLL = 4


def _params(semantics, vmem_bytes):
    assert vmem_bytes < VMEM_V7X_BYTES
    return pltpu.CompilerParams(dimension_semantics=semantics, vmem_limit_bytes=int(vmem_bytes))


def _rms(x, g, eps):
    return x * lax.rsqrt(jnp.mean(x * x, axis=-1, keepdims=True) + eps) * g


FFN_ROWS = 1024
FFN_COLS = 512


def _ffn_body(h_hbm, g_ref, wg_ref, wu_ref, wo_ref, *rest, final):
    if final:
        fg_ref, o_ref, xn_ref, x_buf, x_sem = rest
    else:
        o_ref, xn_ref, x_buf, x_sem = rest
    i, j = pl.program_id(0), pl.program_id(1)
    rows = x_buf.shape[0]

    def x_copy(tile):
        return pltpu.make_async_copy(h_hbm.at[pl.ds(tile * rows, rows), :], x_buf, x_sem)

    @pl.when(j == 0)
    def _():
        @pl.when(i == 0)
        def _():
            x_copy(i).start()

        x_copy(i).wait()
        x = x_buf[...]
        xn_ref[...] = _rms(x, g_ref[...], NORM_EPS).astype(BF16)
        o_ref[...] = x

    @pl.when(jnp.logical_and(j == 1, i + 1 < pl.num_programs(0)))
    def _():
        x_copy(i + 1).start()

    xn = xn_ref[...]
    gate = jnp.dot(xn, wg_ref[...].astype(BF16), preferred_element_type=F32)
    up = jnp.dot(xn, wu_ref[...].astype(BF16), preferred_element_type=F32)
    act = (0.5 * gate / (1.0 + jnp.exp(-gate))) * up
    o_ref[...] += jnp.dot(act.astype(BF16), wo_ref[...].astype(BF16), preferred_element_type=F32)

    if final:
        @pl.when(j == pl.num_programs(1) - 1)
        def _():
            o_ref[...] = _rms(o_ref[...], fg_ref[...], NORM_EPS)


def _ffn(h, g, w_in, w_out, layer, final_g=None):
    n, d = h.shape
    d_ff = w_out.shape[1]
    tm, tf = FFN_ROWS, FFN_COLS
    nff = d_ff // tf
    assert n % tm == 0 and d_ff % tf == 0 and nff >= 2
    final = final_g is not None
    in_specs = [
        pl.BlockSpec(memory_space=pl.ANY),
        pl.BlockSpec((1, d), lambda i, j: (0, 0)),
        pl.BlockSpec((None, d, tf), lambda i, j: (layer, 0, j)),
        pl.BlockSpec((None, d, tf), lambda i, j: (layer, 0, nff + j)),
        pl.BlockSpec((None, tf, d), lambda i, j: (layer, j, 0)),
    ]
    args = [h, g.reshape(1, d), w_in, w_in, w_out]
    if final:
        in_specs.append(pl.BlockSpec((1, d), lambda i, j: (0, 0)))
        args.append(final_g.reshape(1, d))
    vmem = 3 * tm * d * 4 + tm * d * 2 + 2 * (2 * d * tf + tf * d) * 4 + (8 << 20)
    return pl.pallas_call(
        functools.partial(_ffn_body, final=final),
        grid=(n // tm, nff),
        in_specs=in_specs,
        out_specs=pl.BlockSpec((tm, d), lambda i, j: (i, 0)),
        out_shape=jax.ShapeDtypeStruct((n, d), F32),
        scratch_shapes=[pltpu.VMEM((tm, d), BF16), pltpu.VMEM((tm, d), F32), pltpu.SemaphoreType.DMA(())],
        compiler_params=_params(("arbitrary", "arbitrary"), vmem),
        name="ffn",
    )(*args)


PROJ_ROWS = 1024
PROJ_COLS = 1024


def _proj_body(x_ref, g_ref, w_ref, cos_ref, sin_ref, o_ref, xn_ref, *, q_tiles, q_scale):
    j = pl.program_id(1)

    @pl.when(j == 0)
    def _():
        xn_ref[...] = _rms(x_ref[...], g_ref[...], NORM_EPS).astype(BF16)

    y = jnp.dot(xn_ref[...], w_ref[...], preferred_element_type=F32)
    scale = jnp.where(j < q_tiles, q_scale, 1.0).astype(F32)
    cos = cos_ref[...] * scale
    sin = sin_ref[...] * scale
    for c in range(y.shape[1] // HEAD_DIM):
        yh = y[:, c * HEAD_DIM:(c + 1) * HEAD_DIM]
        rot = yh * cos + pltpu.roll(yh, HALF_DIM, 1) * sin
        o_ref[:, c * HEAD_DIM:(c + 1) * HEAD_DIM] = rot.astype(o_ref.dtype)


def _proj(h, g, w, cos, sin, col0, q_width, rope_width, width, q_scale, out_dtype):
    n, d = h.shape
    seq = cos.shape[1]
    tm, tn = PROJ_ROWS, PROJ_COLS
    assert n % tm == 0 and seq % tm == 0
    assert all(c % tn == 0 for c in (col0, q_width, rope_width, width))
    pos_blocks = seq // tm
    rope_tiles = rope_width // tn
    body = functools.partial(_proj_body, q_tiles=q_width // tn, q_scale=q_scale)

    def table_spec():
        return pl.BlockSpec((None, tm, HEAD_DIM),
                            lambda i, j: (jnp.where(j < rope_tiles, 0, 1), i % pos_blocks, 0))

    vmem = 2 * tm * d * 4 + tm * d * 2 + 2 * d * tn * 2 + 4 * tm * tn * 4 + 4 * tm * HEAD_DIM * 4 + (8 << 20)
    return pl.pallas_call(
        body,
        grid=(n // tm, width // tn),
        in_specs=[
            pl.BlockSpec((tm, d), lambda i, j: (i, 0)),
            pl.BlockSpec((1, d), lambda i, j: (0, 0)),
            pl.BlockSpec((d, tn), lambda i, j: (0, col0 // tn + j)),
            table_spec(),
            table_spec(),
        ],
        out_specs=pl.BlockSpec((tm, tn), lambda i, j: (i, j)),
        out_shape=jax.ShapeDtypeStruct((n, width), out_dtype),
        scratch_shapes=[pltpu.VMEM((tm, d), BF16)],
        compiler_params=_params(("parallel", "arbitrary"), vmem),
        name="proj",
    )(h, g.reshape(1, d), w, cos, sin)


PROJ_WIDE_TILE = 512 * 3072


def _proj_wide_body(x_ref, g_ref, w_ref, cos_ref, sin_ref, o_ref, *, q_heads, k_heads, q_scale):
    xn = _rms(x_ref[...], g_ref[...], NORM_EPS).astype(BF16)
    y = jnp.dot(xn, w_ref[...], preferred_element_type=F32)
    cos, sin = cos_ref[...], sin_ref[...]
    cos_q, sin_q = cos * q_scale, sin * q_scale
    for c in range(y.shape[1] // HEAD_DIM):
        yh = y[:, c * HEAD_DIM:(c + 1) * HEAD_DIM]
        if c < q_heads:
            yh = yh * cos_q + pltpu.roll(yh, HALF_DIM, 1) * sin_q
        elif c < q_heads + k_heads:
            yh = yh * cos + pltpu.roll(yh, HALF_DIM, 1) * sin
        o_ref[:, c * HEAD_DIM:(c + 1) * HEAD_DIM] = yh.astype(o_ref.dtype)


def _proj_wide(h, g, w, cos, sin, col0, q_width, rope_width, width, q_scale, out_dtype):
    n, d = h.shape
    seq = cos.shape[1]
    tm = PROJ_WIDE_TILE // width
    assert n % tm == 0 and seq % tm == 0 and col0 % width == 0
    assert q_width % HEAD_DIM == 0 and rope_width % HEAD_DIM == 0 and width % HEAD_DIM == 0
    pos_blocks = seq // tm
    body = functools.partial(_proj_wide_body, q_heads=q_width // HEAD_DIM,
                             k_heads=(rope_width - q_width) // HEAD_DIM, q_scale=q_scale)
    table_spec = pl.BlockSpec((None, tm, HEAD_DIM), lambda i: (0, i % pos_blocks, 0))
    vmem = 2 * tm * d * 4 + tm * d * 2 + d * width * 2 + 3 * tm * width * 4 + 4 * tm * HEAD_DIM * 4 + (8 << 20)
    return pl.pallas_call(
        body,
        grid=(n // tm,),
        in_specs=[
            pl.BlockSpec((tm, d), lambda i: (i, 0)),
            pl.BlockSpec((1, d), lambda i: (0, 0)),
            pl.BlockSpec((d, width), lambda i: (0, col0 // width)),
            table_spec,
            table_spec,
        ],
        out_specs=pl.BlockSpec((tm, width), lambda i: (i, 0)),
        out_shape=jax.ShapeDtypeStruct((n, width), out_dtype),
        compiler_params=_params(("parallel",), vmem),
        name="proj_wide",
    )(h, g.reshape(1, d), w, cos, sin)


def _rope_tables(seq):
    inv_freq = ROPE_THETA ** (-jnp.arange(HALF_DIM, dtype=F32) / HALF_DIM)
    ang = jnp.arange(seq, dtype=F32)[:, None] * inv_freq[None, :]
    cos, sin = jnp.cos(ang), jnp.sin(ang)
    cos = jnp.concatenate([cos, cos], axis=-1)
    sin = jnp.concatenate([-sin, sin], axis=-1)
    return jnp.stack([cos, jnp.ones_like(cos)]), jnp.stack([sin, jnp.zeros_like(sin)])


OPROJ_ROWS = 512
OPROJ_COLS = 2048


def _oproj_body(*refs):
    *aw, h_ref, o_ref = refs
    acc = h_ref[...]
    for a_ref, w_ref in zip(aw[0::2], aw[1::2]):
        acc = acc + jnp.dot(a_ref[...], w_ref[...], preferred_element_type=F32)
    o_ref[...] = acc


def _oproj(h, parts, w):
    n, d = h.shape
    tm, tn = OPROJ_ROWS, OPROJ_COLS
    assert n % tm == 0 and d % tn == 0
    in_specs, args, vmem, row0 = [], [], 0, 0
    for a in parts:
        kdim = a.shape[1]
        assert row0 % kdim == 0
        in_specs += [pl.BlockSpec((tm, kdim), lambda i, j: (i, 0)),
                     pl.BlockSpec((kdim, tn), lambda i, j, r=row0 // kdim: (r, j))]
        args += [a, w]
        vmem += 2 * (tm * kdim + kdim * tn) * 2
        row0 += kdim
    assert row0 == w.shape[0]
    in_specs.append(pl.BlockSpec((tm, tn), lambda i, j: (i, j)))
    args.append(h)
    vmem += 6 * tm * tn * 4 + (8 << 20)
    return pl.pallas_call(
        _oproj_body,
        grid=(n // tm, d // tn),
        in_specs=in_specs,
        out_specs=pl.BlockSpec((tm, tn), lambda i, j: (i, j)),
        out_shape=jax.ShapeDtypeStruct((n, d), F32),
        compiler_params=_params(("parallel", "parallel"), vmem),
        name="oproj",
    )(*args)


def _dilated_body(q_ref, kc_ref, kp_ref, vc_ref, vp_ref, o_ref, kbuf, vbuf, *branch_bufs):
    obufs, lbufs = branch_bufs[:3], branch_bufs[3:]
    sb = pl.program_id(2)
    sup = DIL_SUPER
    kbuf[0:sup, :] = kp_ref[...]
    kbuf[sup:2 * sup, :] = kc_ref[...]
    vbuf[0:sup, :] = vp_ref[...]
    vbuf[sup:2 * sup, :] = vc_ref[...]

    row = lax.broadcasted_iota(jnp.int32, (DIL_BLOCK, 2 * DIL_BLOCK), 0)
    col = lax.broadcasted_iota(jnp.int32, (DIL_BLOCK, 2 * DIL_BLOCK), 1)
    band = jnp.logical_and(col >= row + (DIL_BLOCK - DIL_REACH), col <= row + DIL_BLOCK)

    for br, dil in enumerate(DILATIONS):
        nsub = sup // (DIL_BLOCK * dil)
        obuf, lbuf = obufs[br], lbufs[br]

        def block(t, carry, dil=dil, nsub=nsub, obuf=obuf, lbuf=lbuf):
            res = t // nsub
            sub = t % nsub
            q0 = res + dil * DIL_BLOCK * sub
            k0 = sup + q0 - dil * DIL_BLOCK
            if dil == 1:
                q0 = pl.multiple_of(q0, DIL_BLOCK)
                k0 = pl.multiple_of(k0, DIL_BLOCK)
                qs, ks = pl.ds(q0, DIL_BLOCK), pl.ds(k0, 2 * DIL_BLOCK)
            else:
                qs, ks = pl.ds(q0, DIL_BLOCK, stride=dil), pl.ds(k0, 2 * DIL_BLOCK, stride=dil)
            qb = q_ref[qs, :].astype(BF16)
            kb = kbuf[ks, :].astype(BF16)
            vb = vbuf[ks, :].astype(BF16)
            s = lax.dot_general(qb, kb, NT_DIMS, preferred_element_type=F32)
            has_prev = jnp.logical_or(sb > 0, sub > 0)
            first_col = jnp.where(has_prev, 0, DIL_BLOCK)
            s = jnp.where(jnp.logical_and(band, col >= first_col), s, NEG_INF)
            m = jnp.max(s, axis=-1, keepdims=True)
            p = jnp.exp2((s - m).astype(BF16))
            pv = jnp.dot(p, jnp.concatenate([vb, jnp.ones_like(vb)], axis=1), preferred_element_type=F32)
            den = pv[:, HEAD_DIM:]
            obuf[qs, :] = pv[:, :HEAD_DIM] / den
            lbuf[qs, :] = m + jnp.log2(den)
            return carry

        lax.fori_loop(0, nsub * dil, block, 0, unroll=DIL_UNROLL)

    lses = [lbuf[...] for lbuf in lbufs]
    top = jnp.maximum(jnp.maximum(lses[0], lses[1]), lses[2])
    wts = [jnp.exp2(l - top) for l in lses]
    mix = wts[0] * obufs[0][...] + wts[1] * obufs[1][...] + wts[2] * obufs[2][...]
    o_ref[...] = (mix / (wts[0] + wts[1] + wts[2])).astype(o_ref.dtype)


def _dilated(qkv, batch, seq):
    sup = DIL_SUPER
    assert seq % sup == 0
    nsb = seq // sup
    heads = N_HEADS_A

    def cur(col0):
        return pl.BlockSpec((sup, HEAD_DIM), lambda b, h, s: (b * nsb + s, col0 + h))

    def prev(col0):
        return pl.BlockSpec((sup, HEAD_DIM), lambda b, h, s: (b * nsb + jnp.maximum(s - 1, 0), col0 + h))

    blk = sup * HEAD_DIM * 4
    vmem = 2 * 5 * blk + 2 * sup * HEAD_DIM * 2 + 4 * blk + 6 * blk + 8 * blk + (8 << 20)
    return pl.pallas_call(
        _dilated_body,
        grid=(batch, heads, nsb),
        in_specs=[cur(0), cur(heads), prev(heads), cur(2 * heads), prev(2 * heads)],
        out_specs=pl.BlockSpec((sup, HEAD_DIM), lambda b, h, s: (b * nsb + s, h)),
        out_shape=jax.ShapeDtypeStruct((batch * seq, WIDTH_A), BF16),
        scratch_shapes=[pltpu.VMEM((2 * sup, HEAD_DIM), F32)] * 2 + [pltpu.VMEM((sup, HEAD_DIM), F32)] * 6,
        compiler_params=_params(("parallel", "parallel", "arbitrary"), vmem),
        name="dilated",
    )(qkv, qkv, qkv, qkv, qkv)


def _scores_t(k_blk, q):
    return lax.dot_general(k_blk, q, NT_DIMS, preferred_element_type=F32)


def _flash_t(streams, qi, tk, halves, diag_mask, past_mask):
    assert halves in (1, 2)
    tq = halves * tk
    first = halves * qi

    def key_tile(k_ref, j):
        return k_ref[pl.ds(pl.multiple_of(j * tk, tk), tk), :]

    for q, k_ref, _, p_ref, acc_ref, st_ref in streams:
        s = jnp.concatenate(
            [diag_mask(_scores_t(key_tile(k_ref, first + half), q[half * tk:(half + 1) * tk, :]), half)
             for half in range(halves)], axis=1).astype(BF16)
        m = jnp.max(s, axis=0, keepdims=True)
        p_ref[0] = jnp.exp2(s - m)
        acc_ref[...] = jnp.zeros(acc_ref.shape, F32)
        st_ref[0] = m.astype(F32)
        st_ref[1] = jnp.ones(m.shape, F32)

    def flush_own():
        for _, _, vt_ref, p_ref, acc_ref, st_ref in streams:
            pv = jnp.concatenate(
                [jnp.dot(vt_ref[first + half], p_ref[0, :, half * tk:(half + 1) * tk],
                         preferred_element_type=F32) for half in range(halves)], axis=1)
            acc_ref[...] = st_ref[1] * acc_ref[...] + pv

    def flush(pending, slot):
        for _, _, vt_ref, p_ref, acc_ref, st_ref in streams:
            acc_ref[...] = st_ref[1] * acc_ref[...] + jnp.dot(vt_ref[pending], p_ref[slot],
                                                              preferred_element_type=F32)

    def step(j, do_flush, wr, mask):
        scores = [_scores_t(key_tile(st[1], j), st[0]) for st in streams]
        do_flush()
        for s, (_, _, _, p_ref, _, st_ref) in zip(scores, streams):
            s = mask(s, j).astype(BF16)
            m = st_ref[0]
            m_new = jnp.maximum(m, jnp.max(s, axis=0, keepdims=True).astype(F32))
            p_ref[wr] = jnp.exp2(s - m_new.astype(BF16))
            st_ref[0] = m_new
            st_ref[1] = jnp.exp2(m - m_new)

    if halves == 2:
        def second_half_only(s, j):
            qry = lax.broadcasted_iota(jnp.int32, (tk, tq), 1)
            return jnp.where(qry >= tk, past_mask(s, j), NEG_INF)

        step(first, flush_own, 1, second_half_only)
    slot0 = halves - 1
    n_past = halves * qi

    def past_step(j, parity):
        rd = slot0 ^ parity
        step(j, lambda: flush(jnp.where(j == 0, first, j - 1), rd), 1 - rd, past_mask)

    def four(i, carry):
        for u in range(PAST_UNROLL):
            past_step(PAST_UNROLL * i + u, u % 2)
        return carry

    lax.fori_loop(0, n_past // PAST_UNROLL, four, 0)
    rest = n_past % PAST_UNROLL
    base = n_past - rest

    @pl.when(rest >= 2)
    def _():
        past_step(base, 0)
        past_step(base + 1, 1)

    last = jnp.where(n_past == 0, first, n_past - 1)
    if halves == 2:
        flush(last, slot0)
    else:
        @pl.when(rest % 2 == 1)
        def _():
            past_step(n_past - 1, 0)
            flush(last, 1 - slot0)

        @pl.when(rest % 2 == 0)
        def _():
            flush(last, slot0)

    outs = []
    for _, _, _, _, acc_ref, _ in streams:
        dv = acc_ref.shape[0] - ONES_ROWS
        outs.append((acc_ref[0:dv, :], acc_ref[dv:dv + 1, :]))
    return outs


def _store_transposed_blocks(v_ref, vt_ref):
    nblk, rows_t, rows = vt_ref.shape
    dv = rows_t - ONES_ROWS
    for n in range(nblk):
        vt_ref[n, 0:dv, :] = v_ref[n * rows:(n + 1) * rows, :].astype(F32).T.astype(vt_ref.dtype)
        vt_ref[n, dv:rows_t, :] = jnp.ones((ONES_ROWS, rows), vt_ref.dtype)


def _moba_body(q_ref, k_ref, v_ref, o_ref, kmean_hi, kmean_lo, vt_ref, sel_ref, p_ref, acc_ref, st_ref):
    qi = pl.program_id(2)
    blk, tk, tq = MOBA_BLOCK, ATTN_KEYS, MOBA_HALVES * ATTN_KEYS
    nb = k_ref.shape[0] // blk

    @pl.when(qi == 0)
    def _():
        for n in range(nb):
            mean = jnp.mean(k_ref[n * blk:(n + 1) * blk, :].astype(F32), axis=0, keepdims=True)
            hi = mean.astype(BF16)
            kmean_hi[n:n + 1, :] = hi
            kmean_lo[n:n + 1, :] = (mean - hi.astype(F32)).astype(BF16)
        _store_transposed_blocks(v_ref, vt_ref)

    q = q_ref[...]
    gate = _scores_t(kmean_hi[...], q) + _scores_t(kmean_lo[...], q)
    blk_id = lax.broadcasted_iota(jnp.int32, gate.shape, 0)
    q_blk = (tq // blk) * qi + lax.broadcasted_iota(jnp.int32, gate.shape, 1) // blk
    past = blk_id < q_blk
    work = jnp.where(past, gate, NEG_INF)
    chosen = jnp.zeros(gate.shape, F32)
    for _ in range(MOBA_TOPK):
        best = jnp.max(work, axis=0, keepdims=True)
        first = jnp.min(jnp.where(work == best, blk_id, nb), axis=0, keepdims=True)
        pick = blk_id == first
        chosen = jnp.where(pick, 1.0, chosen)
        work = jnp.where(pick, -jnp.inf, work)
    sel_ref[...] = jnp.where(past, chosen, 0.0)

    def picked(block):
        return sel_ref[pl.ds(block, 1), :] > 0.5

    def past_mask(s, j):
        return jnp.concatenate([jnp.where(picked(2 * j), s[:blk], NEG_INF),
                                jnp.where(picked(2 * j + 1), s[blk:], NEG_INF)], axis=0)

    def diag_mask(s, half):
        key = lax.broadcasted_iota(jnp.int32, (tk, tk), 0)
        qry = lax.broadcasted_iota(jnp.int32, (tk, tk), 1)
        first_picked = sel_ref[pl.ds(2 * (MOBA_HALVES * qi + half), 1), half * tk:(half + 1) * tk] > 0.5
        other = jnp.where(key < blk, jnp.where(first_picked, s, NEG_INF), NEG_INF)
        own_start = (qry // blk) * blk
        return jnp.where(key <= qry, jnp.where(key >= own_start, s, other), other)

    [(acc, l)] = _flash_t([(q, k_ref, vt_ref, p_ref, acc_ref, st_ref)], qi, tk, MOBA_HALVES, diag_mask, past_mask)
    o_ref[...] = (acc * (1.0 / l)).T.astype(o_ref.dtype)


def _moba(qkv, batch, seq):
    blk, tk, tq = MOBA_BLOCK, ATTN_KEYS, MOBA_HALVES * ATTN_KEYS
    assert seq % tq == 0
    nq = seq // tq
    nb = seq // blk
    heads = N_HEADS_B
    vmem = (2 * 2 * seq * HEAD_DIM * 2 + seq * HEAD_DIM * 2 + 4 * tq * HEAD_DIM * 2 + 2 * tk * tq * 2
            + 2 * HEAD_DIM * tq * 4 + 10 * tk * tq * 4 + (8 << 20))
    return pl.pallas_call(
        _moba_body,
        grid=(batch, heads, nq),
        in_specs=[
            pl.BlockSpec((tq, HEAD_DIM), lambda b, h, i: (b * nq + i, h)),
            pl.BlockSpec((seq, HEAD_DIM), lambda b, h, i: (b, heads + h)),
            pl.BlockSpec((seq, HEAD_DIM), lambda b, h, i: (b, 2 * heads + h)),
        ],
        out_specs=pl.BlockSpec((tq, HEAD_DIM), lambda b, h, i: (b * nq + i, h)),
        out_shape=jax.ShapeDtypeStruct((batch * seq, WIDTH_B), BF16),
        scratch_shapes=[pltpu.VMEM((nb, HEAD_DIM), BF16)] * 2
        + [pltpu.VMEM((seq // tk, HEAD_DIM + ONES_ROWS, tk), BF16), pltpu.VMEM((nb, tq), F32),
           pltpu.VMEM((2, tk, tq), BF16), pltpu.VMEM((HEAD_DIM + ONES_ROWS, tq), F32),
           pltpu.VMEM((2, 1, tq), F32)],
        compiler_params=_params(("parallel", "parallel", "arbitrary"), vmem),
        name="moba",
    )(qkv, qkv, qkv)


def _diff_body(lq1_ref, lk1_ref, lq2_ref, lk2_ref, g_ref, q1_ref, q2_ref, k1_ref, k2_ref, v_ref, o_ref,
               vt_ref, p1_ref, p2_ref, acc1_ref, acc2_ref, st1_ref, st2_ref, *, lambda_init):
    qi = pl.program_id(2)
    t = ATTN_KEYS

    @pl.when(qi == 0)
    def _():
        _store_transposed_blocks(v_ref, vt_ref)

    def diag_mask(s, half):
        key = lax.broadcasted_iota(jnp.int32, (t, t), 0)
        qry = lax.broadcasted_iota(jnp.int32, (t, t), 1)
        return jnp.where(key <= qry, s, NEG_INF)

    streams = [(q1_ref[...], k1_ref, vt_ref, p1_ref, acc1_ref, st1_ref),
               (q2_ref[...], k2_ref, vt_ref, p2_ref, acc2_ref, st2_ref)]
    (a1, l1), (a2, l2) = _flash_t(streams, qi, t, DIFF_HALVES, diag_mask, lambda s, j: s)

    lam = (jnp.exp(jnp.sum(lq1_ref[...] * lk1_ref[...], axis=-1, keepdims=True))
           - jnp.exp(jnp.sum(lq2_ref[...] * lk2_ref[...], axis=-1, keepdims=True)) + lambda_init)
    o = a1 * (1.0 / l1) - lam * (a2 * (1.0 / l2))
    o = o * lax.rsqrt(jnp.mean(o * o, axis=0, keepdims=True) + SUBLN_EPS)
    o_ref[...] = (o.T * g_ref[...] * (1.0 - lambda_init)).astype(o_ref.dtype)


def _diff(qkv, lq1, lk1, lq2, lk2, subln_g, lambda_init, batch, seq):
    t, tq = ATTN_KEYS, DIFF_HALVES * ATTN_KEYS
    assert seq % tq == 0
    nq = seq // tq
    heads = N_HEADS_C
    dv = 2 * HEAD_DIM
    kcol = D_MODEL // HEAD_DIM
    vcol = 2 * D_MODEL // dv

    def vec(width):
        return pl.BlockSpec((1, width), lambda b, h, i: (0, 0))

    def q_spec(part):
        return pl.BlockSpec((tq, HEAD_DIM), lambda b, h, i: (b * nq + i, 2 * h + part))

    def k_spec(part):
        return pl.BlockSpec((seq, HEAD_DIM), lambda b, h, i: (b, kcol + 2 * h + part))

    vmem = (2 * (2 * seq * HEAD_DIM + seq * dv) * 2 + seq * dv * 2 + 8 * tq * HEAD_DIM * 2 + 2 * tq * dv * 2
            + 2 * tq * dv * 4 + 4 * t * tq * 2 + 10 * t * tq * 4 + (8 << 20))
    return pl.pallas_call(
        functools.partial(_diff_body, lambda_init=lambda_init),
        grid=(batch, heads, nq),
        in_specs=[vec(HEAD_DIM)] * 4 + [vec(dv), q_spec(0), q_spec(1), k_spec(0), k_spec(1),
                                        pl.BlockSpec((seq, dv), lambda b, h, i: (b, vcol + h))],
        out_specs=pl.BlockSpec((tq, dv), lambda b, h, i: (b * nq + i, h)),
        out_shape=jax.ShapeDtypeStruct((batch * seq, heads * dv), BF16),
        scratch_shapes=[pltpu.VMEM((seq // t, dv + ONES_ROWS, t), BF16)] + [pltpu.VMEM((2, t, tq), BF16)] * 2
        + [pltpu.VMEM((dv + ONES_ROWS, tq), F32)] * 2 + [pltpu.VMEM((2, 1, tq), F32)] * 2,
        compiler_params=_params(("parallel", "parallel", "arbitrary"), vmem),
        name="diffattn",
    )(lq1.reshape(1, -1), lk1.reshape(1, -1), lq2.reshape(1, -1), lk2.reshape(1, -1),
      subln_g.reshape(1, -1), qkv, qkv, qkv, qkv, qkv)


def _lambda_init(layer):
    return 0.8 - 0.6 * math.exp(-0.3 * layer)


def kernel(x, ffa_norm, ffa_w_in, ffa_w_out, mix_norm, even_w_in, even_w_out, odd_w_in, odd_w_out,
           lambda_q1, lambda_k1, lambda_q2, lambda_k2, subln_norm, ffb_norm, ffb_w_in, ffb_w_out, final_norm):
    batch, seq, d = x.shape
    depth = ffa_norm.shape[0]
    cos, sin = _rope_tables(seq)
    h = x.reshape(batch * seq, d)
    for layer in range(depth):
        i = layer // 2
        h = _ffn(h, ffa_norm[layer], ffa_w_in, ffa_w_out, layer)
        if layer % 2 == 0:
            w_in = even_w_in[i].astype(BF16)
            wa, wb = 3 * WIDTH_A, 3 * WIDTH_B
            qkv_a = _proj_wide(h, mix_norm[layer], w_in, cos, sin, 0, WIDTH_A, 2 * WIDTH_A, wa, SCALE * LOG2_E, F32)
            qkv_b = _proj_wide(h, mix_norm[layer], w_in, cos, sin, wa, WIDTH_B, 2 * WIDTH_B, wb, SCALE * LOG2_E, BF16)
            o_a = _dilated(qkv_a, batch, seq)
            o_b = _moba(qkv_b, batch, seq)
            h = _oproj(h, [o_a, o_b], even_w_out[i].astype(BF16))
        else:
            qkv = _proj_wide(h, mix_norm[layer], odd_w_in[i].astype(BF16), cos, sin,
                             0, D_MODEL, 2 * D_MODEL, 3 * D_MODEL, SCALE * LOG2_E, BF16)
            o = _diff(qkv, lambda_q1[i], lambda_k1[i], lambda_q2[i], lambda_k2[i], subln_norm[i],
                      _lambda_init(layer), batch, seq)
            h = _oproj(h, [o], odd_w_out[i].astype(BF16))
        last = layer == depth - 1
        h = _ffn(h, ffb_norm[layer], ffb_w_in, ffb_w_out, layer, final_norm if last else None)
    return h.reshape(batch, seq, d)
```

```python
import functools
import math

import jax
import jax.numpy as jnp
from jax import lax
from jax.experimental import pallas as pl
from jax.experimental.pallas import tpu as pltpu

F32 = jnp.float32
BF16 = jnp.bfloat16

D_MODEL = 2048
HEAD_DIM = 128
HALF_DIM = HEAD_DIM // 2
N_HEADS_A = 8
N_HEADS_B = 8
N_HEADS_C = 8
WIDTH_A = N_HEADS_A * HEAD_DIM
WIDTH_B = N_HEADS_B * HEAD_DIM
DILATIONS = (1, 4, 16)
DIL_BLOCK = 128
DIL_REACH = 128
DIL_SUPER = DIL_BLOCK * DILATIONS[-1]
DIL_UNROLL = 16
MOBA_BLOCK = 256
MOBA_TOPK = 3
ATTN_KEYS = 512
MOBA_HALVES = 2
DIFF_HALVES = 1
ROPE_THETA = 10000.0
NORM_EPS = 1e-6
SUBLN_EPS = 1e-5
NEG_INF = -1e30
SCALE = HEAD_DIM ** -0.5
LOG2_E = math.log2(math.e)

VMEM_V7X_BYTES = 64 * 1024 * 1024
NT_DIMS = (((1,), (1,)), ((), ()))
ONES_ROWS = 16
PAST_UNROLL = 4


def _params(semantics, vmem_bytes):
    assert vmem_bytes < VMEM_V7X_BYTES
    return pltpu.CompilerParams(dimension_semantics=semantics, vmem_limit_bytes=int(vmem_bytes))


def _rms(x, g, eps):
    return x * lax.rsqrt(jnp.mean(x * x, axis=-1, keepdims=True) + eps) * g


FFN_ROWS = 1024
FFN_COLS = 512


def _ffn_body(h_hbm, g_ref, wg_ref, wu_ref, wo_ref, *rest, final):
    if final:
        fg_ref, o_ref, xn_ref, x_buf, x_sem = rest
    else:
        o_ref, xn_ref, x_buf, x_sem = rest
    i, j = pl.program_id(0), pl.program_id(1)
    rows = x_buf.shape[0]

    def x_copy(tile):
        return pltpu.make_async_copy(h_hbm.at[pl.ds(tile * rows, rows), :], x_buf, x_sem)

    @pl.when(j == 0)
    def _():
        @pl.when(i == 0)
        def _():
            x_copy(i).start()

        x_copy(i).wait()
        x = x_buf[...]
        xn_ref[...] = _rms(x, g_ref[...], NORM_EPS).astype(BF16)
        o_ref[...] = x

    @pl.when(jnp.logical_and(j == 1, i + 1 < pl.num_programs(0)))
    def _():
        x_copy(i + 1).start()

    xn = xn_ref[...]
    gate = jnp.dot(xn, wg_ref[...].astype(BF16), preferred_element_type=F32)
    up = jnp.dot(xn, wu_ref[...].astype(BF16), preferred_element_type=F32)
    act = (0.5 * gate / (1.0 + jnp.exp(-gate))) * up
    o_ref[...] += jnp.dot(act.astype(BF16), wo_ref[...].astype(BF16), preferred_element_type=F32)

    if final:
        @pl.when(j == pl.num_programs(1) - 1)
        def _():
            o_ref[...] = _rms(o_ref[...], fg_ref[...], NORM_EPS)


def _ffn(h, g, w_in, w_out, layer, final_g=None):
    n, d = h.shape
    d_ff = w_out.shape[1]
    tm, tf = FFN_ROWS, FFN_COLS
    nff = d_ff // tf
    assert n % tm == 0 and d_ff % tf == 0 and nff >= 2
    final = final_g is not None
    in_specs = [
        pl.BlockSpec(memory_space=pl.ANY),
        pl.BlockSpec((1, d), lambda i, j: (0, 0)),
        pl.BlockSpec((None, d, tf), lambda i, j: (layer, 0, j)),
        pl.BlockSpec((None, d, tf), lambda i, j: (layer, 0, nff + j)),
        pl.BlockSpec((None, tf, d), lambda i, j: (layer, j, 0)),
    ]
    args = [h, g.reshape(1, d), w_in, w_in, w_out]
    if final:
        in_specs.append(pl.BlockSpec((1, d), lambda i, j: (0, 0)))
        args.append(final_g.reshape(1, d))
    vmem = 3 * tm * d * 4 + tm * d * 2 + 2 * (2 * d * tf + tf * d) * 4 + (8 << 20)
    return pl.pallas_call(
        functools.partial(_ffn_body, final=final),
        grid=(n // tm, nff),
        in_specs=in_specs,
        out_specs=pl.BlockSpec((tm, d), lambda i, j: (i, 0)),
        out_shape=jax.ShapeDtypeStruct((n, d), F32),
        scratch_shapes=[pltpu.VMEM((tm, d), BF16), pltpu.VMEM((tm, d), F32), pltpu.SemaphoreType.DMA(())],
        compiler_params=_params(("arbitrary", "arbitrary"), vmem),
        name="ffn",
    )(*args)


PROJ_TILE = 512 * 3072


def _proj_body(x_ref, g_ref, w_ref, cos_ref, sin_ref, o_ref, *, q_heads, k_heads, q_scale):
    xn = _rms(x_ref[...], g_ref[...], NORM_EPS).astype(BF16)
    y = jnp.dot(xn, w_ref[...], preferred_element_type=F32)
    cos, sin = cos_ref[...], sin_ref[...]
    cos_q, sin_q = cos * q_scale, sin * q_scale
    for c in range(y.shape[1] // HEAD_DIM):
        yh = y[:, c * HEAD_DIM:(c + 1) * HEAD_DIM]
        if c < q_heads:
            yh = yh * cos_q + pltpu.roll(yh, HALF_DIM, 1) * sin_q
        elif c < q_heads + k_heads:
            yh = yh * cos + pltpu.roll(yh, HALF_DIM, 1) * sin
        o_ref[:, c * HEAD_DIM:(c + 1) * HEAD_DIM] = yh.astype(o_ref.dtype)


def _proj(h, g, w, cos, sin, col0, q_width, rope_width, width, q_scale, out_dtype):
    n, d = h.shape
    seq = cos.shape[0]
    tm = PROJ_TILE // width
    assert n % tm == 0 and seq % tm == 0 and col0 % width == 0
    assert q_width % HEAD_DIM == 0 and rope_width % HEAD_DIM == 0 and width % HEAD_DIM == 0
    pos_blocks = seq // tm
    body = functools.partial(_proj_body, q_heads=q_width // HEAD_DIM,
                             k_heads=(rope_width - q_width) // HEAD_DIM, q_scale=q_scale)
    table_spec = pl.BlockSpec((tm, HEAD_DIM), lambda i: (i % pos_blocks, 0))
    vmem = 2 * tm * d * 4 + tm * d * 2 + d * width * 2 + 3 * tm * width * 4 + 4 * tm * HEAD_DIM * 4 + (8 << 20)
    return pl.pallas_call(
        body,
        grid=(n // tm,),
        in_specs=[
            pl.BlockSpec((tm, d), lambda i: (i, 0)),
            pl.BlockSpec((1, d), lambda i: (0, 0)),
            pl.BlockSpec((d, width), lambda i: (0, col0 // width)),
            table_spec,
            table_spec,
        ],
        out_specs=pl.BlockSpec((tm, width), lambda i: (i, 0)),
        out_shape=jax.ShapeDtypeStruct((n, width), out_dtype),
        compiler_params=_params(("parallel",), vmem),
        name="proj",
    )(h, g.reshape(1, d), w, cos, sin)


def _rope_tables(seq):
    inv_freq = ROPE_THETA ** (-jnp.arange(HALF_DIM, dtype=F32) / HALF_DIM)
    ang = jnp.arange(seq, dtype=F32)[:, None] * inv_freq[None, :]
    cos, sin = jnp.cos(ang), jnp.sin(ang)
    return jnp.concatenate([cos, cos], axis=-1), jnp.concatenate([-sin, sin], axis=-1)


OPROJ_ROWS = 512
OPROJ_COLS = 2048


def _oproj_body(*refs):
    *aw, h_ref, o_ref = refs
    acc = h_ref[...]
    for a_ref, w_ref in zip(aw[0::2], aw[1::2]):
        acc = acc + jnp.dot(a_ref[...], w_ref[...], preferred_element_type=F32)
    o_ref[...] = acc


def _oproj(h, parts, w):
    n, d = h.shape
    tm, tn = OPROJ_ROWS, OPROJ_COLS
    assert n % tm == 0 and d % tn == 0
    in_specs, args, vmem, row0 = [], [], 0, 0
    for a in parts:
        kdim = a.shape[1]
        assert row0 % kdim == 0
        in_specs += [pl.BlockSpec((tm, kdim), lambda i, j: (i, 0)),
                     pl.BlockSpec((kdim, tn), lambda i, j, r=row0 // kdim: (r, j))]
        args += [a, w]
        vmem += 2 * (tm * kdim + kdim * tn) * 2
        row0 += kdim
    assert row0 == w.shape[0]
    in_specs.append(pl.BlockSpec((tm, tn), lambda i, j: (i, j)))
    args.append(h)
    vmem += 6 * tm * tn * 4 + (8 << 20)
    return pl.pallas_call(
        _oproj_body,
        grid=(n // tm, d // tn),
        in_specs=in_specs,
        out_specs=pl.BlockSpec((tm, tn), lambda i, j: (i, j)),
        out_shape=jax.ShapeDtypeStruct((n, d), F32),
        compiler_params=_params(("parallel", "parallel"), vmem),
        name="oproj",
    )(*args)


def _dilated_body(q_ref, kc_ref, kp_ref, vc_ref, vp_ref, o_ref, kbuf, vbuf, *branch_bufs):
    obufs, lbufs = branch_bufs[:3], branch_bufs[3:]
    sb = pl.program_id(2)
    sup = DIL_SUPER
    kbuf[0:sup, :] = kp_ref[...]
    kbuf[sup:2 * sup, :] = kc_ref[...]
    vbuf[0:sup, :] = vp_ref[...]
    vbuf[sup:2 * sup, :] = vc_ref[...]

    row = lax.broadcasted_iota(jnp.int32, (DIL_BLOCK, 2 * DIL_BLOCK), 0)
    col = lax.broadcasted_iota(jnp.int32, (DIL_BLOCK, 2 * DIL_BLOCK), 1)
    band = jnp.logical_and(col >= row + (DIL_BLOCK - DIL_REACH), col <= row + DIL_BLOCK)

    for br, dil in enumerate(DILATIONS):
        nsub = sup // (DIL_BLOCK * dil)
        obuf, lbuf = obufs[br], lbufs[br]

        def block(t, carry, dil=dil, nsub=nsub, obuf=obuf, lbuf=lbuf):
            res = t // nsub
            sub = t % nsub
            q0 = res + dil * DIL_BLOCK * sub
            k0 = sup + q0 - dil * DIL_BLOCK
            if dil == 1:
                q0 = pl.multiple_of(q0, DIL_BLOCK)
                k0 = pl.multiple_of(k0, DIL_BLOCK)
                qs, ks = pl.ds(q0, DIL_BLOCK), pl.ds(k0, 2 * DIL_BLOCK)
            else:
                qs, ks = pl.ds(q0, DIL_BLOCK, stride=dil), pl.ds(k0, 2 * DIL_BLOCK, stride=dil)
            qb = q_ref[qs, :].astype(BF16)
            kb = kbuf[ks, :].astype(BF16)
            vb = vbuf[ks, :].astype(BF16)
            s = lax.dot_general(qb, kb, NT_DIMS, preferred_element_type=F32)
            has_prev = jnp.logical_or(sb > 0, sub > 0)
            first_col = jnp.where(has_prev, 0, DIL_BLOCK)
            s = jnp.where(jnp.logical_and(band, col >= first_col), s, NEG_INF)
            m = jnp.max(s, axis=-1, keepdims=True)
            p = jnp.exp2((s - m).astype(BF16))
            pv = jnp.dot(p, jnp.concatenate([vb, jnp.ones_like(vb)], axis=1), preferred_element_type=F32)
            den = pv[:, HEAD_DIM:]
            obuf[qs, :] = pv[:, :HEAD_DIM] / den
            lbuf[qs, :] = m + jnp.log2(den)
            return carry

        lax.fori_loop(0, nsub * dil, block, 0, unroll=DIL_UNROLL)

    lses = [lbuf[...] for lbuf in lbufs]
    top = jnp.maximum(jnp.maximum(lses[0], lses[1]), lses[2])
    wts = [jnp.exp2(l - top) for l in lses]
    mix = wts[0] * obufs[0][...] + wts[1] * obufs[1][...] + wts[2] * obufs[2][...]
    o_ref[...] = (mix / (wts[0] + wts[1] + wts[2])).astype(o_ref.dtype)


def _dilated(qkv, batch, seq):
    sup = DIL_SUPER
    assert seq % sup == 0
    nsb = seq // sup
    heads = N_HEADS_A

    def cur(col0):
        return pl.BlockSpec((sup, HEAD_DIM), lambda b, h, s: (b * nsb + s, col0 + h))

    def prev(col0):
        return pl.BlockSpec((sup, HEAD_DIM), lambda b, h, s: (b * nsb + jnp.maximum(s - 1, 0), col0 + h))

    blk = sup * HEAD_DIM * 4
    vmem = 2 * 5 * blk + 2 * sup * HEAD_DIM * 2 + 4 * blk + 6 * blk + 8 * blk + (8 << 20)
    return pl.pallas_call(
        _dilated_body,
        grid=(batch, heads, nsb),
        in_specs=[cur(0), cur(heads), prev(heads), cur(2 * heads), prev(2 * heads)],
        out_specs=pl.BlockSpec((sup, HEAD_DIM), lambda b, h, s: (b * nsb + s, h)),
        out_shape=jax.ShapeDtypeStruct((batch * seq, WIDTH_A), BF16),
        scratch_shapes=[pltpu.VMEM((2 * sup, HEAD_DIM), F32)] * 2 + [pltpu.VMEM((sup, HEAD_DIM), F32)] * 6,
        compiler_params=_params(("parallel", "parallel", "arbitrary"), vmem),
        name="dilated",
    )(qkv, qkv, qkv, qkv, qkv)


def _scores_t(k_blk, q):
    return lax.dot_general(k_blk, q, NT_DIMS, preferred_element_type=F32)


def _flash_t(streams, qi, tk, halves, diag_mask, past_mask):
    assert halves in (1, 2)
    tq = halves * tk
    first = halves * qi

    def key_tile(k_ref, j):
        return k_ref[pl.ds(pl.multiple_of(j * tk, tk), tk), :]

    for q, k_ref, _, p_ref, acc_ref, st_ref in streams:
        s = jnp.concatenate(
            [diag_mask(_scores_t(key_tile(k_ref, first + half), q[half * tk:(half + 1) * tk, :]), half)
             for half in range(halves)], axis=1).astype(BF16)
        m = jnp.max(s, axis=0, keepdims=True)
        p_ref[0] = jnp.exp2(s - m)
        acc_ref[...] = jnp.zeros(acc_ref.shape, F32)
        st_ref[0] = m.astype(F32)
        st_ref[1] = jnp.ones(m.shape, F32)

    def flush_own():
        for _, _, vt_ref, p_ref, acc_ref, st_ref in streams:
            pv = jnp.concatenate(
                [jnp.dot(vt_ref[first + half], p_ref[0, :, half * tk:(half + 1) * tk],
                         preferred_element_type=F32) for half in range(halves)], axis=1)
            acc_ref[...] = st_ref[1] * acc_ref[...] + pv

    def flush(pending, slot):
        for _, _, vt_ref, p_ref, acc_ref, st_ref in streams:
            acc_ref[...] = st_ref[1] * acc_ref[...] + jnp.dot(vt_ref[pending], p_ref[slot],
                                                              preferred_element_type=F32)

    def step(j, do_flush, wr, mask):
        scores = [_scores_t(key_tile(st[1], j), st[0]) for st in streams]
        do_flush()
        for s, (_, _, _, p_ref, _, st_ref) in zip(scores, streams):
            s = mask(s, j).astype(BF16)
            m = st_ref[0]
            m_new = jnp.maximum(m, jnp.max(s, axis=0, keepdims=True).astype(F32))
            p_ref[wr] = jnp.exp2(s - m_new.astype(BF16))
            st_ref[0] = m_new
            st_ref[1] = jnp.exp2(m - m_new)

    if halves == 2:
        def second_half_only(s, j):
            qry = lax.broadcasted_iota(jnp.int32, (tk, tq), 1)
            return jnp.where(qry >= tk, past_mask(s, j), NEG_INF)

        step(first, flush_own, 1, second_half_only)
    slot0 = halves - 1
    n_past = halves * qi

    def past_step(j, parity):
        rd = slot0 ^ parity
        step(j, lambda: flush(jnp.where(j == 0, first, j - 1), rd), 1 - rd, past_mask)

    def four(i, carry):
        for u in range(PAST_UNROLL):
            past_step(PAST_UNROLL * i + u, u % 2)
        return carry

    lax.fori_loop(0, n_past // PAST_UNROLL, four, 0)
    rest = n_past % PAST_UNROLL
    base = n_past - rest

    @pl.when(rest >= 2)
    def _():
        past_step(base, 0)
        past_step(base + 1, 1)

    last = jnp.where(n_past == 0, first, n_past - 1)
    if halves == 2:
        flush(last, slot0)
    else:
        @pl.when(rest % 2 == 1)
        def _():
            past_step(n_past - 1, 0)
            flush(last, 1 - slot0)

        @pl.when(rest % 2 == 0)
        def _():
            flush(last, slot0)

    outs = []
    for _, _, _, _, acc_ref, _ in streams:
        dv = acc_ref.shape[0] - ONES_ROWS
        outs.append((acc_ref[0:dv, :], acc_ref[dv:dv + 1, :]))
    return outs


def _store_transposed_blocks(v_ref, vt_ref):
    nblk, rows_t, rows = vt_ref.shape
    dv = rows_t - ONES_ROWS
    for n in range(nblk):
        vt_ref[n, 0:dv, :] = v_ref[n * rows:(n + 1) * rows, :].astype(F32).T.astype(vt_ref.dtype)
        vt_ref[n, dv:rows_t, :] = jnp.ones((ONES_ROWS, rows), vt_ref.dtype)


def _moba_body(q_ref, k_ref, v_ref, o_ref, kmean_hi, kmean_lo, vt_ref, sel_ref, p_ref, acc_ref, st_ref):
    qi = pl.program_id(2)
    blk, tk, tq = MOBA_BLOCK, ATTN_KEYS, MOBA_HALVES * ATTN_KEYS
    nb = k_ref.shape[0] // blk

    @pl.when(qi == 0)
    def _():
        for n in range(nb):
            mean = jnp.mean(k_ref[n * blk:(n + 1) * blk, :].astype(F32), axis=0, keepdims=True)
            hi = mean.astype(BF16)
            kmean_hi[n:n + 1, :] = hi
            kmean_lo[n:n + 1, :] = (mean - hi.astype(F32)).astype(BF16)
        _store_transposed_blocks(v_ref, vt_ref)

    q = q_ref[...]
    gate = _scores_t(kmean_hi[...], q) + _scores_t(kmean_lo[...], q)
    blk_id = lax.broadcasted_iota(jnp.int32, gate.shape, 0)
    q_blk = (tq // blk) * qi + lax.broadcasted_iota(jnp.int32, gate.shape, 1) // blk
    past = blk_id < q_blk
    work = jnp.where(past, gate, NEG_INF)
    chosen = jnp.zeros(gate.shape, F32)
    for _ in range(MOBA_TOPK):
        best = jnp.max(work, axis=0, keepdims=True)
        first = jnp.min(jnp.where(work == best, blk_id, nb), axis=0, keepdims=True)
        pick = blk_id == first
        chosen = jnp.where(pick, 1.0, chosen)
        work = jnp.where(pick, -jnp.inf, work)
    sel_ref[...] = jnp.where(past, chosen, 0.0)

    def picked(block):
        return sel_ref[pl.ds(block, 1), :] > 0.5

    def past_mask(s, j):
        return jnp.concatenate([jnp.where(picked(2 * j), s[:blk], NEG_INF),
                                jnp.where(picked(2 * j + 1), s[blk:], NEG_INF)], axis=0)

    def diag_mask(s, half):
        key = lax.broadcasted_iota(jnp.int32, (tk, tk), 0)
        qry = lax.broadcasted_iota(jnp.int32, (tk, tk), 1)
        first_picked = sel_ref[pl.ds(2 * (MOBA_HALVES * qi + half), 1), half * tk:(half + 1) * tk] > 0.5
        other = jnp.where(key < blk, jnp.where(first_picked, s, NEG_INF), NEG_INF)
        own_start = (qry // blk) * blk
        return jnp.where(key <= qry, jnp.where(key >= own_start, s, other), other)

    [(acc, l)] = _flash_t([(q, k_ref, vt_ref, p_ref, acc_ref, st_ref)], qi, tk, MOBA_HALVES, diag_mask, past_mask)
    o_ref[...] = (acc * (1.0 / l)).T.astype(o_ref.dtype)


def _moba(qkv, batch, seq):
    blk, tk, tq = MOBA_BLOCK, ATTN_KEYS, MOBA_HALVES * ATTN_KEYS
    assert seq % tq == 0
    nq = seq // tq
    nb = seq // blk
    heads = N_HEADS_B
    vmem = (2 * 2 * seq * HEAD_DIM * 2 + seq * HEAD_DIM * 2 + 4 * tq * HEAD_DIM * 2 + 2 * tk * tq * 2
            + 2 * HEAD_DIM * tq * 4 + 10 * tk * tq * 4 + (8 << 20))
    return pl.pallas_call(
        _moba_body,
        grid=(batch, heads, nq),
        in_specs=[
            pl.BlockSpec((tq, HEAD_DIM), lambda b, h, i: (b * nq + i, h)),
            pl.BlockSpec((seq, HEAD_DIM), lambda b, h, i: (b, heads + h)),
            pl.BlockSpec((seq, HEAD_DIM), lambda b, h, i: (b, 2 * heads + h)),
        ],
        out_specs=pl.BlockSpec((tq, HEAD_DIM), lambda b, h, i: (b * nq + i, h)),
        out_shape=jax.ShapeDtypeStruct((batch * seq, WIDTH_B), BF16),
        scratch_shapes=[pltpu.VMEM((nb, HEAD_DIM), BF16)] * 2
        + [pltpu.VMEM((seq // tk, HEAD_DIM + ONES_ROWS, tk), BF16), pltpu.VMEM((nb, tq), F32),
           pltpu.VMEM((2, tk, tq), BF16), pltpu.VMEM((HEAD_DIM + ONES_ROWS, tq), F32),
           pltpu.VMEM((2, 1, tq), F32)],
        compiler_params=_params(("parallel", "parallel", "arbitrary"), vmem),
        name="moba",
    )(qkv, qkv, qkv)


def _diff_body(lq1_ref, lk1_ref, lq2_ref, lk2_ref, g_ref, q1_ref, q2_ref, k1_ref, k2_ref, v_ref, o_ref,
               vt_ref, p1_ref, p2_ref, acc1_ref, acc2_ref, st1_ref, st2_ref, *, lambda_init):
    qi = pl.program_id(2)
    t = ATTN_KEYS

    @pl.when(qi == 0)
    def _():
        _store_transposed_blocks(v_ref, vt_ref)

    def diag_mask(s, half):
        key = lax.broadcasted_iota(jnp.int32, (t, t), 0)
        qry = lax.broadcasted_iota(jnp.int32, (t, t), 1)
        return jnp.where(key <= qry, s, NEG_INF)

    streams = [(q1_ref[...], k1_ref, vt_ref, p1_ref, acc1_ref, st1_ref),
               (q2_ref[...], k2_ref, vt_ref, p2_ref, acc2_ref, st2_ref)]
    (a1, l1), (a2, l2) = _flash_t(streams, qi, t, DIFF_HALVES, diag_mask, lambda s, j: s)

    lam = (jnp.exp(jnp.sum(lq1_ref[...] * lk1_ref[...], axis=-1, keepdims=True))
           - jnp.exp(jnp.sum(lq2_ref[...] * lk2_ref[...], axis=-1, keepdims=True)) + lambda_init)
    o = a1 * (1.0 / l1) - lam * (a2 * (1.0 / l2))
    o = o * lax.rsqrt(jnp.mean(o * o, axis=0, keepdims=True) + SUBLN_EPS)
    o_ref[...] = (o.T * g_ref[...] * (1.0 - lambda_init)).astype(o_ref.dtype)


def _diff(qkv, lq1, lk1, lq2, lk2, subln_g, lambda_init, batch, seq):
    t, tq = ATTN_KEYS, DIFF_HALVES * ATTN_KEYS
    assert seq % tq == 0
    nq = seq // tq
    heads = N_HEADS_C
    dv = 2 * HEAD_DIM
    kcol = D_MODEL // HEAD_DIM
    vcol = 2 * D_MODEL // dv

    def vec(width):
        return pl.BlockSpec((1, width), lambda b, h, i: (0, 0))

    def q_spec(part):
        return pl.BlockSpec((tq, HEAD_DIM), lambda b, h, i: (b * nq + i, 2 * h + part))

    def k_spec(part):
        return pl.BlockSpec((seq, HEAD_DIM), lambda b, h, i: (b, kcol + 2 * h + part))

    vmem = (2 * (2 * seq * HEAD_DIM + seq * dv) * 2 + seq * dv * 2 + 8 * tq * HEAD_DIM * 2 + 2 * tq * dv * 2
            + 2 * tq * dv * 4 + 4 * t * tq * 2 + 10 * t * tq * 4 + (8 << 20))
    return pl.pallas_call(
        functools.partial(_diff_body, lambda_init=lambda_init),
        grid=(batch, heads, nq),
        in_specs=[vec(HEAD_DIM)] * 4 + [vec(dv), q_spec(0), q_spec(1), k_spec(0), k_spec(1),
                                        pl.BlockSpec((seq, dv), lambda b, h, i: (b, vcol + h))],
        out_specs=pl.BlockSpec((tq, dv), lambda b, h, i: (b * nq + i, h)),
        out_shape=jax.ShapeDtypeStruct((batch * seq, heads * dv), BF16),
        scratch_shapes=[pltpu.VMEM((seq // t, dv + ONES_ROWS, t), BF16)] + [pltpu.VMEM((2, t, tq), BF16)] * 2
        + [pltpu.VMEM((dv + ONES_ROWS, tq), F32)] * 2 + [pltpu.VMEM((2, 1, tq), F32)] * 2,
        compiler_params=_params(("parallel", "parallel", "arbitrary"), vmem),
        name="diffattn",
    )(lq1.reshape(1, -1), lk1.reshape(1, -1), lq2.reshape(1, -1), lk2.reshape(1, -1),
      subln_g.reshape(1, -1), qkv, qkv, qkv, qkv, qkv)


def _lambda_init(layer):
    return 0.8 - 0.6 * math.exp(-0.3 * layer)


def kernel(x, ffa_norm, ffa_w_in, ffa_w_out, mix_norm, even_w_in, even_w_out, odd_w_in, odd_w_out,
           lambda_q1, lambda_k1, lambda_q2, lambda_k2, subln_norm, ffb_norm, ffb_w_in, ffb_w_out, final_norm):
    batch, seq, d = x.shape
    depth = ffa_norm.shape[0]
    cos, sin = _rope_tables(seq)
    h = x.reshape(batch * seq, d)
    for layer in range(depth):
        i = layer // 2
        h = _ffn(h, ffa_norm[layer], ffa_w_in, ffa_w_out, layer)
        if layer % 2 == 0:
            w_in = even_w_in[i].astype(BF16)
            wa, wb = 3 * WIDTH_A, 3 * WIDTH_B
            qkv_a = _proj(h, mix_norm[layer], w_in, cos, sin, 0, WIDTH_A, 2 * WIDTH_A, wa, SCALE * LOG2_E, F32)
            qkv_b = _proj(h, mix_norm[layer], w_in, cos, sin, wa, WIDTH_B, 2 * WIDTH_B, wb, SCALE * LOG2_E, BF16)
            o_a = _dilated(qkv_a, batch, seq)
            o_b = _moba(qkv_b, batch, seq)
            h = _oproj(h, [o_a, o_b], even_w_out[i].astype(BF16))
        else:
            qkv = _proj(h, mix_norm[layer], odd_w_in[i].astype(BF16), cos, sin,
                        0, D_MODEL, 2 * D_MODEL, 3 * D_MODEL, SCALE * LOG2_E, BF16)
            o = _diff(qkv, lambda_q1[i], lambda_k1[i], lambda_q2[i], lambda_k2[i], subln_norm[i],
                      _lambda_init(layer), batch, seq)
            h = _oproj(h, [o], odd_w_out[i].astype(BF16))
        last = layer == depth - 1
        h = _ffn(h, ffb_norm[layer], ffb_w_in, ffb_w_out, layer, final_norm if last else None)
    return h.reshape(batch, seq, d)
```

```python
import functools
import math

import jax
import jax.numpy as jnp
from jax import lax
from jax.experimental import pallas as pl
from jax.experimental.pallas import tpu as pltpu

F32 = jnp.float32
BF16 = jnp.bfloat16

D_MODEL = 2048
HEAD_DIM = 128
HALF_DIM = HEAD_DIM // 2
N_HEADS_A = 8
N_HEADS_B = 8
N_HEADS_C = 8
WIDTH_A = N_HEADS_A * HEAD_DIM
WIDTH_B = N_HEADS_B * HEAD_DIM
DILATIONS = (1, 4, 16)
DIL_BLOCK = 128
DIL_REACH = 128
DIL_SUPER = DIL_BLOCK * DILATIONS[-1]
DIL_UNROLL = 16
MOBA_BLOCK = 256
MOBA_TOPK = 3
ATTN_KEYS = 512
MOBA_HALVES = 2
DIFF_HALVES = 1
ROPE_THETA = 10000.0
NORM_EPS = 1e-6
SUBLN_EPS = 1e-5
NEG_INF = -1e30
SCALE = HEAD_DIM ** -0.5
LOG2_E = math.log2(math.e)

VMEM_V7X_BYTES = 64 * 1024 * 1024
VMEM_TEMP_BYTES = 8 * 1024 * 1024
NT_DIMS = (((1,), (1,)), ((), ()))
ONES_ROWS = 16
PAST_UNROLL = 4


def _params(semantics, buffer_bytes):
    vmem_bytes = buffer_bytes + VMEM_TEMP_BYTES
    assert vmem_bytes < VMEM_V7X_BYTES
    return pltpu.CompilerParams(dimension_semantics=semantics, vmem_limit_bytes=int(vmem_bytes))


def _rms(x, g, eps):
    return x * lax.rsqrt(jnp.mean(x * x, axis=-1, keepdims=True) + eps) * g


FFN_ROWS = 1024
FFN_COLS = 512


def _ffn_body(h_hbm, g_ref, wg_ref, wu_ref, wo_ref, *rest, final):
    if final:
        fg_ref, o_ref, xn_ref, x_buf, x_sem = rest
    else:
        o_ref, xn_ref, x_buf, x_sem = rest
    i, j = pl.program_id(0), pl.program_id(1)
    rows = x_buf.shape[0]

    def x_copy(tile):
        return pltpu.make_async_copy(h_hbm.at[pl.ds(tile * rows, rows), :], x_buf, x_sem)

    @pl.when(j == 0)
    def _():
        @pl.when(i == 0)
        def _():
            x_copy(i).start()

        x_copy(i).wait()
        x = x_buf[...]
        xn_ref[...] = _rms(x, g_ref[...], NORM_EPS).astype(BF16)
        o_ref[...] = x

    @pl.when(jnp.logical_and(j == 1, i + 1 < pl.num_programs(0)))
    def _():
        x_copy(i + 1).start()

    xn = xn_ref[...]
    gate = jnp.dot(xn, wg_ref[...].astype(BF16), preferred_element_type=F32)
    up = jnp.dot(xn, wu_ref[...].astype(BF16), preferred_element_type=F32)
    act = (0.5 * gate / (1.0 + jnp.exp(-gate))) * up
    o_ref[...] += jnp.dot(act.astype(BF16), wo_ref[...].astype(BF16), preferred_element_type=F32)

    if final:
        @pl.when(j == pl.num_programs(1) - 1)
        def _():
            o_ref[...] = _rms(o_ref[...], fg_ref[...], NORM_EPS)


def _ffn(h, g, w_in, w_out, layer, final_g=None):
    n, d = h.shape
    d_ff = w_out.shape[1]
    tm, tf = FFN_ROWS, FFN_COLS
    nff = d_ff // tf
    assert n % tm == 0 and d_ff % tf == 0 and nff >= 2
    final = final_g is not None
    in_specs = [
        pl.BlockSpec(memory_space=pl.ANY),
        pl.BlockSpec((1, d), lambda i, j: (0, 0)),
        pl.BlockSpec((None, d, tf), lambda i, j: (layer, 0, j)),
        pl.BlockSpec((None, d, tf), lambda i, j: (layer, 0, nff + j)),
        pl.BlockSpec((None, tf, d), lambda i, j: (layer, j, 0)),
    ]
    args = [h, g.reshape(1, d), w_in, w_in, w_out]
    if final:
        in_specs.append(pl.BlockSpec((1, d), lambda i, j: (0, 0)))
        args.append(final_g.reshape(1, d))
    vmem = 3 * tm * d * 4 + tm * d * 2 + 2 * (2 * d * tf + tf * d) * 4
    return pl.pallas_call(
        functools.partial(_ffn_body, final=final),
        grid=(n // tm, nff),
        in_specs=in_specs,
        out_specs=pl.BlockSpec((tm, d), lambda i, j: (i, 0)),
        out_shape=jax.ShapeDtypeStruct((n, d), F32),
        scratch_shapes=[pltpu.VMEM((tm, d), BF16), pltpu.VMEM((tm, d), F32), pltpu.SemaphoreType.DMA(())],
        compiler_params=_params(("arbitrary", "arbitrary"), vmem),
        name="ffn",
    )(*args)


PROJ_TILE = 512 * 3072


def _proj_body(x_ref, g_ref, w_ref, cos_ref, sin_ref, o_ref, *, q_heads, k_heads, q_scale):
    xn = _rms(x_ref[...], g_ref[...], NORM_EPS).astype(BF16)
    y = jnp.dot(xn, w_ref[...], preferred_element_type=F32)
    cos, sin = cos_ref[...], sin_ref[...]
    cos_q, sin_q = cos * q_scale, sin * q_scale
    for c in range(y.shape[1] // HEAD_DIM):
        yh = y[:, c * HEAD_DIM:(c + 1) * HEAD_DIM]
        if c < q_heads:
            yh = yh * cos_q + pltpu.roll(yh, HALF_DIM, 1) * sin_q
        elif c < q_heads + k_heads:
            yh = yh * cos + pltpu.roll(yh, HALF_DIM, 1) * sin
        o_ref[:, c * HEAD_DIM:(c + 1) * HEAD_DIM] = yh.astype(o_ref.dtype)


def _proj(h, g, w, cos, sin, col0, q_width, rope_width, width, q_scale, out_dtype):
    n, d = h.shape
    seq = cos.shape[0]
    tm = PROJ_TILE // width
    assert n % tm == 0 and seq % tm == 0 and col0 % width == 0
    assert q_width % HEAD_DIM == 0 and rope_width % HEAD_DIM == 0 and width % HEAD_DIM == 0
    pos_blocks = seq // tm
    body = functools.partial(_proj_body, q_heads=q_width // HEAD_DIM,
                             k_heads=(rope_width - q_width) // HEAD_DIM, q_scale=q_scale)
    table_spec = pl.BlockSpec((tm, HEAD_DIM), lambda i: (i % pos_blocks, 0))
    vmem = 2 * tm * d * 4 + tm * d * 2 + d * width * 2 + 3 * tm * width * 4 + 4 * tm * HEAD_DIM * 4
    return pl.pallas_call(
        body,
        grid=(n // tm,),
        in_specs=[
            pl.BlockSpec((tm, d), lambda i: (i, 0)),
            pl.BlockSpec((1, d), lambda i: (0, 0)),
            pl.BlockSpec((d, width), lambda i: (0, col0 // width)),
            table_spec,
            table_spec,
        ],
        out_specs=pl.BlockSpec((tm, width), lambda i: (i, 0)),
        out_shape=jax.ShapeDtypeStruct((n, width), out_dtype),
        compiler_params=_params(("parallel",), vmem),
        name="proj",
    )(h, g.reshape(1, d), w, cos, sin)


def _rope_tables(seq):
    inv_freq = ROPE_THETA ** (-jnp.arange(HALF_DIM, dtype=F32) / HALF_DIM)
    ang = jnp.arange(seq, dtype=F32)[:, None] * inv_freq[None, :]
    cos, sin = jnp.cos(ang), jnp.sin(ang)
    return jnp.concatenate([cos, cos], axis=-1), jnp.concatenate([-sin, sin], axis=-1)


OPROJ_ROWS = 512
OPROJ_COLS = 2048


def _oproj_body(*refs):
    *aw, h_ref, o_ref = refs
    acc = h_ref[...]
    for a_ref, w_ref in zip(aw[0::2], aw[1::2]):
        acc = acc + jnp.dot(a_ref[...], w_ref[...], preferred_element_type=F32)
    o_ref[...] = acc


def _oproj(h, parts, w):
    n, d = h.shape
    tm, tn = OPROJ_ROWS, OPROJ_COLS
    assert n % tm == 0 and d % tn == 0
    in_specs, args, vmem, row0 = [], [], 0, 0
    for a in parts:
        kdim = a.shape[1]
        assert row0 % kdim == 0
        in_specs += [pl.BlockSpec((tm, kdim), lambda i, j: (i, 0)),
                     pl.BlockSpec((kdim, tn), lambda i, j, r=row0 // kdim: (r, j))]
        args += [a, w]
        vmem += 2 * (tm * kdim + kdim * tn) * 2
        row0 += kdim
    assert row0 == w.shape[0]
    in_specs.append(pl.BlockSpec((tm, tn), lambda i, j: (i, j)))
    args.append(h)
    vmem += 6 * tm * tn * 4
    return pl.pallas_call(
        _oproj_body,
        grid=(n // tm, d // tn),
        in_specs=in_specs,
        out_specs=pl.BlockSpec((tm, tn), lambda i, j: (i, j)),
        out_shape=jax.ShapeDtypeStruct((n, d), F32),
        compiler_params=_params(("parallel", "parallel"), vmem),
        name="oproj",
    )(*args)


def _dilated_body(q_ref, kc_ref, kp_ref, vc_ref, vp_ref, o_ref, kbuf, vbuf, *branch_bufs):
    obufs, lbufs = branch_bufs[:3], branch_bufs[3:]
    sb = pl.program_id(2)
    sup = DIL_SUPER
    kbuf[0:sup, :] = kp_ref[...]
    kbuf[sup:2 * sup, :] = kc_ref[...]
    vbuf[0:sup, :] = vp_ref[...]
    vbuf[sup:2 * sup, :] = vc_ref[...]

    row = lax.broadcasted_iota(jnp.int32, (DIL_BLOCK, 2 * DIL_BLOCK), 0)
    col = lax.broadcasted_iota(jnp.int32, (DIL_BLOCK, 2 * DIL_BLOCK), 1)
    band = jnp.logical_and(col >= row + (DIL_BLOCK - DIL_REACH), col <= row + DIL_BLOCK)

    for br, dil in enumerate(DILATIONS):
        nsub = sup // (DIL_BLOCK * dil)
        obuf, lbuf = obufs[br], lbufs[br]

        def block(t, carry, dil=dil, nsub=nsub, obuf=obuf, lbuf=lbuf):
            res = t // nsub
            sub = t % nsub
            q0 = res + dil * DIL_BLOCK * sub
            k0 = sup + q0 - dil * DIL_BLOCK
            if dil == 1:
                q0 = pl.multiple_of(q0, DIL_BLOCK)
                k0 = pl.multiple_of(k0, DIL_BLOCK)
                qs, ks = pl.ds(q0, DIL_BLOCK), pl.ds(k0, 2 * DIL_BLOCK)
            else:
                qs, ks = pl.ds(q0, DIL_BLOCK, stride=dil), pl.ds(k0, 2 * DIL_BLOCK, stride=dil)
            qb = q_ref[qs, :].astype(BF16)
            kb = kbuf[ks, :].astype(BF16)
            vb = vbuf[ks, :].astype(BF16)
            s = lax.dot_general(qb, kb, NT_DIMS, preferred_element_type=F32)
            has_prev = jnp.logical_or(sb > 0, sub > 0)
            first_col = jnp.where(has_prev, 0, DIL_BLOCK)
            s = jnp.where(jnp.logical_and(band, col >= first_col), s, NEG_INF)
            m = jnp.max(s, axis=-1, keepdims=True)
            p = jnp.exp2((s - m).astype(BF16))
            pv = jnp.dot(p, jnp.concatenate([vb, jnp.ones_like(vb)], axis=1), preferred_element_type=F32)
            den = pv[:, HEAD_DIM:]
            obuf[qs, :] = pv[:, :HEAD_DIM] / den
            lbuf[qs, :] = m + jnp.log2(den)
            return carry

        lax.fori_loop(0, nsub * dil, block, 0, unroll=DIL_UNROLL)

    lses = [lbuf[...] for lbuf in lbufs]
    top = jnp.maximum(jnp.maximum(lses[0], lses[1]), lses[2])
    wts = [jnp.exp2(l - top) for l in lses]
    mix = wts[0] * obufs[0][...] + wts[1] * obufs[1][...] + wts[2] * obufs[2][...]
    o_ref[...] = (mix / (wts[0] + wts[1] + wts[2])).astype(o_ref.dtype)


def _dilated(qkv, batch, seq):
    sup = DIL_SUPER
    assert seq % sup == 0
    nsb = seq // sup
    heads = N_HEADS_A

    def cur(col0):
        return pl.BlockSpec((sup, HEAD_DIM), lambda b, h, s: (b * nsb + s, col0 + h))

    def prev(col0):
        return pl.BlockSpec((sup, HEAD_DIM), lambda b, h, s: (b * nsb + jnp.maximum(s - 1, 0), col0 + h))

    blk = sup * HEAD_DIM * 4
    vmem = 2 * 5 * blk + 2 * sup * HEAD_DIM * 2 + 4 * blk + 6 * blk + 8 * blk
    return pl.pallas_call(
        _dilated_body,
        grid=(batch, heads, nsb),
        in_specs=[cur(0), cur(heads), prev(heads), cur(2 * heads), prev(2 * heads)],
        out_specs=pl.BlockSpec((sup, HEAD_DIM), lambda b, h, s: (b * nsb + s, h)),
        out_shape=jax.ShapeDtypeStruct((batch * seq, WIDTH_A), BF16),
        scratch_shapes=[pltpu.VMEM((2 * sup, HEAD_DIM), F32)] * 2 + [pltpu.VMEM((sup, HEAD_DIM), F32)] * 6,
        compiler_params=_params(("parallel", "parallel", "arbitrary"), vmem),
        name="dilated",
    )(qkv, qkv, qkv, qkv, qkv)


def _scores_t(k_blk, q):
    return lax.dot_general(k_blk, q, NT_DIMS, preferred_element_type=F32)


def _flash_t(streams, qi, tk, halves, diag_mask, past_mask):
    assert halves in (1, 2)
    tq = halves * tk
    first = halves * qi

    def key_tile(k_ref, j):
        return k_ref[pl.ds(pl.multiple_of(j * tk, tk), tk), :]

    for q, k_ref, _, p_ref, acc_ref, st_ref in streams:
        s = jnp.concatenate(
            [diag_mask(_scores_t(key_tile(k_ref, first + half), q[half * tk:(half + 1) * tk, :]), half)
             for half in range(halves)], axis=1).astype(BF16)
        m = jnp.max(s, axis=0, keepdims=True)
        p_ref[0] = jnp.exp2(s - m)
        acc_ref[...] = jnp.zeros(acc_ref.shape, F32)
        st_ref[0] = m.astype(F32)
        st_ref[1] = jnp.ones(m.shape, F32)

    def flush_own():
        for _, _, vt_ref, p_ref, acc_ref, st_ref in streams:
            pv = jnp.concatenate(
                [jnp.dot(vt_ref[first + half], p_ref[0, :, half * tk:(half + 1) * tk],
                         preferred_element_type=F32) for half in range(halves)], axis=1)
            acc_ref[...] = st_ref[1] * acc_ref[...] + pv

    def flush(pending, slot):
        for _, _, vt_ref, p_ref, acc_ref, st_ref in streams:
            acc_ref[...] = st_ref[1] * acc_ref[...] + jnp.dot(vt_ref[pending], p_ref[slot],
                                                              preferred_element_type=F32)

    def step(j, do_flush, wr, mask):
        scores = [_scores_t(key_tile(st[1], j), st[0]) for st in streams]
        do_flush()
        for s, (_, _, _, p_ref, _, st_ref) in zip(scores, streams):
            s = mask(s, j).astype(BF16)
            m = st_ref[0]
            m_new = jnp.maximum(m, jnp.max(s, axis=0, keepdims=True).astype(F32))
            p_ref[wr] = jnp.exp2(s - m_new.astype(BF16))
            st_ref[0] = m_new
            st_ref[1] = jnp.exp2(m - m_new)

    if halves == 2:
        def second_half_only(s, j):
            qry = lax.broadcasted_iota(jnp.int32, (tk, tq), 1)
            return jnp.where(qry >= tk, past_mask(s, j), NEG_INF)

        step(first, flush_own, 1, second_half_only)
    slot0 = halves - 1
    n_past = halves * qi

    def past_step(j, parity):
        rd = slot0 ^ parity
        step(j, lambda: flush(jnp.where(j == 0, first, j - 1), rd), 1 - rd, past_mask)

    def four(i, carry):
        for u in range(PAST_UNROLL):
            past_step(PAST_UNROLL * i + u, u % 2)
        return carry

    lax.fori_loop(0, n_past // PAST_UNROLL, four, 0)
    rest = n_past % PAST_UNROLL
    base = n_past - rest

    @pl.when(rest >= 2)
    def _():
        past_step(base, 0)
        past_step(base + 1, 1)

    last = jnp.where(n_past == 0, first, n_past - 1)
    if halves == 2:
        flush(last, slot0)
    else:
        @pl.when(rest % 2 == 1)
        def _():
            past_step(n_past - 1, 0)
            flush(last, 1 - slot0)

        @pl.when(rest % 2 == 0)
        def _():
            flush(last, slot0)

    outs = []
    for _, _, _, _, acc_ref, _ in streams:
        dv = acc_ref.shape[0] - ONES_ROWS
        outs.append((acc_ref[0:dv, :], acc_ref[dv:dv + 1, :]))
    return outs


def _store_transposed_blocks(v_ref, vt_ref):
    nblk, rows_t, rows = vt_ref.shape
    dv = rows_t - ONES_ROWS
    for n in range(nblk):
        vt_ref[n, 0:dv, :] = v_ref[n * rows:(n + 1) * rows, :].astype(F32).T.astype(vt_ref.dtype)
        vt_ref[n, dv:rows_t, :] = jnp.ones((ONES_ROWS, rows), vt_ref.dtype)


def _moba_body(q_ref, k_ref, v_ref, o_ref, kmean_hi, kmean_lo, vt_ref, sel_ref, p_ref, acc_ref, st_ref):
    qi = pl.program_id(2)
    blk, tk, tq = MOBA_BLOCK, ATTN_KEYS, MOBA_HALVES * ATTN_KEYS
    nb = k_ref.shape[0] // blk

    @pl.when(qi == 0)
    def _():
        for n in range(nb):
            mean = jnp.mean(k_ref[n * blk:(n + 1) * blk, :].astype(F32), axis=0, keepdims=True)
            hi = mean.astype(BF16)
            kmean_hi[n:n + 1, :] = hi
            kmean_lo[n:n + 1, :] = (mean - hi.astype(F32)).astype(BF16)
        _store_transposed_blocks(v_ref, vt_ref)

    q = q_ref[...]
    gate = _scores_t(kmean_hi[...], q) + _scores_t(kmean_lo[...], q)
    blk_id = lax.broadcasted_iota(jnp.int32, gate.shape, 0)
    q_blk = (tq // blk) * qi + lax.broadcasted_iota(jnp.int32, gate.shape, 1) // blk
    past = blk_id < q_blk
    work = jnp.where(past, gate, NEG_INF)
    chosen = jnp.zeros(gate.shape, F32)
    for _ in range(MOBA_TOPK):
        best = jnp.max(work, axis=0, keepdims=True)
        first = jnp.min(jnp.where(work == best, blk_id, nb), axis=0, keepdims=True)
        pick = blk_id == first
        chosen = jnp.where(pick, 1.0, chosen)
        work = jnp.where(pick, -jnp.inf, work)
    sel_ref[...] = jnp.where(past, chosen, 0.0)

    def picked(block):
        return sel_ref[pl.ds(block, 1), :] > 0.5

    def past_mask(s, j):
        return jnp.concatenate([jnp.where(picked(2 * j), s[:blk], NEG_INF),
                                jnp.where(picked(2 * j + 1), s[blk:], NEG_INF)], axis=0)

    def diag_mask(s, half):
        key = lax.broadcasted_iota(jnp.int32, (tk, tk), 0)
        qry = lax.broadcasted_iota(jnp.int32, (tk, tk), 1)
        first_picked = sel_ref[pl.ds(2 * (MOBA_HALVES * qi + half), 1), half * tk:(half + 1) * tk] > 0.5
        other = jnp.where(key < blk, jnp.where(first_picked, s, NEG_INF), NEG_INF)
        own_start = (qry // blk) * blk
        return jnp.where(key <= qry, jnp.where(key >= own_start, s, other), other)

    [(acc, l)] = _flash_t([(q, k_ref, vt_ref, p_ref, acc_ref, st_ref)], qi, tk, MOBA_HALVES, diag_mask, past_mask)
    o_ref[...] = (acc * (1.0 / l)).T.astype(o_ref.dtype)


def _moba(qkv, batch, seq):
    blk, tk, tq = MOBA_BLOCK, ATTN_KEYS, MOBA_HALVES * ATTN_KEYS
    assert seq % tq == 0
    nq = seq // tq
    nb = seq // blk
    heads = N_HEADS_B
    vmem = (2 * 2 * seq * HEAD_DIM * 2 + seq * HEAD_DIM * 2 + 4 * tq * HEAD_DIM * 2 + 2 * tk * tq * 2
            + 2 * HEAD_DIM * tq * 4 + 10 * tk * tq * 4)
    return pl.pallas_call(
        _moba_body,
        grid=(batch, heads, nq),
        in_specs=[
            pl.BlockSpec((tq, HEAD_DIM), lambda b, h, i: (b * nq + i, h)),
            pl.BlockSpec((seq, HEAD_DIM), lambda b, h, i: (b, heads + h)),
            pl.BlockSpec((seq, HEAD_DIM), lambda b, h, i: (b, 2 * heads + h)),
        ],
        out_specs=pl.BlockSpec((tq, HEAD_DIM), lambda b, h, i: (b * nq + i, h)),
        out_shape=jax.ShapeDtypeStruct((batch * seq, WIDTH_B), BF16),
        scratch_shapes=[pltpu.VMEM((nb, HEAD_DIM), BF16)] * 2
        + [pltpu.VMEM((seq // tk, HEAD_DIM + ONES_ROWS, tk), BF16), pltpu.VMEM((nb, tq), F32),
           pltpu.VMEM((2, tk, tq), BF16), pltpu.VMEM((HEAD_DIM + ONES_ROWS, tq), F32),
           pltpu.VMEM((2, 1, tq), F32)],
        compiler_params=_params(("parallel", "parallel", "arbitrary"), vmem),
        name="moba",
    )(qkv, qkv, qkv)


def _diff_body(lq1_ref, lk1_ref, lq2_ref, lk2_ref, g_ref, q1_ref, q2_ref, k1_ref, k2_ref, v_ref, o_ref,
               vt_ref, p1_ref, p2_ref, acc1_ref, acc2_ref, st1_ref, st2_ref, *, lambda_init):
    qi = pl.program_id(2)
    t = ATTN_KEYS

    @pl.when(qi == 0)
    def _():
        _store_transposed_blocks(v_ref, vt_ref)

    def diag_mask(s, half):
        key = lax.broadcasted_iota(jnp.int32, (t, t), 0)
        qry = lax.broadcasted_iota(jnp.int32, (t, t), 1)
        return jnp.where(key <= qry, s, NEG_INF)

    streams = [(q1_ref[...], k1_ref, vt_ref, p1_ref, acc1_ref, st1_ref),
               (q2_ref[...], k2_ref, vt_ref, p2_ref, acc2_ref, st2_ref)]
    (a1, l1), (a2, l2) = _flash_t(streams, qi, t, DIFF_HALVES, diag_mask, lambda s, j: s)

    lam = (jnp.exp(jnp.sum(lq1_ref[...] * lk1_ref[...], axis=-1, keepdims=True))
           - jnp.exp(jnp.sum(lq2_ref[...] * lk2_ref[...], axis=-1, keepdims=True)) + lambda_init)
    o = a1 * (1.0 / l1) - lam * (a2 * (1.0 / l2))
    o = o * lax.rsqrt(jnp.mean(o * o, axis=0, keepdims=True) + SUBLN_EPS)
    o_ref[...] = (o.T * g_ref[...] * (1.0 - lambda_init)).astype(o_ref.dtype)


def _diff(qkv, lq1, lk1, lq2, lk2, subln_g, lambda_init, batch, seq):
    t, tq = ATTN_KEYS, DIFF_HALVES * ATTN_KEYS
    assert seq % tq == 0
    nq = seq // tq
    heads = N_HEADS_C
    dv = 2 * HEAD_DIM
    kcol = D_MODEL // HEAD_DIM
    vcol = 2 * D_MODEL // dv

    def vec(width):
        return pl.BlockSpec((1, width), lambda b, h, i: (0, 0))

    def q_spec(part):
        return pl.BlockSpec((tq, HEAD_DIM), lambda b, h, i: (b * nq + i, 2 * h + part))

    def k_spec(part):
        return pl.BlockSpec((seq, HEAD_DIM), lambda b, h, i: (b, kcol + 2 * h + part))

    vmem = (2 * (2 * seq * HEAD_DIM + seq * dv) * 2 + seq * dv * 2 + 8 * tq * HEAD_DIM * 2 + 2 * tq * dv * 2
            + 2 * tq * dv * 4 + 4 * t * tq * 2 + 10 * t * tq * 4)
    return pl.pallas_call(
        functools.partial(_diff_body, lambda_init=lambda_init),
        grid=(batch, heads, nq),
        in_specs=[vec(HEAD_DIM)] * 4 + [vec(dv), q_spec(0), q_spec(1), k_spec(0), k_spec(1),
                                        pl.BlockSpec((seq, dv), lambda b, h, i: (b, vcol + h))],
        out_specs=pl.BlockSpec((tq, dv), lambda b, h, i: (b * nq + i, h)),
        out_shape=jax.ShapeDtypeStruct((batch * seq, heads * dv), BF16),
        scratch_shapes=[pltpu.VMEM((seq // t, dv + ONES_ROWS, t), BF16)] + [pltpu.VMEM((2, t, tq), BF16)] * 2
        + [pltpu.VMEM((dv + ONES_ROWS, tq), F32)] * 2 + [pltpu.VMEM((2, 1, tq), F32)] * 2,
        compiler_params=_params(("parallel", "parallel", "arbitrary"), vmem),
        name="diffattn",
    )(lq1.reshape(1, -1), lk1.reshape(1, -1), lq2.reshape(1, -1), lk2.reshape(1, -1),
      subln_g.reshape(1, -1), qkv, qkv, qkv, qkv, qkv)


def _lambda_init(layer):
    return 0.8 - 0.6 * math.exp(-0.3 * layer)


def kernel(x, ffa_norm, ffa_w_in, ffa_w_out, mix_norm, even_w_in, even_w_out, odd_w_in, odd_w_out,
           lambda_q1, lambda_k1, lambda_q2, lambda_k2, subln_norm, ffb_norm, ffb_w_in, ffb_w_out, final_norm):
    batch, seq, d = x.shape
    depth = ffa_norm.shape[0]
    cos, sin = _rope_tables(seq)
    h = x.reshape(batch * seq, d)
    for layer in range(depth):
        i = layer // 2
        h = _ffn(h, ffa_norm[layer], ffa_w_in, ffa_w_out, layer)
        if layer % 2 == 0:
            w_in = even_w_in[i].astype(BF16)
            wa, wb = 3 * WIDTH_A, 3 * WIDTH_B
            qkv_a = _proj(h, mix_norm[layer], w_in, cos, sin, 0, WIDTH_A, 2 * WIDTH_A, wa, SCALE * LOG2_E, F32)
            qkv_b = _proj(h, mix_norm[layer], w_in, cos, sin, wa, WIDTH_B, 2 * WIDTH_B, wb, SCALE * LOG2_E, BF16)
            o_a = _dilated(qkv_a, batch, seq)
            o_b = _moba(qkv_b, batch, seq)
            h = _oproj(h, [o_a, o_b], even_w_out[i].astype(BF16))
        else:
            qkv = _proj(h, mix_norm[layer], odd_w_in[i].astype(BF16), cos, sin,
                        0, D_MODEL, 2 * D_MODEL, 3 * D_MODEL, SCALE * LOG2_E, BF16)
            o = _diff(qkv, lambda_q1[i], lambda_k1[i], lambda_q2[i], lambda_k2[i], subln_norm[i],
                      _lambda_init(layer), batch, seq)
            h = _oproj(h, [o], odd_w_out[i].astype(BF16))
        last = layer == depth - 1
        h = _ffn(h, ffb_norm[layer], ffb_w_in, ffb_w_out, layer, final_norm if last else None)
    return h.reshape(batch, seq, d)
```

```python
import functools
import math

import jax
import jax.numpy as jnp
from jax import lax
from jax.experimental import pallas as pl
from jax.experimental.pallas import tpu as pltpu

F32 = jnp.float32
BF16 = jnp.bfloat16

D_MODEL = 2048
HEAD_DIM = 128
HALF_DIM = HEAD_DIM // 2
N_HEADS_A = 8
N_HEADS_B = 8
N_HEADS_C = 8
WIDTH_A = N_HEADS_A * HEAD_DIM
WIDTH_B = N_HEADS_B * HEAD_DIM
DILATIONS = (1, 4, 16)
DIL_BLOCK = 128
DIL_REACH = 128
DIL_SUPER = DIL_BLOCK * DILATIONS[-1]
DIL_UNROLL = 16
MOBA_BLOCK = 256
MOBA_TOPK = 3
ATTN_KEYS = 512
MOBA_HALVES = 2
DIFF_HALVES = 1
ROPE_THETA = 10000.0
NORM_EPS = 1e-6
SUBLN_EPS = 1e-5
NEG_INF = -1e30
SCALE = HEAD_DIM ** -0.5
LOG2_E = math.log2(math.e)

VMEM_V7X_BYTES = 64 * 1024 * 1024
VMEM_TEMP_BYTES = 8 * 1024 * 1024
NT_DIMS = (((1,), (1,)), ((), ()))
ONES_ROWS = 16
PAST_UNROLL = 4
SOFTMAX_LANES = 128


def _params(semantics, buffer_bytes):
    vmem_bytes = buffer_bytes + VMEM_TEMP_BYTES
    assert vmem_bytes < VMEM_V7X_BYTES
    return pltpu.CompilerParams(dimension_semantics=semantics, vmem_limit_bytes=int(vmem_bytes))


def _rms(x, g, eps):
    return x * lax.rsqrt(jnp.mean(x * x, axis=-1, keepdims=True) + eps) * g


FFN_ROWS = 1024
FFN_COLS = 512


def _ffn_body(h_hbm, g_ref, wg_ref, wu_ref, wo_ref, *rest, final):
    if final:
        fg_ref, o_ref, xn_ref, x_buf, x_sem = rest
    else:
        o_ref, xn_ref, x_buf, x_sem = rest
    i, j = pl.program_id(0), pl.program_id(1)
    rows = x_buf.shape[0]

    def x_copy(tile):
        return pltpu.make_async_copy(h_hbm.at[pl.ds(tile * rows, rows), :], x_buf, x_sem)

    @pl.when(j == 0)
    def _():
        @pl.when(i == 0)
        def _():
            x_copy(i).start()

        x_copy(i).wait()
        x = x_buf[...]
        xn_ref[...] = _rms(x, g_ref[...], NORM_EPS).astype(BF16)
        o_ref[...] = x

    @pl.when(jnp.logical_and(j == 1, i + 1 < pl.num_programs(0)))
    def _():
        x_copy(i + 1).start()

    xn = xn_ref[...]
    gate = jnp.dot(xn, wg_ref[...].astype(BF16), preferred_element_type=F32)
    up = jnp.dot(xn, wu_ref[...].astype(BF16), preferred_element_type=F32)
    act = (0.5 * gate / (1.0 + jnp.exp(-gate))) * up
    o_ref[...] += jnp.dot(act.astype(BF16), wo_ref[...].astype(BF16), preferred_element_type=F32)

    if final:
        @pl.when(j == pl.num_programs(1) - 1)
        def _():
            o_ref[...] = _rms(o_ref[...], fg_ref[...], NORM_EPS)


def _ffn(h, g, w_in, w_out, layer, final_g=None):
    n, d = h.shape
    d_ff = w_out.shape[1]
    tm, tf = FFN_ROWS, FFN_COLS
    nff = d_ff // tf
    assert n % tm == 0 and d_ff % tf == 0 and nff >= 2
    final = final_g is not None
    in_specs = [
        pl.BlockSpec(memory_space=pl.ANY),
        pl.BlockSpec((1, d), lambda i, j: (0, 0)),
        pl.BlockSpec((None, d, tf), lambda i, j: (layer, 0, j)),
        pl.BlockSpec((None, d, tf), lambda i, j: (layer, 0, nff + j)),
        pl.BlockSpec((None, tf, d), lambda i, j: (layer, j, 0)),
    ]
    args = [h, g.reshape(1, d), w_in, w_in, w_out]
    if final:
        in_specs.append(pl.BlockSpec((1, d), lambda i, j: (0, 0)))
        args.append(final_g.reshape(1, d))
    vmem = 3 * tm * d * 4 + tm * d * 2 + 2 * (2 * d * tf + tf * d) * 4
    return pl.pallas_call(
        functools.partial(_ffn_body, final=final),
        grid=(n // tm, nff),
        in_specs=in_specs,
        out_specs=pl.BlockSpec((tm, d), lambda i, j: (i, 0)),
        out_shape=jax.ShapeDtypeStruct((n, d), F32),
        scratch_shapes=[pltpu.VMEM((tm, d), BF16), pltpu.VMEM((tm, d), F32), pltpu.SemaphoreType.DMA(())],
        compiler_params=_params(("arbitrary", "arbitrary"), vmem),
        name="ffn",
    )(*args)


PROJ_TILE = 512 * 3072


def _proj_body(x_ref, g_ref, w_ref, cos_ref, sin_ref, o_ref, *, q_heads, k_heads, q_scale):
    xn = _rms(x_ref[...], g_ref[...], NORM_EPS).astype(BF16)
    y = jnp.dot(xn, w_ref[...], preferred_element_type=F32)
    cos, sin = cos_ref[...], sin_ref[...]
    cos_q, sin_q = cos * q_scale, sin * q_scale
    for c in range(y.shape[1] // HEAD_DIM):
        yh = y[:, c * HEAD_DIM:(c + 1) * HEAD_DIM]
        if c < q_heads:
            yh = yh * cos_q + pltpu.roll(yh, HALF_DIM, 1) * sin_q
        elif c < q_heads + k_heads:
            yh = yh * cos + pltpu.roll(yh, HALF_DIM, 1) * sin
        o_ref[:, c * HEAD_DIM:(c + 1) * HEAD_DIM] = yh.astype(o_ref.dtype)


def _proj(h, g, w, cos, sin, col0, q_width, rope_width, width, q_scale, out_dtype):
    n, d = h.shape
    seq = cos.shape[0]
    tm = PROJ_TILE // width
    assert n % tm == 0 and seq % tm == 0 and col0 % width == 0
    assert q_width % HEAD_DIM == 0 and rope_width % HEAD_DIM == 0 and width % HEAD_DIM == 0
    pos_blocks = seq // tm
    body = functools.partial(_proj_body, q_heads=q_width // HEAD_DIM,
                             k_heads=(rope_width - q_width) // HEAD_DIM, q_scale=q_scale)
    table_spec = pl.BlockSpec((tm, HEAD_DIM), lambda i: (i % pos_blocks, 0))
    vmem = 2 * tm * d * 4 + tm * d * 2 + d * width * 2 + 3 * tm * width * 4 + 4 * tm * HEAD_DIM * 4
    return pl.pallas_call(
        body,
        grid=(n // tm,),
        in_specs=[
            pl.BlockSpec((tm, d), lambda i: (i, 0)),
            pl.BlockSpec((1, d), lambda i: (0, 0)),
            pl.BlockSpec((d, width), lambda i: (0, col0 // width)),
            table_spec,
            table_spec,
        ],
        out_specs=pl.BlockSpec((tm, width), lambda i: (i, 0)),
        out_shape=jax.ShapeDtypeStruct((n, width), out_dtype),
        compiler_params=_params(("parallel",), vmem),
        name="proj",
    )(h, g.reshape(1, d), w, cos, sin)


def _rope_tables(seq):
    inv_freq = ROPE_THETA ** (-jnp.arange(HALF_DIM, dtype=F32) / HALF_DIM)
    ang = jnp.arange(seq, dtype=F32)[:, None] * inv_freq[None, :]
    cos, sin = jnp.cos(ang), jnp.sin(ang)
    return jnp.concatenate([cos, cos], axis=-1), jnp.concatenate([-sin, sin], axis=-1)


OPROJ_ROWS = 512
OPROJ_COLS = 2048


def _oproj_body(*refs):
    *aw, h_ref, o_ref = refs
    acc = h_ref[...]
    for a_ref, w_ref in zip(aw[0::2], aw[1::2]):
        acc = acc + jnp.dot(a_ref[...], w_ref[...], preferred_element_type=F32)
    o_ref[...] = acc


def _oproj(h, parts, w):
    n, d = h.shape
    tm, tn = OPROJ_ROWS, OPROJ_COLS
    assert n % tm == 0 and d % tn == 0
    in_specs, args, vmem, row0 = [], [], 0, 0
    for a in parts:
        kdim = a.shape[1]
        assert row0 % kdim == 0
        in_specs += [pl.BlockSpec((tm, kdim), lambda i, j: (i, 0)),
                     pl.BlockSpec((kdim, tn), lambda i, j, r=row0 // kdim: (r, j))]
        args += [a, w]
        vmem += 2 * (tm * kdim + kdim * tn) * 2
        row0 += kdim
    assert row0 == w.shape[0]
    in_specs.append(pl.BlockSpec((tm, tn), lambda i, j: (i, j)))
    args.append(h)
    vmem += 6 * tm * tn * 4
    return pl.pallas_call(
        _oproj_body,
        grid=(n // tm, d // tn),
        in_specs=in_specs,
        out_specs=pl.BlockSpec((tm, tn), lambda i, j: (i, j)),
        out_shape=jax.ShapeDtypeStruct((n, d), F32),
        compiler_params=_params(("parallel", "parallel"), vmem),
        name="oproj",
    )(*args)


def _dilated_body(q_ref, kc_ref, kp_ref, vc_ref, vp_ref, o_ref, kbuf, vbuf, *branch_bufs):
    obufs, lbufs = branch_bufs[:3], branch_bufs[3:]
    sb = pl.program_id(2)
    sup = DIL_SUPER
    kbuf[0:sup, :] = kp_ref[...]
    kbuf[sup:2 * sup, :] = kc_ref[...]
    vbuf[0:sup, :] = vp_ref[...]
    vbuf[sup:2 * sup, :] = vc_ref[...]

    row = lax.broadcasted_iota(jnp.int32, (DIL_BLOCK, 2 * DIL_BLOCK), 0)
    col = lax.broadcasted_iota(jnp.int32, (DIL_BLOCK, 2 * DIL_BLOCK), 1)
    band = jnp.logical_and(col >= row + (DIL_BLOCK - DIL_REACH), col <= row + DIL_BLOCK)

    for br, dil in enumerate(DILATIONS):
        nsub = sup // (DIL_BLOCK * dil)
        obuf, lbuf = obufs[br], lbufs[br]

        def block(t, carry, dil=dil, nsub=nsub, obuf=obuf, lbuf=lbuf):
            res = t // nsub
            sub = t % nsub
            q0 = res + dil * DIL_BLOCK * sub
            k0 = sup + q0 - dil * DIL_BLOCK
            if dil == 1:
                q0 = pl.multiple_of(q0, DIL_BLOCK)
                k0 = pl.multiple_of(k0, DIL_BLOCK)
                qs, ks = pl.ds(q0, DIL_BLOCK), pl.ds(k0, 2 * DIL_BLOCK)
            else:
                qs, ks = pl.ds(q0, DIL_BLOCK, stride=dil), pl.ds(k0, 2 * DIL_BLOCK, stride=dil)
            qb = q_ref[qs, :].astype(BF16)
            kb = kbuf[ks, :].astype(BF16)
            vb = vbuf[ks, :].astype(BF16)
            s = lax.dot_general(qb, kb, NT_DIMS, preferred_element_type=F32)
            has_prev = jnp.logical_or(sb > 0, sub > 0)
            first_col = jnp.where(has_prev, 0, DIL_BLOCK)
            s = jnp.where(jnp.logical_and(band, col >= first_col), s, NEG_INF)
            m = jnp.max(s, axis=-1, keepdims=True)
            p = jnp.exp2((s - m).astype(BF16))
            pv = jnp.dot(p, jnp.concatenate([vb, jnp.ones_like(vb)], axis=1), preferred_element_type=F32)
            den = pv[:, HEAD_DIM:]
            obuf[qs, :] = pv[:, :HEAD_DIM] / den
            lbuf[qs, :] = m + jnp.log2(den)
            return carry

        lax.fori_loop(0, nsub * dil, block, 0, unroll=DIL_UNROLL)

    lses = [lbuf[...] for lbuf in lbufs]
    top = jnp.maximum(jnp.maximum(lses[0], lses[1]), lses[2])
    wts = [jnp.exp2(l - top) for l in lses]
    mix = wts[0] * obufs[0][...] + wts[1] * obufs[1][...] + wts[2] * obufs[2][...]
    o_ref[...] = (mix / (wts[0] + wts[1] + wts[2])).astype(o_ref.dtype)


def _dilated(qkv, batch, seq):
    sup = DIL_SUPER
    assert seq % sup == 0
    nsb = seq // sup
    heads = N_HEADS_A

    def cur(col0):
        return pl.BlockSpec((sup, HEAD_DIM), lambda b, h, s: (b * nsb + s, col0 + h))

    def prev(col0):
        return pl.BlockSpec((sup, HEAD_DIM), lambda b, h, s: (b * nsb + jnp.maximum(s - 1, 0), col0 + h))

    blk = sup * HEAD_DIM * 4
    vmem = 2 * 5 * blk + 2 * sup * HEAD_DIM * 2 + 4 * blk + 6 * blk + 8 * blk
    return pl.pallas_call(
        _dilated_body,
        grid=(batch, heads, nsb),
        in_specs=[cur(0), cur(heads), prev(heads), cur(2 * heads), prev(2 * heads)],
        out_specs=pl.BlockSpec((sup, HEAD_DIM), lambda b, h, s: (b * nsb + s, h)),
        out_shape=jax.ShapeDtypeStruct((batch * seq, WIDTH_A), BF16),
        scratch_shapes=[pltpu.VMEM((2 * sup, HEAD_DIM), F32)] * 2 + [pltpu.VMEM((sup, HEAD_DIM), F32)] * 6,
        compiler_params=_params(("parallel", "parallel", "arbitrary"), vmem),
        name="dilated",
    )(qkv, qkv, qkv, qkv, qkv)


def _scores_t(k_blk, q):
    return lax.dot_general(k_blk, q, NT_DIMS, preferred_element_type=F32)


def _flash_t(streams, qi, tk, halves, diag_mask, past_mask):
    assert halves in (1, 2)
    tq = halves * tk
    first = halves * qi

    def key_tile(k_ref, j):
        return k_ref[pl.ds(pl.multiple_of(j * tk, tk), tk), :]

    for q, k_ref, _, p_ref, acc_ref, st_ref in streams:
        s = jnp.concatenate(
            [diag_mask(_scores_t(key_tile(k_ref, first + half), q[half * tk:(half + 1) * tk, :]), half)
             for half in range(halves)], axis=1).astype(BF16)
        m = jnp.max(s, axis=0, keepdims=True)
        p_ref[0] = jnp.exp2(s - m)
        acc_ref[...] = jnp.zeros(acc_ref.shape, F32)
        st_ref[0] = m.astype(F32)
        st_ref[1] = jnp.ones(m.shape, F32)

    def flush_own():
        for _, _, vt_ref, p_ref, acc_ref, st_ref in streams:
            pv = jnp.concatenate(
                [jnp.dot(vt_ref[first + half], p_ref[0, :, half * tk:(half + 1) * tk],
                         preferred_element_type=F32) for half in range(halves)], axis=1)
            acc_ref[...] = st_ref[1] * acc_ref[...] + pv

    def flush(pending, slot):
        for _, _, vt_ref, p_ref, acc_ref, st_ref in streams:
            acc_ref[...] = st_ref[1] * acc_ref[...] + jnp.dot(vt_ref[pending], p_ref[slot],
                                                              preferred_element_type=F32)

    def step(j, do_flush, wr, mask):
        scores = [_scores_t(key_tile(st[1], j), st[0]) for st in streams]
        do_flush()
        for s, (_, _, _, p_ref, _, st_ref) in zip(scores, streams):
            s = mask(s, j).astype(BF16)
            for c0 in range(0, tq, SOFTMAX_LANES):
                cols = slice(c0, c0 + SOFTMAX_LANES)
                part = s[:, cols]
                m = st_ref[0, :, cols]
                m_new = jnp.maximum(m, jnp.max(part, axis=0, keepdims=True).astype(F32))
                p_ref[wr, :, cols] = jnp.exp2(part - m_new.astype(BF16))
                st_ref[0, :, cols] = m_new
                st_ref[1, :, cols] = jnp.exp2(m - m_new)

    if halves == 2:
        def second_half_only(s, j):
            qry = lax.broadcasted_iota(jnp.int32, (tk, tq), 1)
            return jnp.where(qry >= tk, past_mask(s, j), NEG_INF)

        step(first, flush_own, 1, second_half_only)
    slot0 = halves - 1
    n_past = halves * qi

    def past_step(j, parity):
        rd = slot0 ^ parity
        step(j, lambda: flush(jnp.where(j == 0, first, j - 1), rd), 1 - rd, past_mask)

    def four(i, carry):
        for u in range(PAST_UNROLL):
            past_step(PAST_UNROLL * i + u, u % 2)
        return carry

    lax.fori_loop(0, n_past // PAST_UNROLL, four, 0)
    rest = n_past % PAST_UNROLL
    base = n_past - rest

    @pl.when(rest >= 2)
    def _():
        past_step(base, 0)
        past_step(base + 1, 1)

    last = jnp.where(n_past == 0, first, n_past - 1)
    if halves == 2:
        flush(last, slot0)
    else:
        @pl.when(rest % 2 == 1)
        def _():
            past_step(n_past - 1, 0)
            flush(last, 1 - slot0)

        @pl.when(rest % 2 == 0)
        def _():
            flush(last, slot0)

    outs = []
    for _, _, _, _, acc_ref, _ in streams:
        dv = acc_ref.shape[0] - ONES_ROWS
        outs.append((acc_ref[0:dv, :], acc_ref[dv:dv + 1, :]))
    return outs


def _store_transposed_blocks(v_ref, vt_ref):
    nblk, rows_t, rows = vt_ref.shape
    dv = rows_t - ONES_ROWS
    for n in range(nblk):
        vt_ref[n, 0:dv, :] = v_ref[n * rows:(n + 1) * rows, :].astype(F32).T.astype(vt_ref.dtype)
        vt_ref[n, dv:rows_t, :] = jnp.ones((ONES_ROWS, rows), vt_ref.dtype)


def _moba_body(q_ref, k_ref, v_ref, o_ref, kmean_hi, kmean_lo, vt_ref, sel_ref, p_ref, acc_ref, st_ref):
    qi = pl.program_id(2)
    blk, tk, tq = MOBA_BLOCK, ATTN_KEYS, MOBA_HALVES * ATTN_KEYS
    nb = k_ref.shape[0] // blk

    @pl.when(qi == 0)
    def _():
        for n in range(nb):
            mean = jnp.mean(k_ref[n * blk:(n + 1) * blk, :].astype(F32), axis=0, keepdims=True)
            hi = mean.astype(BF16)
            kmean_hi[n:n + 1, :] = hi
            kmean_lo[n:n + 1, :] = (mean - hi.astype(F32)).astype(BF16)
        _store_transposed_blocks(v_ref, vt_ref)

    q = q_ref[...]
    gate = _scores_t(kmean_hi[...], q) + _scores_t(kmean_lo[...], q)
    blk_id = lax.broadcasted_iota(jnp.int32, gate.shape, 0)
    q_blk = (tq // blk) * qi + lax.broadcasted_iota(jnp.int32, gate.shape, 1) // blk
    past = blk_id < q_blk
    work = jnp.where(past, gate, NEG_INF)
    chosen = jnp.zeros(gate.shape, F32)
    for _ in range(MOBA_TOPK):
        best = jnp.max(work, axis=0, keepdims=True)
        first = jnp.min(jnp.where(work == best, blk_id, nb), axis=0, keepdims=True)
        pick = blk_id == first
        chosen = jnp.where(pick, 1.0, chosen)
        work = jnp.where(pick, -jnp.inf, work)
    sel_ref[...] = jnp.where(past, chosen, 0.0)

    def picked(block):
        return sel_ref[pl.ds(block, 1), :] > 0.5

    def past_mask(s, j):
        return jnp.concatenate([jnp.where(picked(2 * j), s[:blk], NEG_INF),
                                jnp.where(picked(2 * j + 1), s[blk:], NEG_INF)], axis=0)

    def diag_mask(s, half):
        key = lax.broadcasted_iota(jnp.int32, (tk, tk), 0)
        qry = lax.broadcasted_iota(jnp.int32, (tk, tk), 1)
        first_picked = sel_ref[pl.ds(2 * (MOBA_HALVES * qi + half), 1), half * tk:(half + 1) * tk] > 0.5
        other = jnp.where(key < blk, jnp.where(first_picked, s, NEG_INF), NEG_INF)
        own_start = (qry // blk) * blk
        return jnp.where(key <= qry, jnp.where(key >= own_start, s, other), other)

    [(acc, l)] = _flash_t([(q, k_ref, vt_ref, p_ref, acc_ref, st_ref)], qi, tk, MOBA_HALVES, diag_mask, past_mask)
    o_ref[...] = (acc * (1.0 / l)).T.astype(o_ref.dtype)


def _moba(qkv, batch, seq):
    blk, tk, tq = MOBA_BLOCK, ATTN_KEYS, MOBA_HALVES * ATTN_KEYS
    assert seq % tq == 0
    nq = seq // tq
    nb = seq // blk
    heads = N_HEADS_B
    vmem = (2 * 2 * seq * HEAD_DIM * 2 + seq * HEAD_DIM * 2 + 4 * tq * HEAD_DIM * 2 + 2 * tk * tq * 2
            + 2 * HEAD_DIM * tq * 4 + 10 * tk * tq * 4)
    return pl.pallas_call(
        _moba_body,
        grid=(batch, heads, nq),
        in_specs=[
            pl.BlockSpec((tq, HEAD_DIM), lambda b, h, i: (b * nq + i, h)),
            pl.BlockSpec((seq, HEAD_DIM), lambda b, h, i: (b, heads + h)),
            pl.BlockSpec((seq, HEAD_DIM), lambda b, h, i: (b, 2 * heads + h)),
        ],
        out_specs=pl.BlockSpec((tq, HEAD_DIM), lambda b, h, i: (b * nq + i, h)),
        out_shape=jax.ShapeDtypeStruct((batch * seq, WIDTH_B), BF16),
        scratch_shapes=[pltpu.VMEM((nb, HEAD_DIM), BF16)] * 2
        + [pltpu.VMEM((seq // tk, HEAD_DIM + ONES_ROWS, tk), BF16), pltpu.VMEM((nb, tq), F32),
           pltpu.VMEM((2, tk, tq), BF16), pltpu.VMEM((HEAD_DIM + ONES_ROWS, tq), F32),
           pltpu.VMEM((2, 1, tq), F32)],
        compiler_params=_params(("parallel", "parallel", "arbitrary"), vmem),
        name="moba",
    )(qkv, qkv, qkv)


def _diff_body(lq1_ref, lk1_ref, lq2_ref, lk2_ref, g_ref, q1_ref, q2_ref, k1_ref, k2_ref, v_ref, o_ref,
               vt_ref, p1_ref, p2_ref, acc1_ref, acc2_ref, st1_ref, st2_ref, *, lambda_init):
    qi = pl.program_id(2)
    t = ATTN_KEYS

    @pl.when(qi == 0)
    def _():
        _store_transposed_blocks(v_ref, vt_ref)

    def diag_mask(s, half):
        key = lax.broadcasted_iota(jnp.int32, (t, t), 0)
        qry = lax.broadcasted_iota(jnp.int32, (t, t), 1)
        return jnp.where(key <= qry, s, NEG_INF)

    streams = [(q1_ref[...], k1_ref, vt_ref, p1_ref, acc1_ref, st1_ref),
               (q2_ref[...], k2_ref, vt_ref, p2_ref, acc2_ref, st2_ref)]
    (a1, l1), (a2, l2) = _flash_t(streams, qi, t, DIFF_HALVES, diag_mask, lambda s, j: s)

    lam = (jnp.exp(jnp.sum(lq1_ref[...] * lk1_ref[...], axis=-1, keepdims=True))
           - jnp.exp(jnp.sum(lq2_ref[...] * lk2_ref[...], axis=-1, keepdims=True)) + lambda_init)
    o = a1 * (1.0 / l1) - lam * (a2 * (1.0 / l2))
    o = o * lax.rsqrt(jnp.mean(o * o, axis=0, keepdims=True) + SUBLN_EPS)
    o_ref[...] = (o.T * g_ref[...] * (1.0 - lambda_init)).astype(o_ref.dtype)


def _diff(qkv, lq1, lk1, lq2, lk2, subln_g, lambda_init, batch, seq):
    t, tq = ATTN_KEYS, DIFF_HALVES * ATTN_KEYS
    assert seq % tq == 0
    nq = seq // tq
    heads = N_HEADS_C
    dv = 2 * HEAD_DIM
    kcol = D_MODEL // HEAD_DIM
    vcol = 2 * D_MODEL // dv

    def vec(width):
        return pl.BlockSpec((1, width), lambda b, h, i: (0, 0))

    def q_spec(part):
        return pl.BlockSpec((tq, HEAD_DIM), lambda b, h, i: (b * nq + i, 2 * h + part))

    def k_spec(part):
        return pl.BlockSpec((seq, HEAD_DIM), lambda b, h, i: (b, kcol + 2 * h + part))

    vmem = (2 * (2 * seq * HEAD_DIM + seq * dv) * 2 + seq * dv * 2 + 8 * tq * HEAD_DIM * 2 + 2 * tq * dv * 2
            + 2 * tq * dv * 4 + 4 * t * tq * 2 + 10 * t * tq * 4)
    return pl.pallas_call(
        functools.partial(_diff_body, lambda_init=lambda_init),
        grid=(batch, heads, nq),
        in_specs=[vec(HEAD_DIM)] * 4 + [vec(dv), q_spec(0), q_spec(1), k_spec(0), k_spec(1),
                                        pl.BlockSpec((seq, dv), lambda b, h, i: (b, vcol + h))],
        out_specs=pl.BlockSpec((tq, dv), lambda b, h, i: (b * nq + i, h)),
        out_shape=jax.ShapeDtypeStruct((batch * seq, heads * dv), BF16),
        scratch_shapes=[pltpu.VMEM((seq // t, dv + ONES_ROWS, t), BF16)] + [pltpu.VMEM((2, t, tq), BF16)] * 2
        + [pltpu.VMEM((dv + ONES_ROWS, tq), F32)] * 2 + [pltpu.VMEM((2, 1, tq), F32)] * 2,
        compiler_params=_params(("parallel", "parallel", "arbitrary"), vmem),
        name="diffattn",
    )(lq1.reshape(1, -1), lk1.reshape(1, -1), lq2.reshape(1, -1), lk2.reshape(1, -1),
      subln_g.reshape(1, -1), qkv, qkv, qkv, qkv, qkv)


def _lambda_init(layer):
    return 0.8 - 0.6 * math.exp(-0.3 * layer)


def kernel(x, ffa_norm, ffa_w_in, ffa_w_out, mix_norm, even_w_in, even_w_out, odd_w_in, odd_w_out,
           lambda_q1, lambda_k1, lambda_q2, lambda_k2, subln_norm, ffb_norm, ffb_w_in, ffb_w_out, final_norm):
    batch, seq, d = x.shape
    depth = ffa_norm.shape[0]
    cos, sin = _rope_tables(seq)
    h = x.reshape(batch * seq, d)
    for layer in range(depth):
        i = layer // 2
        h = _ffn(h, ffa_norm[layer], ffa_w_in, ffa_w_out, layer)
        if layer % 2 == 0:
            w_in = even_w_in[i].astype(BF16)
            wa, wb = 3 * WIDTH_A, 3 * WIDTH_B
            qkv_a = _proj(h, mix_norm[layer], w_in, cos, sin, 0, WIDTH_A, 2 * WIDTH_A, wa, SCALE * LOG2_E, F32)
            qkv_b = _proj(h, mix_norm[layer], w_in, cos, sin, wa, WIDTH_B, 2 * WIDTH_B, wb, SCALE * LOG2_E, BF16)
            o_a = _dilated(qkv_a, batch, seq)
            o_b = _moba(qkv_b, batch, seq)
            h = _oproj(h, [o_a, o_b], even_w_out[i].astype(BF16))
        else:
            qkv = _proj(h, mix_norm[layer], odd_w_in[i].astype(BF16), cos, sin,
                        0, D_MODEL, 2 * D_MODEL, 3 * D_MODEL, SCALE * LOG2_E, BF16)
            o = _diff(qkv, lambda_q1[i], lambda_k1[i], lambda_q2[i], lambda_k2[i], subln_norm[i],
                      _lambda_init(layer), batch, seq)
            h = _oproj(h, [o], odd_w_out[i].astype(BF16))
        last = layer == depth - 1
        h = _ffn(h, ffb_norm[layer], ffb_w_in, ffb_w_out, layer, final_norm if last else None)
    return h.reshape(batch, seq, d)
```
